```python
import math
import jax, jax.numpy as jnp
from jax import lax
import numpy as np

D_MODEL = 1024
BATCH = 8
SEQ = 4096
DEPTH = 2
DEC_BATCH = 8
DEC_SEQ = 8192
PAST_LEN = 128

N_MEM = 256
EPS = 1e-6
D_FF = 2816
FNET_HEADS = 4
FNET_HEAD_DIM = D_MODEL // FNET_HEADS
FNET_WIDTH = FNET_HEADS * FNET_HEAD_DIM
SSM_HEAD_DIM = 64
SSM_INNER = 2 * D_MODEL
SSM_HEADS = SSM_INNER // SSM_HEAD_DIM
SSM_STATE = 128
SSM_GROUPS = 4
SSM_CONV = 5
SSM_CHUNK = 128
SSM_CONV_CH = SSM_INNER + 2 * SSM_GROUPS * SSM_STATE
DT_MIN = 0.001
DT_MAX = 0.1
AB_IN = FNET_WIDTH + SSM_INNER + SSM_CONV_CH + 2 * SSM_HEADS
AB_OUT = FNET_WIDTH + SSM_INNER
POOL_WINDOWS = (2, 4, 8, 16)
POOL_GROUPS = len(POOL_WINDOWS)
POOL_GROUP_DIM = D_MODEL // POOL_GROUPS
POOL_WIDTH = POOL_GROUPS * POOL_GROUP_DIM
DIFF_HEADS = 8
DIFF_HEAD_DIM = 64
DIFF_V_DIM = 2 * DIFF_HEAD_DIM
DIFF_QK_WIDTH = DIFF_HEADS * 2 * DIFF_HEAD_DIM
DIFF_V_WIDTH = DIFF_HEADS * DIFF_V_DIM
ROPE_THETA = 500000.0
ROT_DIM = DIFF_HEAD_DIM // 4
Q_BLOCK = 128
CD_IN = POOL_WIDTH + 2 * DIFF_QK_WIDTH + DIFF_V_WIDTH
CD_OUT = POOL_WIDTH + DIFF_V_WIDTH
CROSS_HEADS = 4
CROSS_HEAD_DIM = D_MODEL // CROSS_HEADS
N_EVEN = (DEPTH + 1) // 2
N_ODD = DEPTH // 2

kernel_name = "hybrid_bidir_fnet_ssd_pool_diffattn_encoder"


def _rms_norm(x, g):
    xf = x.astype(jnp.float32)
    y = xf * lax.rsqrt(jnp.mean(xf * xf, axis=-1, keepdims=True) + EPS)
    return (y * g.astype(jnp.float32)).astype(x.dtype)


def _swiglu(x, w_gate, w_up, w_down):
    return (jax.nn.silu(x @ w_gate) * (x @ w_up)) @ w_down


def _lambda_init(layer_idx):
    return 0.8 - 0.6 * math.exp(-0.3 * layer_idx)


def _rope_tables(length):
    inv = ROPE_THETA ** (-jnp.arange(0, ROT_DIM, 2, dtype=jnp.float32) / ROT_DIM)
    ang = jnp.arange(length, dtype=jnp.float32)[:, None] * inv[None, :]
    return jnp.cos(ang), jnp.sin(ang)


def _partial_rope(t, cos, sin):
    half = ROT_DIM // 2
    c = cos[None, :, None, None, :].astype(t.dtype)
    s = sin[None, :, None, None, :].astype(t.dtype)
    t1 = t[..., :half]
    t2 = t[..., half:ROT_DIM]
    return jnp.concatenate([t1 * c - t2 * s, t2 * c + t1 * s, t[..., ROT_DIM:]], axis=-1)


def _ssd_scan(x, dt, a, b_in, c_in):
    nb, length, nh, hp = x.shape
    ng, ns = b_in.shape[2], b_in.shape[3]
    hpg = nh // ng
    nc = length // SSM_CHUNK

    def chunks(t):
        return jnp.moveaxis(t.reshape((nb, nc, SSM_CHUNK) + t.shape[2:]), 1, 0)

    lower = jnp.tril(jnp.ones((SSM_CHUNK, SSM_CHUNK), dtype=bool))

    def step(state, inp):
        xc, dtc, bc, cc = inp
        acum = jnp.cumsum(dtc * a, axis=1)
        bh = jnp.repeat(bc, hpg, axis=2)
        ch = jnp.repeat(cc, hpg, axis=2)
        ah = jnp.swapaxes(acum, 1, 2)
        seg = ah[..., :, None] - ah[..., None, :]
        decay = jnp.exp(jnp.where(lower, seg, -jnp.inf))
        xdt = xc * dtc[..., None]
        scores = jnp.einsum('blhn,bshn->bhls', ch, bh) * decay
        y = jnp.einsum('bhls,bshp->blhp', scores, xdt)
        y = y + jnp.einsum('blhn,bhpn->blhp', ch, state) * jnp.exp(acum)[..., None]
        last = acum[:, -1]
        w_end = jnp.exp(last[:, None, :] - acum)
        state = state * jnp.exp(last)[:, :, None, None] + jnp.einsum('bsh,bshn,bshp->bhpn', w_end, bh, xdt)
        return state, y

    s0 = jnp.zeros((nb, nh, hp, ns), jnp.float32)
    _, ys = lax.scan(step, s0, (chunks(x), chunks(dt), chunks(b_in), chunks(c_in)))
    return jnp.moveaxis(ys, 0, 1).reshape(nb, length, nh, hp)


def _mixer_ab(hn, w_in, conv_w, conv_b, dt_bias, a_log, d_skip, gate_norm, w_out):
    f32 = jnp.float32
    nb, length, _ = hn.shape
    proj = hn @ w_in
    u_f, z, xbc, dt_raw = jnp.split(
        proj, [FNET_WIDTH, FNET_WIDTH + SSM_INNER, FNET_WIDTH + SSM_INNER + SSM_CONV_CH], axis=-1)
    uf = u_f.astype(f32).reshape(nb, length, FNET_HEADS, FNET_HEAD_DIM)
    y_four = jnp.fft.fftn(uf, axes=(1, 3), norm='ortho').real
    y_four = y_four.reshape(nb, length, FNET_WIDTH).astype(hn.dtype)
    xbc = lax.conv_general_dilated(
        xbc, conv_w[:, None, :].astype(xbc.dtype), window_strides=(1,),
        padding=[(SSM_CONV // 2, SSM_CONV // 2)], dimension_numbers=('NWC', 'WIO', 'NWC'),
        feature_group_count=SSM_CONV_CH)
    xbc = jax.nn.silu(xbc + conv_b).astype(f32)
    xs, bs, cs = jnp.split(xbc, [SSM_INNER, SSM_INNER + SSM_GROUPS * SSM_STATE], axis=-1)
    xs = xs.reshape(nb, length, SSM_HEADS, SSM_HEAD_DIM)
    bs = bs.reshape(nb, length, SSM_GROUPS, SSM_STATE)
    cs = cs.reshape(nb, length, SSM_GROUPS, SSM_STATE)
    dt = jax.nn.softplus(dt_raw.astype(f32).reshape(nb, length, 2, SSM_HEADS) + dt_bias.astype(f32))
    a = -jnp.exp(a_log.astype(f32))
    y_fw = _ssd_scan(xs, dt[:, :, 0], a[0], bs, cs)
    flip = lambda t: jnp.flip(t, axis=1)
    y_bw = flip(_ssd_scan(flip(xs), flip(dt[:, :, 1]), a[1], flip(bs), flip(cs)))
    y = y_fw + y_bw + d_skip.astype(f32)[:, None] * xs
    y = y.reshape(nb, length, SSM_INNER) * jax.nn.silu(z.astype(f32))
    y = _rms_norm(y.reshape(nb, length, SSM_GROUPS, SSM_INNER // SSM_GROUPS),
                  gate_norm.reshape(SSM_GROUPS, SSM_INNER // SSM_GROUPS))
    y = y.reshape(nb, length, SSM_INNER).astype(hn.dtype)
    return jnp.concatenate([y_four, y], axis=-1) @ w_out


def _mixer_cd(hn, w_in, pool_w, pool_scale, q_norm, k_norm, lq1, lk1, lq2, lk2, sub_norm, w_out, lambda_init):
    f32 = jnp.float32
    nb, length, _ = hn.shape
    proj = hn @ w_in
    u_p, q, k, v = jnp.split(
        proj, [POOL_WIDTH, POOL_WIDTH + DIFF_QK_WIDTH, POOL_WIDTH + 2 * DIFF_QK_WIDTH], axis=-1)
    upf = u_p.astype(f32).reshape(nb, length, POOL_GROUPS, POOL_GROUP_DIM)
    prefix = jnp.concatenate(
        [jnp.zeros((nb, 1, POOL_GROUPS, POOL_GROUP_DIM), f32), jnp.cumsum(upf, axis=1)], axis=1)
    pos = jnp.arange(length)
    outs = []
    for g, w in enumerate(POOL_WINDOWS):
        lo = jnp.clip(pos - w // 2, 0, length - 1)
        hi = jnp.clip(pos + w // 2 - 1, 0, length - 1)
        pg = prefix[:, :, g]
        total = jnp.take(pg, hi + 1, axis=1) - jnp.take(pg, lo, axis=1)
        mean = total / (hi - lo + 1).astype(f32)[:, None]
        outs.append(jnp.einsum('bsc,cd->bsd', (mean - upf[:, :, g]).astype(hn.dtype), pool_w[g]))
    y_pool = jnp.concatenate(outs, axis=-1) * pool_scale
    q = _rms_norm(q.reshape(nb, length, DIFF_HEADS, 2, DIFF_HEAD_DIM), q_norm)
    k = _rms_norm(k.reshape(nb, length, DIFF_HEADS, 2, DIFF_HEAD_DIM), k_norm)
    cos, sin = _rope_tables(length)
    q = _partial_rope(q, cos, sin) * (DIFF_HEAD_DIM ** -0.5)
    k = _partial_rope(k, cos, sin)
    v = v.reshape(nb, length, DIFF_HEADS, DIFF_V_DIM)
    lam = (jnp.exp(jnp.sum(lq1.astype(f32) * lk1.astype(f32)))
           - jnp.exp(jnp.sum(lq2.astype(f32) * lk2.astype(f32))) + lambda_init)
    q_blocks = jnp.moveaxis(
        q.reshape(nb, length // Q_BLOCK, Q_BLOCK, DIFF_HEADS, 2, DIFF_HEAD_DIM), 1, 0)

    def attend(q_blk):
        s = jnp.einsum('bqhcd,bkhcd->bhcqk', q_blk, k).astype(f32)
        p = jax.nn.softmax(s, axis=-1)
        wts = (p[:, :, 0] - lam * p[:, :, 1]).astype(v.dtype)
        return jnp.einsum('bhqk,bkhe->bqhe', wts, v)

    o = lax.map(attend, q_blocks)
    o = jnp.moveaxis(o, 0, 1).reshape(nb, length, DIFF_HEADS, DIFF_V_DIM)
    o = (_rms_norm(o, sub_norm) * (1.0 - lambda_init)).reshape(nb, length, DIFF_V_WIDTH)
    return jnp.concatenate([y_pool.astype(hn.dtype), o.astype(hn.dtype)], axis=-1) @ w_out


def _cross_attn(hn, mem_n, w_q, w_kv, q_norm, k_norm, w_o):
    nb, length, _ = hn.shape
    n_mem = mem_n.shape[1]
    q = _rms_norm((hn @ w_q).reshape(nb, length, CROSS_HEADS, CROSS_HEAD_DIM), q_norm)
    q = q * (CROSS_HEAD_DIM ** -0.5)
    k, v = jnp.split(mem_n @ w_kv, [D_MODEL], axis=-1)
    k = _rms_norm(k.reshape(nb, n_mem, CROSS_HEADS, CROSS_HEAD_DIM), k_norm)
    v = v.reshape(nb, n_mem, CROSS_HEADS, CROSS_HEAD_DIM)
    s = jnp.einsum('bqhd,bkhd->bhqk', q, k).astype(jnp.float32)
    p = jax.nn.softmax(s, axis=-1).astype(v.dtype)
    o = jnp.einsum('bhqk,bkhd->bqhd', p, v).reshape(nb, length, D_MODEL)
    return o @ w_o


def _trunk(x, mem, p):
    for l in range(DEPTH):
        x = x + 0.5 * _swiglu(_rms_norm(x, p['ffn1_norm'][l]), p['ffn1_w_gate'][l],
                              p['ffn1_w_up'][l], p['ffn1_w_down'][l])
        hn = _rms_norm(x, p['mix_norm'][l])
        i = l // 2
        if l % 2 == 0:
            x = x + _mixer_ab(hn, p['ab_w_in'][i], p['ab_conv_w'][i], p['ab_conv_b'][i],
                              p['ab_dt_bias'][i], p['ab_a_log'][i], p['ab_d_skip'][i],
                              p['ab_gate_norm'][i], p['ab_w_out'][i])
        else:
            x = x + _mixer_cd(hn, p['cd_w_in'][i], p['cd_pool_w'][i], p['cd_pool_scale'][i],
                              p['cd_q_norm'][i], p['cd_k_norm'][i], p['cd_lambda_q1'][i],
                              p['cd_lambda_k1'][i], p['cd_lambda_q2'][i], p['cd_lambda_k2'][i],
                              p['cd_sub_norm'][i], p['cd_w_out'][i], _lambda_init(l))
        x = x + _cross_attn(_rms_norm(x, p['cross_norm'][l]), _rms_norm(mem, p['cross_mem_norm'][l]),
                            p['cross_w_q'][l], p['cross_w_kv'][l], p['cross_q_norm'][l],
                            p['cross_k_norm'][l], p['cross_w_o'][l])
        x = x + 0.5 * _swiglu(_rms_norm(x, p['ffn2_norm'][l]), p['ffn2_w_gate'][l],
                              p['ffn2_w_up'][l], p['ffn2_w_down'][l])
    return x


def setup_inputs(seed: int = 0) -> dict:
    key = jax.random.key(seed)
    ks = iter(jax.random.split(key, 64))
    f32 = jnp.float32

    def nrm(shape, scale):
        return jax.random.normal(next(ks), shape, f32) * scale

    def gain(shape):
        return 1.0 + 0.02 * jax.random.normal(next(ks), shape, f32)

    dt0 = jnp.exp(jax.random.uniform(next(ks), (N_EVEN, 2, SSM_HEADS), f32)
                  * (math.log(DT_MAX) - math.log(DT_MIN)) + math.log(DT_MIN))
    ab_dt_bias = dt0 + jnp.log(-jnp.expm1(-dt0))
    ab_a_log = jnp.log(jax.random.uniform(next(ks), (N_EVEN, 2, SSM_HEADS), f32, 1.0, 16.0))
    return {
        'x_prompt': nrm((BATCH, SEQ, D_MODEL), 1.0),
        'x_sample': nrm((DEC_BATCH, DEC_SEQ, D_MODEL), 1.0),
        'mem_prompt': nrm((BATCH, N_MEM, D_MODEL), 1.0),
        'mem_sample': nrm((DEC_BATCH, N_MEM, D_MODEL), 1.0),
        'ffn1_norm': gain((DEPTH, D_MODEL)),
        'ffn1_w_gate': nrm((DEPTH, D_MODEL, D_FF), D_MODEL ** -0.5),
        'ffn1_w_up': nrm((DEPTH, D_MODEL, D_FF), D_MODEL ** -0.5),
        'ffn1_w_down': nrm((DEPTH, D_FF, D_MODEL), D_FF ** -0.5),
        'mix_norm': gain((DEPTH, D_MODEL)),
        'ab_w_in': nrm((N_EVEN, D_MODEL, AB_IN), D_MODEL ** -0.5),
        'ab_conv_w': nrm((N_EVEN, SSM_CONV, SSM_CONV_CH), SSM_CONV ** -0.5),
        'ab_conv_b': nrm((N_EVEN, SSM_CONV_CH), 0.02),
        'ab_dt_bias': ab_dt_bias,
        'ab_a_log': ab_a_log,
        'ab_d_skip': 1.0 + nrm((N_EVEN, SSM_HEADS), 0.1),
        'ab_gate_norm': gain((N_EVEN, SSM_INNER)),
        'ab_w_out': nrm((N_EVEN, AB_OUT, D_MODEL), AB_OUT ** -0.5),
        'cd_w_in': nrm((N_ODD, D_MODEL, CD_IN), D_MODEL ** -0.5),
        'cd_pool_w': nrm((N_ODD, POOL_GROUPS, POOL_GROUP_DIM, POOL_GROUP_DIM), POOL_GROUP_DIM ** -0.5),
        'cd_pool_scale': gain((N_ODD, POOL_WIDTH)),
        'cd_q_norm': gain((N_ODD, DIFF_HEAD_DIM)),
        'cd_k_norm': gain((N_ODD, DIFF_HEAD_DIM)),
        'cd_lambda_q1': nrm((N_ODD, DIFF_HEAD_DIM), 0.1),
        'cd_lambda_k1': nrm((N_ODD, DIFF_HEAD_DIM), 0.1),
        'cd_lambda_q2': nrm((N_ODD, DIFF_HEAD_DIM), 0.1),
        'cd_lambda_k2': nrm((N_ODD, DIFF_HEAD_DIM), 0.1),
        'cd_sub_norm': gain((N_ODD, DIFF_V_DIM)),
        'cd_w_out': nrm((N_ODD, CD_OUT, D_MODEL), CD_OUT ** -0.5),
        'cross_norm': gain((DEPTH, D_MODEL)),
        'cross_mem_norm': gain((DEPTH, D_MODEL)),
        'cross_w_q': nrm((DEPTH, D_MODEL, D_MODEL), D_MODEL ** -0.5),
        'cross_w_kv': nrm((DEPTH, D_MODEL, 2 * D_MODEL), D_MODEL ** -0.5),
        'cross_q_norm': gain((DEPTH, CROSS_HEAD_DIM)),
        'cross_k_norm': gain((DEPTH, CROSS_HEAD_DIM)),
        'cross_w_o': nrm((DEPTH, D_MODEL, D_MODEL), D_MODEL ** -0.5),
        'ffn2_norm': gain((DEPTH, D_MODEL)),
        'ffn2_w_gate': nrm((DEPTH, D_MODEL, D_FF), D_MODEL ** -0.5),
        'ffn2_w_up': nrm((DEPTH, D_MODEL, D_FF), D_MODEL ** -0.5),
        'ffn2_w_down': nrm((DEPTH, D_FF, D_MODEL), D_FF ** -0.5),
    }


def reference(x_prompt, x_sample, mem_prompt, mem_sample,
              ffn1_norm, ffn1_w_gate, ffn1_w_up, ffn1_w_down,
              mix_norm,
              ab_w_in, ab_conv_w, ab_conv_b, ab_dt_bias, ab_a_log, ab_d_skip, ab_gate_norm, ab_w_out,
              cd_w_in, cd_pool_w, cd_pool_scale, cd_q_norm, cd_k_norm, cd_lambda_q1, cd_lambda_k1,
              cd_lambda_q2, cd_lambda_k2, cd_sub_norm, cd_w_out,
              cross_norm, cross_mem_norm, cross_w_q, cross_w_kv, cross_q_norm, cross_k_norm, cross_w_o,
              ffn2_norm, ffn2_w_gate, ffn2_w_up, ffn2_w_down):
    p = {
        'ffn1_norm': ffn1_norm, 'ffn1_w_gate': ffn1_w_gate, 'ffn1_w_up': ffn1_w_up, 'ffn1_w_down': ffn1_w_down,
        'mix_norm': mix_norm,
        'ab_w_in': ab_w_in, 'ab_conv_w': ab_conv_w, 'ab_conv_b': ab_conv_b, 'ab_dt_bias': ab_dt_bias,
        'ab_a_log': ab_a_log, 'ab_d_skip': ab_d_skip, 'ab_gate_norm': ab_gate_norm, 'ab_w_out': ab_w_out,
        'cd_w_in': cd_w_in, 'cd_pool_w': cd_pool_w, 'cd_pool_scale': cd_pool_scale, 'cd_q_norm': cd_q_norm,
        'cd_k_norm': cd_k_norm, 'cd_lambda_q1': cd_lambda_q1, 'cd_lambda_k1': cd_lambda_k1,
        'cd_lambda_q2': cd_lambda_q2, 'cd_lambda_k2': cd_lambda_k2, 'cd_sub_norm': cd_sub_norm,
        'cd_w_out': cd_w_out,
        'cross_norm': cross_norm, 'cross_mem_norm': cross_mem_norm, 'cross_w_q': cross_w_q,
        'cross_w_kv': cross_w_kv, 'cross_q_norm': cross_q_norm, 'cross_k_norm': cross_k_norm,
        'cross_w_o': cross_w_o,
        'ffn2_norm': ffn2_norm, 'ffn2_w_gate': ffn2_w_gate, 'ffn2_w_up': ffn2_w_up, 'ffn2_w_down': ffn2_w_down,
    }
    y_prompt = _trunk(x_prompt, mem_prompt, p)
    y_sample = _trunk(x_sample, mem_sample, p)
    return (y_prompt, y_sample)
```

```python
import functools
import math

import numpy as np
import jax
import jax.numpy as jnp
from jax import lax
from jax.experimental import pallas as pl
from jax.experimental.pallas import tpu as pltpu

F32 = jnp.float32
BF16 = jnp.bfloat16
EPS = 1e-6

VMEM_LIMIT_BYTES = 56 * 1024 * 1024
BF16_SUBLANE_TILE = 16

D_MODEL = 1024
FNET_HEADS = 4
FNET_HEAD_DIM = 256
FFT_INNER = 64
SSM_HEADS = 32
SSM_HEAD_DIM = 64
SSM_STATE = 128
SSM_GROUPS = 4
SSM_HEADS_PER_GROUP = SSM_HEADS // SSM_GROUPS
SSM_INNER = SSM_HEADS * SSM_HEAD_DIM
SSM_GROUP_WIDTH = SSM_INNER // SSM_GROUPS
SSM_BC_WIDTH = SSM_GROUPS * SSM_STATE
SSM_CONV_CH = SSM_INNER + 2 * SSM_BC_WIDTH
SSM_CONV = 5
SSM_CHUNK = 128
POOL_WINDOWS = (2, 4, 8, 16)
POOL_GROUP_DIM = 256
DIFF_HEADS = 8
DIFF_HEAD_DIM = 64
DIFF_V_DIM = 128
ROT_DIM = 16
ROPE_THETA = 500000.0
CROSS_HEADS = 4
CROSS_HEAD_DIM = 256
HALO = BF16_SUBLANE_TILE


def _cparams(*semantics):
    return pltpu.CompilerParams(dimension_semantics=semantics, vmem_limit_bytes=VMEM_LIMIT_BYTES)


def _dot(a, b):
    return jnp.dot(a, b, preferred_element_type=F32)


def _dot_nt(a, b):
    return lax.dot_general(a, b, (((1,), (1,)), ((), ())), preferred_element_type=F32)


def _dot_tn(a, b):
    return lax.dot_general(a, b, (((0,), (0,)), ((), ())), preferred_element_type=F32)


def _rms(x, g):
    return x * lax.rsqrt(jnp.mean(x * x, axis=-1, keepdims=True) + EPS) * g


def _silu(x):
    return x * jax.nn.sigmoid(x)


def _row_tile(n, want):
    t = min(n, want)
    assert n % t == 0, (n, t)
    return t


def _ffn_body(x_ref, g_ref, wg_ref, wu_ref, wd_ref, o_ref, xn_ref, acc_ref):
    j = pl.program_id(1)

    @pl.when(j == 0)
    def _():
        xn_ref[...] = _rms(x_ref[...], g_ref[...]).astype(BF16)
        acc_ref[...] = jnp.zeros_like(acc_ref)

    xn = xn_ref[...]
    gate = _dot(xn, wg_ref[...])
    up = _dot(xn, wu_ref[...])
    h = (_silu(gate) * up).astype(BF16)
    acc_ref[...] += _dot(h, wd_ref[...])

    @pl.when(j == pl.num_programs(1) - 1)
    def _():
        o_ref[...] = x_ref[...] + 0.5 * acc_ref[...]


def _ffn(x, g, wg, wu, wd):
    t, d = x.shape
    f = wg.shape[1]
    tm = _row_tile(t, 512)
    tf = f // 2 if (f // 2) % 128 == 0 else f
    return pl.pallas_call(
        _ffn_body,
        out_shape=jax.ShapeDtypeStruct((t, d), F32),
        grid=(t // tm, f // tf),
        in_specs=[
            pl.BlockSpec((tm, d), lambda i, j: (i, 0)),
            pl.BlockSpec((1, d), lambda i, j: (0, 0)),
            pl.BlockSpec((d, tf), lambda i, j: (0, j)),
            pl.BlockSpec((d, tf), lambda i, j: (0, j)),
            pl.BlockSpec((tf, d), lambda i, j: (j, 0)),
        ],
        out_specs=pl.BlockSpec((tm, d), lambda i, j: (i, 0)),
        scratch_shapes=[pltpu.VMEM((tm, d), BF16), pltpu.VMEM((tm, d), F32)],
        compiler_params=_cparams("parallel", "arbitrary"),
        name="ffn",
    )(x, g.reshape(1, d), wg, wu, wd)


def _ab_in_body(x_ref, g_ref, w_ref, wdt_ref, uf_ref, z_ref, xbc_ref, dt_ref):
    xn = _rms(x_ref[...], g_ref[...]).astype(BF16)
    col = 0
    for ref in (uf_ref, z_ref, xbc_ref):
        width = ref.shape[1]
        for c in range(0, width, 1024):
            ref[:, c:c + 1024] = _dot(xn, w_ref[:, col + c:col + c + 1024]).astype(ref.dtype)
        col += width
    dt_ref[...] = _dot(xn, wdt_ref[...])


def _ab_in(x, g, w_main, w_dt):
    t, d = x.shape
    tm = _row_tile(t, 512)
    n_main = w_main.shape[1]
    row = lambda i: (i, 0)
    fixed = lambda i: (0, 0)
    return pl.pallas_call(
        _ab_in_body,
        out_shape=(
            jax.ShapeDtypeStruct((t, D_MODEL), BF16),
            jax.ShapeDtypeStruct((t, SSM_INNER), BF16),
            jax.ShapeDtypeStruct((t, SSM_CONV_CH), BF16),
            jax.ShapeDtypeStruct((t, 128), F32),
        ),
        grid=(t // tm,),
        in_specs=[
            pl.BlockSpec((tm, d), row),
            pl.BlockSpec((1, d), fixed),
            pl.BlockSpec((d, n_main), fixed),
            pl.BlockSpec((d, 128), fixed),
        ],
        out_specs=(
            pl.BlockSpec((tm, D_MODEL), row),
            pl.BlockSpec((tm, SSM_INNER), row),
            pl.BlockSpec((tm, SSM_CONV_CH), row),
            pl.BlockSpec((tm, 128), row),
        ),
        compiler_params=_cparams("parallel"),
        name="ab_in",
    )(x, g.reshape(1, d), w_main, w_dt)


def _dft_tables(length):
    l2 = FFT_INNER
    l1 = length // l2
    assert l1 * l2 == length
    k1 = np.arange(l1)
    ang1 = 2.0 * np.pi * ((k1[:, None] * k1[None, :]) % l1) / l1
    f1 = np.concatenate([np.cos(ang1), -np.sin(ang1)], axis=0)
    k2 = np.arange(l2)
    n2 = np.arange(l2)
    kk = k1[:, None, None] + l1 * k2[None, :, None]
    ang2 = 2.0 * np.pi * ((kk * n2[None, None, :]) % length) / length
    mr, mi = np.cos(ang2), -np.sin(ang2)
    m2 = np.concatenate([np.concatenate([mr, -mi], axis=2),
                         np.concatenate([mi, mr], axis=2)], axis=1)
    c = np.arange(FNET_HEAD_DIM)
    angc = 2.0 * np.pi * ((c[:, None] * c[None, :]) % FNET_HEAD_DIM) / FNET_HEAD_DIM
    fc = np.concatenate([np.cos(angc), np.sin(angc)], axis=0)
    return (jnp.asarray(f1, dtype=BF16), jnp.asarray(m2, dtype=BF16), jnp.asarray(fc, dtype=BF16))


def _fft1_body(f_ref, x_ref, o_ref):
    o_ref[0] = _dot(f_ref[...], x_ref[0]).astype(o_ref.dtype)


def _fft2_body(scale, m_ref, fc_ref, t_ref, o_ref):
    l2 = t_ref.shape[2]
    t = jnp.concatenate([t_ref[0, 0], t_ref[0, 1]], axis=0)
    y = _dot(m_ref[0], t)
    yr, yi = y[:l2].astype(BF16), y[l2:].astype(BF16)
    fc = fc_ref[...]
    outs = []
    for h in range(FNET_HEADS):
        sl = slice(h * FNET_HEAD_DIM, (h + 1) * FNET_HEAD_DIM)
        outs.append(_dot(jnp.concatenate([yr[:, sl], yi[:, sl]], axis=1), fc))
    o_ref[0] = (jnp.concatenate(outs, axis=1) * scale).astype(o_ref.dtype)


def _fourier(uf, nb, length):
    c = D_MODEL
    l2 = FFT_INNER
    l1 = length // l2
    f1, m2, fc = _dft_tables(length)
    x1 = uf.reshape(nb, l1, l2 * c)
    tcol = min(l2 * c, 8192)
    t = pl.pallas_call(
        _fft1_body,
        out_shape=jax.ShapeDtypeStruct((nb, 2 * l1, l2 * c), BF16),
        grid=(nb, (l2 * c) // tcol),
        in_specs=[pl.BlockSpec((2 * l1, l1), lambda b, j: (0, 0)),
                  pl.BlockSpec((1, l1, tcol), lambda b, j: (b, 0, j))],
        out_specs=pl.BlockSpec((1, 2 * l1, tcol), lambda b, j: (b, 0, j)),
        compiler_params=_cparams("parallel", "parallel"),
        name="fft_stage1",
    )(f1, x1)
    t5 = t.reshape(nb, 2, l1, l2, c)
    scale = 1.0 / math.sqrt(length * FNET_HEAD_DIM)
    y = pl.pallas_call(
        functools.partial(_fft2_body, scale),
        out_shape=jax.ShapeDtypeStruct((nb, l2, l1 * c), BF16),
        grid=(nb, l1),
        in_specs=[pl.BlockSpec((1, 2 * l2, 2 * l2), lambda b, k: (k, 0, 0)),
                  pl.BlockSpec((2 * FNET_HEAD_DIM, FNET_HEAD_DIM), lambda b, k: (0, 0)),
                  pl.BlockSpec((1, 2, None, l2, c), lambda b, k: (b, 0, k, 0, 0))],
        out_specs=pl.BlockSpec((1, l2, c), lambda b, k: (b, 0, k)),
        compiler_params=_cparams("parallel", "parallel"),
        name="fft_stage2",
    )(m2, fc, t5)
    return y.reshape(nb * length, c)


def _halo_specs(tr, width, length):
    per = tr // HALO
    last = length // HALO - 1
    return [
        pl.BlockSpec((1, HALO, width), lambda b, i: (b, jnp.maximum(i * per - 1, 0), 0)),
        pl.BlockSpec((1, tr, width), lambda b, i: (b, i, 0)),
        pl.BlockSpec((1, HALO, width), lambda b, i: (b, jnp.minimum((i + 1) * per, last), 0)),
    ]


def _with_halo(prev_ref, main_ref, next_ref, cols):
    i = pl.program_id(1)
    keep_prev = (i > 0).astype(F32)
    keep_next = (i < pl.num_programs(1) - 1).astype(F32)
    return jnp.concatenate([
        prev_ref[0, :, cols].astype(F32) * keep_prev,
        main_ref[0, :, cols].astype(F32),
        next_ref[0, :, cols].astype(F32) * keep_next,
    ], axis=0)


def _shift_rows(x, k):
    n = x.shape[0]
    return x if k % n == 0 else pltpu.roll(x, (-k) % n, 0)


def _conv_body(prev_ref, main_ref, next_ref, w_ref, b_ref, o_ref):
    tr = main_ref.shape[1]
    half = SSM_CONV // 2
    for c in range(0, SSM_CONV_CH, 512):
        cols = slice(c, c + 512)
        ext = _with_halo(prev_ref, main_ref, next_ref, cols)
        acc = jnp.zeros((tr, 512), F32)
        for j in range(SSM_CONV):
            acc = acc + _shift_rows(ext, j - half)[HALO:HALO + tr] * w_ref[j:j + 1, cols]
        o_ref[0, :, cols] = _silu(acc + b_ref[:, cols]).astype(o_ref.dtype)


def _conv_silu(xbc, conv_w, conv_b):
    nb, length, ch = xbc.shape
    tr = _row_tile(length, 256)
    fixed = lambda b, i: (0, 0)
    return pl.pallas_call(
        _conv_body,
        out_shape=jax.ShapeDtypeStruct((nb, length, ch), BF16),
        grid=(nb, length // tr),
        in_specs=_halo_specs(tr, ch, length) + [pl.BlockSpec((SSM_CONV, ch), fixed),
                                                pl.BlockSpec((1, ch), fixed)],
        out_specs=pl.BlockSpec((1, tr, ch), lambda b, i: (b, i, 0)),
        compiler_params=_cparams("parallel", "parallel"),
        name="ssd_conv",
    )(xbc, xbc, xbc, conv_w, conv_b.reshape(1, ch))


def _ssd_chunk(reverse, x_ref, b_ref, c_ref, dt_ref, dtb_ref, alog_ref, state_ref):
    q = SSM_CHUNK
    hd = SSM_HEAD_DIM
    d = 1 if reverse else 0
    x = x_ref[0].astype(F32)
    dt_raw = dt_ref[0][:, SSM_HEADS * d:SSM_HEADS * (d + 1)] + dtb_ref[d:d + 1, :]
    dt = jnp.maximum(dt_raw, 0.0) + jnp.log1p(jnp.exp(-jnp.abs(dt_raw)))
    da = dt * (-jnp.exp(alog_ref[d:d + 1, :]))
    row = lax.broadcasted_iota(jnp.int32, (q, q), 0)
    col = lax.broadcasted_iota(jnp.int32, (q, q), 1)
    mask = (col >= row) if reverse else (col <= row)
    tri = mask.astype(F32)
    acum = jnp.dot(tri, da, preferred_element_type=F32, precision=lax.Precision.HIGHEST)
    acum_t = lax.dot_general(da, tri, (((0,), (1,)), ((), ())), preferred_element_type=F32,
                             precision=lax.Precision.HIGHEST)
    total = acum[0:1] if reverse else acum[q - 1:q]
    w_end = jnp.exp(total - acum)
    e_acum = jnp.exp(acum)
    e_total = jnp.exp(total)

    def per_head(v, g, rows):
        return jnp.concatenate(
            [jnp.broadcast_to(v[:, g * SSM_HEADS_PER_GROUP + h:g * SSM_HEADS_PER_GROUP + h + 1], (rows, hd))
             for h in range(SSM_HEADS_PER_GROUP)], axis=1)

    ys = []
    for g in range(SSM_GROUPS):
        bg = b_ref[0, :, g * SSM_STATE:(g + 1) * SSM_STATE]
        cg = c_ref[0, :, g * SSM_STATE:(g + 1) * SSM_STATE]
        scores = _dot_nt(cg, bg)
        xdt = x[:, g * SSM_GROUP_WIDTH:(g + 1) * SSM_GROUP_WIDTH] * per_head(dt, g, q)
        xdt_b = xdt.astype(BF16)
        intra = []
        for h in range(SSM_HEADS_PER_GROUP):
            hh = g * SSM_HEADS_PER_GROUP + h
            seg = acum[:, hh:hh + 1] - acum_t[hh:hh + 1, :]
            decay = jnp.exp(jnp.where(mask, seg, -jnp.inf))
            intra.append(_dot((scores * decay).astype(BF16), xdt_b[:, h * hd:(h + 1) * hd]))
        state = state_ref[g]
        y = jnp.concatenate(intra, axis=1) + _dot(cg, state.astype(BF16)) * per_head(e_acum, g, q)
        ys.append(y)
        upd = _dot_tn(bg, (xdt * per_head(w_end, g, q)).astype(BF16))
        state_ref[g] = state * per_head(e_total, g, SSM_STATE) + upd
    return jnp.concatenate(ys, axis=1), x


def _ssd_fwd_body(x_ref, b_ref, c_ref, dt_ref, dtb_ref, alog_ref, y_ref, state_ref):
    @pl.when(pl.program_id(1) == 0)
    def _():
        state_ref[...] = jnp.zeros_like(state_ref)

    y, _ = _ssd_chunk(False, x_ref, b_ref, c_ref, dt_ref, dtb_ref, alog_ref, state_ref)
    y_ref[0] = y


def _ssd_bwd_body(x_ref, b_ref, c_ref, dt_ref, dtb_ref, alog_ref, dskip_ref, yf_ref, y_ref, state_ref):
    @pl.when(pl.program_id(1) == 0)
    def _():
        state_ref[...] = jnp.zeros_like(state_ref)

    y, x = _ssd_chunk(True, x_ref, b_ref, c_ref, dt_ref, dtb_ref, alog_ref, state_ref)
    y_ref[0] = yf_ref[0] + y + dskip_ref[...] * x


def _ssd(xbc, dt, dt_bias, a_log, d_skip):
    nb, length, _ = xbc.shape
    q = SSM_CHUNK
    nc = length // q
    fixed = lambda b, c: (0, 0)
    scratch = [pltpu.VMEM((SSM_GROUPS, SSM_STATE, SSM_GROUP_WIDTH), F32)]

    def specs(chunk):
        return [
            pl.BlockSpec((1, q, SSM_INNER), lambda b, c: (b, chunk(c), 0)),
            pl.BlockSpec((1, q, SSM_BC_WIDTH), lambda b, c: (b, chunk(c), SSM_INNER // SSM_BC_WIDTH)),
            pl.BlockSpec((1, q, SSM_BC_WIDTH), lambda b, c: (b, chunk(c), SSM_INNER // SSM_BC_WIDTH + 1)),
            pl.BlockSpec((1, q, 128), lambda b, c: (b, chunk(c), 0)),
            pl.BlockSpec((2, SSM_HEADS), fixed),
            pl.BlockSpec((2, SSM_HEADS), fixed),
        ]

    fw = lambda c: c
    y_fw = pl.pallas_call(
        _ssd_fwd_body,
        out_shape=jax.ShapeDtypeStruct((nb, length, SSM_INNER), F32),
        grid=(nb, nc),
        in_specs=specs(fw),
        out_specs=pl.BlockSpec((1, q, SSM_INNER), lambda b, c: (b, c, 0)),
        scratch_shapes=scratch,
        compiler_params=_cparams("parallel", "arbitrary"),
        name="ssd_forward",
    )(xbc, xbc, xbc, dt, dt_bias, a_log)
    bw = lambda c: nc - 1 - c
    dskip = jnp.repeat(d_skip, SSM_HEAD_DIM).reshape(1, SSM_INNER)
    return pl.pallas_call(
        _ssd_bwd_body,
        out_shape=jax.ShapeDtypeStruct((nb, length, SSM_INNER), F32),
        grid=(nb, nc),
        in_specs=specs(bw) + [pl.BlockSpec((1, SSM_INNER), fixed),
                              pl.BlockSpec((1, q, SSM_INNER), lambda b, c: (b, bw(c), 0))],
        out_specs=pl.BlockSpec((1, q, SSM_INNER), lambda b, c: (b, bw(c), 0)),
        scratch_shapes=scratch,
        compiler_params=_cparams("parallel", "arbitrary"),
        name="ssd_backward",
    )(xbc, xbc, xbc, dt, dt_bias, a_log, dskip, y_fw)


def _ab_out_body(x_ref, yf_ref, ys_ref, z_ref, gn_ref, w_ref, o_ref):
    gw = SSM_GROUP_WIDTH
    acc = x_ref[...] + _dot(yf_ref[...], w_ref[0:D_MODEL, :])
    for g in range(SSM_GROUPS):
        cols = slice(g * gw, (g + 1) * gw)
        y = ys_ref[:, cols] * _silu(z_ref[:, cols].astype(F32))
        yn = _rms(y, gn_ref[:, cols]).astype(BF16)
        acc = acc + _dot(yn, w_ref[D_MODEL + g * gw:D_MODEL + (g + 1) * gw, :])
    o_ref[...] = acc


def _ab_out(x, y_four, y_ssd, z, gate_norm, w_out):
    t, d = x.shape
    tm = _row_tile(t, 512)
    row = lambda i: (i, 0)
    fixed = lambda i: (0, 0)
    return pl.pallas_call(
        _ab_out_body,
        out_shape=jax.ShapeDtypeStruct((t, d), F32),
        grid=(t // tm,),
        in_specs=[
            pl.BlockSpec((tm, d), row),
            pl.BlockSpec((tm, D_MODEL), row),
            pl.BlockSpec((tm, SSM_INNER), row),
            pl.BlockSpec((tm, SSM_INNER), row),
            pl.BlockSpec((1, SSM_INNER), fixed),
            pl.BlockSpec(w_out.shape, fixed),
        ],
        out_specs=pl.BlockSpec((tm, d), row),
        compiler_params=_cparams("parallel"),
        name="ab_out",
    )(x, y_four, y_ssd, z, gate_norm.reshape(1, SSM_INNER), w_out)


def _rope_tables(length):
    inv = ROPE_THETA ** (-jnp.arange(0, ROT_DIM, 2, dtype=F32) / ROT_DIM)
    ang = jnp.arange(length, dtype=F32)[:, None] * inv[None, :]
    cos, sin = jnp.cos(ang), jnp.sin(ang)
    half = ROT_DIM // 2
    pad = DIFF_HEAD_DIM - ROT_DIM
    ones = jnp.ones((length, pad), F32)
    zeros = jnp.zeros((length, pad), F32)
    zh = jnp.zeros((length, half), F32)
    c_self = jnp.concatenate([cos, cos, ones], axis=1)
    c_up = jnp.concatenate([-sin, zh, zeros], axis=1)
    c_down = jnp.concatenate([zh, sin, zeros], axis=1)
    rep = 128 // DIFF_HEAD_DIM
    return tuple(jnp.tile(tb, (1, rep)) for tb in (c_self, c_up, c_down))


def _cd_in_body(x_ref, g_ref, w_ref, ones_ref, qg_ref, kg_ref, cs_ref, cu_ref, cd_ref,
                up_ref, q_ref, k_ref, v_ref):
    xn = _rms(x_ref[0], g_ref[...]).astype(BF16)
    d = D_MODEL
    half = ROT_DIM // 2
    rep = d // 128
    c_self = jnp.tile(cs_ref[...], (1, rep))
    c_up = jnp.tile(cu_ref[...], (1, rep))
    c_down = jnp.tile(cd_ref[...], (1, rep))

    def qk_norm_rope(t, gain):
        sq = (t * t).astype(BF16)
        ms = jnp.concatenate([_dot(sq[:, c:c + 256], ones_ref[...]) for c in range(0, d, 256)], axis=1)
        t = t * lax.rsqrt(ms * (1.0 / DIFF_HEAD_DIM) + EPS) * gain
        return t * c_self + pltpu.roll(t, d - half, 1) * c_up + pltpu.roll(t, half, 1) * c_down

    up_ref[0] = _dot(xn, w_ref[:, 0:d]).astype(BF16)
    q = qk_norm_rope(_dot(xn, w_ref[:, d:2 * d]), qg_ref[...])
    q_ref[0] = (q * (DIFF_HEAD_DIM ** -0.5)).astype(BF16)
    k_ref[0] = qk_norm_rope(_dot(xn, w_ref[:, 2 * d:3 * d]), kg_ref[...]).astype(BF16)
    v_ref[0] = _dot(xn, w_ref[:, 3 * d:4 * d]).astype(BF16)


def _cd_in(x, g, w_in, q_norm, k_norm):
    nb, length, d = x.shape
    tm = _row_tile(length, 512)
    fixed = lambda b, i: (0, 0)
    tile = lambda b, i: (b, i, 0)
    pos = lambda b, i: (i, 0)
    ones_blk = jnp.asarray(np.kron(np.eye(256 // DIFF_HEAD_DIM), np.ones((DIFF_HEAD_DIM, DIFF_HEAD_DIM))), BF16)
    qg = jnp.tile(q_norm, d // DIFF_HEAD_DIM).reshape(1, d)
    kg = jnp.tile(k_norm, d // DIFF_HEAD_DIM).reshape(1, d)
    out = jax.ShapeDtypeStruct((nb, length, d), BF16)
    return pl.pallas_call(
        _cd_in_body,
        out_shape=(out, out, out, out),
        grid=(nb, length // tm),
        in_specs=[
            pl.BlockSpec((1, tm, d), tile),
            pl.BlockSpec((1, d), fixed),
            pl.BlockSpec(w_in.shape, fixed),
            pl.BlockSpec((256, 256), fixed),
            pl.BlockSpec((1, d), fixed),
            pl.BlockSpec((1, d), fixed),
            pl.BlockSpec((tm, 128), pos),
            pl.BlockSpec((tm, 128), pos),
            pl.BlockSpec((tm, 128), pos),
        ],
        out_specs=tuple(pl.BlockSpec((1, tm, d), tile) for _ in range(4)),
        compiler_params=_cparams("parallel", "parallel"),
        name="cd_in",
    )(x, g.reshape(1, d), w_in, ones_blk, qg, kg, *_rope_tables(length))


def _pool_body(length, prev_ref, main_ref, next_ref, w_ref, s_ref, o_ref):
    tr = main_ref.shape[1]
    gd = POOL_GROUP_DIM
    pos = pl.program_id(1) * tr + lax.broadcasted_iota(jnp.int32, (tr, 1), 0)
    ext = _with_halo(prev_ref, main_ref, next_ref, slice(None))
    win = ext + _shift_rows(ext, -1)
    outs = []
    for g, w in enumerate(POOL_WINDOWS):
        if g > 0:
            win = win[:, gd:]
            win = _shift_rows(win, -(w // 4)) + _shift_rows(win, w // 4)
        lo = jnp.maximum(pos - w // 2, 0)
        hi = jnp.minimum(pos + w // 2 - 1, length - 1)
        mean = win[HALO:HALO + tr, :gd] / (hi - lo + 1).astype(F32)
        centred = (mean - ext[HALO:HALO + tr, g * gd:(g + 1) * gd]).astype(BF16)
        outs.append(_dot(centred, w_ref[g]))
    o_ref[0] = (jnp.concatenate(outs, axis=1) * s_ref[...]).astype(o_ref.dtype)


def _pool(up, pool_w, pool_scale):
    nb, length, d = up.shape
    tr = _row_tile(length, 256)
    return pl.pallas_call(
        functools.partial(_pool_body, length),
        out_shape=jax.ShapeDtypeStruct((nb, length, d), BF16),
        grid=(nb, length // tr),
        in_specs=_halo_specs(tr, d, length) + [
            pl.BlockSpec(pool_w.shape, lambda b, i: (0, 0, 0)),
            pl.BlockSpec((1, d), lambda b, i: (0, 0)),
        ],
        out_specs=pl.BlockSpec((1, tr, d), lambda b, i: (b, i, 0)),
        compiler_params=_cparams("parallel", "parallel"),
        name="pool",
    )(up, up, up, pool_w, pool_scale.reshape(1, d))


def _diff_attn_body(lambda_init, q_ref, k_ref, v_ref, lam_ref, sub_ref, o_ref, qs_ref, m_ref, l_ref, acc_ref):
    kv = pl.program_id(3)
    tq = q_ref.shape[1]

    @pl.when(kv == 0)
    def _():
        q = q_ref[0]
        lane = lax.broadcasted_iota(jnp.int32, q.shape, 1)
        zero = jnp.zeros_like(q)
        qs_ref[0:tq] = jnp.where(lane < DIFF_HEAD_DIM, q, zero)
        qs_ref[tq:2 * tq] = jnp.where(lane >= DIFF_HEAD_DIM, q, zero)
        m_ref[...] = jnp.full_like(m_ref, -jnp.inf)
        l_ref[...] = jnp.zeros_like(l_ref)
        acc_ref[...] = jnp.zeros_like(acc_ref)

    s = _dot_nt(qs_ref[...], k_ref[0])
    m_prev = m_ref[...]
    m_new = jnp.maximum(m_prev, jnp.max(s, axis=-1, keepdims=True))
    alpha = jnp.exp(m_prev - m_new)
    p = jnp.exp(s - m_new)
    l_ref[...] = alpha * l_ref[...] + jnp.sum(p, axis=-1, keepdims=True)
    acc_ref[...] = alpha * acc_ref[...] + _dot(p.astype(BF16), v_ref[0])
    m_ref[...] = m_new

    @pl.when(kv == pl.num_programs(3) - 1)
    def _():
        o = acc_ref[...] / l_ref[...]
        lv = lam_ref[...]
        lam = (jnp.exp(jnp.sum(lv[0:1] * lv[1:2], axis=-1, keepdims=True))
               - jnp.exp(jnp.sum(lv[2:3] * lv[3:4], axis=-1, keepdims=True)) + lambda_init)
        diff = o[0:tq] - lam * o[tq:2 * tq]
        o_ref[0] = (_rms(diff, sub_ref[...]) * (1.0 - lambda_init)).astype(o_ref.dtype)


def _diff_attn(q, k, v, lam_vecs, sub_norm, lambda_init):
    nb, length, d = q.shape
    tq = _row_tile(length, 256)
    tk = _row_tile(length, 512)
    hw = 2 * DIFF_HEAD_DIM
    return pl.pallas_call(
        functools.partial(_diff_attn_body, lambda_init),
        out_shape=jax.ShapeDtypeStruct((nb, length, d), BF16),
        grid=(nb, DIFF_HEADS, length // tq, length // tk),
        in_specs=[
            pl.BlockSpec((1, tq, hw), lambda b, h, i, j: (b, i, h)),
            pl.BlockSpec((1, tk, hw), lambda b, h, i, j: (b, j, h)),
            pl.BlockSpec((1, tk, DIFF_V_DIM), lambda b, h, i, j: (b, j, h)),
            pl.BlockSpec((4, DIFF_HEAD_DIM), lambda b, h, i, j: (0, 0)),
            pl.BlockSpec((1, DIFF_V_DIM), lambda b, h, i, j: (0, 0)),
        ],
        out_specs=pl.BlockSpec((1, tq, DIFF_V_DIM), lambda b, h, i, j: (b, i, h)),
        scratch_shapes=[
            pltpu.VMEM((2 * tq, hw), BF16),
            pltpu.VMEM((2 * tq, 1), F32),
            pltpu.VMEM((2 * tq, 1), F32),
            pltpu.VMEM((2 * tq, DIFF_V_DIM), F32),
        ],
        compiler_params=_cparams("parallel", "parallel", "parallel", "arbitrary"),
        name="diff_attn",
    )(q, k, v, lam_vecs, sub_norm.reshape(1, DIFF_V_DIM))


def _cd_out_body(x_ref, yp_ref, o_ref_in, w_ref, o_ref):
    d = D_MODEL
    o_ref[...] = x_ref[...] + _dot(yp_ref[...], w_ref[0:d, :]) + _dot(o_ref_in[...], w_ref[d:2 * d, :])


def _cd_out(x, y_pool, o, w_out):
    t, d = x.shape
    tm = _row_tile(t, 512)
    row = lambda i: (i, 0)
    return pl.pallas_call(
        _cd_out_body,
        out_shape=jax.ShapeDtypeStruct((t, d), F32),
        grid=(t // tm,),
        in_specs=[pl.BlockSpec((tm, d), row), pl.BlockSpec((tm, d), row), pl.BlockSpec((tm, d), row),
                  pl.BlockSpec(w_out.shape, lambda i: (0, 0))],
        out_specs=pl.BlockSpec((tm, d), row),
        compiler_params=_cparams("parallel"),
        name="cd_out",
    )(x, y_pool, o, w_out)


def _mem_kv_body(m_ref, g_ref, w_ref, kg_ref, k_ref, v_ref):
    d = D_MODEL
    mn = _rms(m_ref[0], g_ref[...]).astype(BF16)
    k = _dot(mn, w_ref[:, 0:d])
    hd = CROSS_HEAD_DIM
    k_ref[0] = jnp.concatenate(
        [_rms(k[:, h * hd:(h + 1) * hd], kg_ref[...]) for h in range(CROSS_HEADS)], axis=1).astype(BF16)
    v_ref[0] = _dot(mn, w_ref[:, d:2 * d]).astype(BF16)


def _mem_kv(mem, g, w_kv, k_norm):
    nb, n_mem, d = mem.shape
    fixed = lambda b: (0, 0)
    out = jax.ShapeDtypeStruct((nb, n_mem, d), BF16)
    blk = pl.BlockSpec((1, n_mem, d), lambda b: (b, 0, 0))
    return pl.pallas_call(
        _mem_kv_body,
        out_shape=(out, out),
        grid=(nb,),
        in_specs=[blk, pl.BlockSpec((1, d), fixed), pl.BlockSpec(w_kv.shape, fixed),
                  pl.BlockSpec((1, CROSS_HEAD_DIM), fixed)],
        out_specs=(blk, blk),
        compiler_params=_cparams("parallel"),
        name="cross_mem_kv",
    )(mem, g.reshape(1, d), w_kv, k_norm.reshape(1, CROSS_HEAD_DIM))


def _cross_body(x_ref, g_ref, wq_ref, qg_ref, k_ref, v_ref, wo_ref, o_ref):
    hd = CROSS_HEAD_DIM
    x = x_ref[0]
    q = _dot(_rms(x, g_ref[...]).astype(BF16), wq_ref[...])
    heads = []
    for h in range(CROSS_HEADS):
        cols = slice(h * hd, (h + 1) * hd)
        qh = (_rms(q[:, cols], qg_ref[...]) * (hd ** -0.5)).astype(BF16)
        s = _dot_nt(qh, k_ref[0, :, cols])
        p = jnp.exp(s - jnp.max(s, axis=-1, keepdims=True))
        p = p / jnp.sum(p, axis=-1, keepdims=True)
        heads.append(_dot(p.astype(BF16), v_ref[0, :, cols]).astype(BF16))
    o_ref[0] = x + _dot(jnp.concatenate(heads, axis=1), wo_ref[...])


def _cross(x, g, w_q, q_norm, k, v, w_o):
    nb, length, d = x.shape
    n_mem = k.shape[1]
    tm = _row_tile(length, 512)
    fixed = lambda b, i: (0, 0)
    tile = lambda b, i: (b, i, 0)
    per_batch = lambda b, i: (b, 0, 0)
    return pl.pallas_call(
        _cross_body,
        out_shape=jax.ShapeDtypeStruct((nb, length, d), F32),
        grid=(nb, length // tm),
        in_specs=[
            pl.BlockSpec((1, tm, d), tile),
            pl.BlockSpec((1, d), fixed),
            pl.BlockSpec((d, d), fixed),
            pl.BlockSpec((1, CROSS_HEAD_DIM), fixed),
            pl.BlockSpec((1, n_mem, d), per_batch),
            pl.BlockSpec((1, n_mem, d), per_batch),
            pl.BlockSpec((d, d), fixed),
        ],
        out_specs=pl.BlockSpec((1, tm, d), tile),
        compiler_params=_cparams("parallel", "parallel"),
        name="cross_attn",
    )(x, g.reshape(1, d), w_q, q_norm.reshape(1, CROSS_HEAD_DIM), k, v, w_o)


def _lambda_init(layer_idx):
    return 0.8 - 0.6 * math.exp(-0.3 * layer_idx)


def _mixer_ab(x, p, i):
    nb, length, d = x.shape
    t = nb * length
    w_in = p['ab_w_in'][i]
    n_main = D_MODEL + SSM_INNER + SSM_CONV_CH
    w_dt = jnp.pad(w_in[:, n_main:], ((0, 0), (0, 128 - 2 * SSM_HEADS)))
    uf, z, xbc, dt = _ab_in(x.reshape(t, d), p['mix_norm_l'], w_in[:, :n_main].astype(BF16), w_dt.astype(BF16))
    y_four = _fourier(uf, nb, length)
    xbc = _conv_silu(xbc.reshape(nb, length, SSM_CONV_CH), p['ab_conv_w'][i], p['ab_conv_b'][i])
    y_ssd = _ssd(xbc, dt.reshape(nb, length, 128), p['ab_dt_bias'][i], p['ab_a_log'][i], p['ab_d_skip'][i])
    out = _ab_out(x.reshape(t, d), y_four, y_ssd.reshape(t, SSM_INNER), z, p['ab_gate_norm'][i],
                  p['ab_w_out'][i].astype(BF16))
    return out.reshape(nb, length, d)


def _mixer_cd(x, p, i, layer_idx):
    nb, length, d = x.shape
    t = nb * length
    up, q, k, v = _cd_in(x, p['mix_norm_l'], p['cd_w_in'][i].astype(BF16), p['cd_q_norm'][i], p['cd_k_norm'][i])
    y_pool = _pool(up, p['cd_pool_w'][i].astype(BF16), p['cd_pool_scale'][i])
    lam_vecs = jnp.stack([p['cd_lambda_q1'][i], p['cd_lambda_k1'][i], p['cd_lambda_q2'][i], p['cd_lambda_k2'][i]])
    o = _diff_attn(q, k, v, lam_vecs, p['cd_sub_norm'][i], _lambda_init(layer_idx))
    out = _cd_out(x.reshape(t, d), y_pool.reshape(t, d), o.reshape(t, d), p['cd_w_out'][i].astype(BF16))
    return out.reshape(nb, length, d)


def _trunk(x, mem, p, depth):
    nb, length, d = x.shape
    t = nb * length
    for l in range(depth):
        x = _ffn(x.reshape(t, d), p['ffn1_norm'][l], p['ffn1_w_gate'][l].astype(BF16),
                 p['ffn1_w_up'][l].astype(BF16), p['ffn1_w_down'][l].astype(BF16)).reshape(nb, length, d)
        pl_ = dict(p, mix_norm_l=p['mix_norm'][l])
        if l % 2 == 0:
            x = _mixer_ab(x, pl_, l // 2)
        else:
            x = _mixer_cd(x, pl_, l // 2, l)
        mk, mv = _mem_kv(mem, p['cross_mem_norm'][l], p['cross_w_kv'][l].astype(BF16), p['cross_k_norm'][l])
        x = _cross(x, p['cross_norm'][l], p['cross_w_q'][l].astype(BF16), p['cross_q_norm'][l], mk, mv,
                   p['cross_w_o'][l].astype(BF16))
        x = _ffn(x.reshape(t, d), p['ffn2_norm'][l], p['ffn2_w_gate'][l].astype(BF16),
                 p['ffn2_w_up'][l].astype(BF16), p['ffn2_w_down'][l].astype(BF16)).reshape(nb, length, d)
    return x


def kernel(x_prompt, x_sample, mem_prompt, mem_sample, ffn1_norm, ffn1_w_gate, ffn1_w_up, ffn1_w_down, mix_norm, ab_w_in, ab_conv_w, ab_conv_b, ab_dt_bias, ab_a_log, ab_d_skip, ab_gate_norm, ab_w_out, cd_w_in, cd_pool_w, cd_pool_scale, cd_q_norm, cd_k_norm, cd_lambda_q1, cd_lambda_k1, cd_lambda_q2, cd_lambda_k2, cd_sub_norm, cd_w_out, cross_norm, cross_mem_norm, cross_w_q, cross_w_kv, cross_q_norm, cross_k_norm, cross_w_o, ffn2_norm, ffn2_w_gate, ffn2_w_up, ffn2_w_down):
    p = {
        'ffn1_norm': ffn1_norm, 'ffn1_w_gate': ffn1_w_gate, 'ffn1_w_up': ffn1_w_up, 'ffn1_w_down': ffn1_w_down,
        'mix_norm': mix_norm,
        'ab_w_in': ab_w_in, 'ab_conv_w': ab_conv_w, 'ab_conv_b': ab_conv_b, 'ab_dt_bias': ab_dt_bias,
        'ab_a_log': ab_a_log, 'ab_d_skip': ab_d_skip, 'ab_gate_norm': ab_gate_norm, 'ab_w_out': ab_w_out,
        'cd_w_in': cd_w_in, 'cd_pool_w': cd_pool_w, 'cd_pool_scale': cd_pool_scale, 'cd_q_norm': cd_q_norm,
        'cd_k_norm': cd_k_norm, 'cd_lambda_q1': cd_lambda_q1, 'cd_lambda_k1': cd_lambda_k1,
        'cd_lambda_q2': cd_lambda_q2, 'cd_lambda_k2': cd_lambda_k2, 'cd_sub_norm': cd_sub_norm,
        'cd_w_out': cd_w_out,
        'cross_norm': cross_norm, 'cross_mem_norm': cross_mem_norm, 'cross_w_q': cross_w_q,
        'cross_w_kv': cross_w_kv, 'cross_q_norm': cross_q_norm, 'cross_k_norm': cross_k_norm,
        'cross_w_o': cross_w_o,
        'ffn2_norm': ffn2_norm, 'ffn2_w_gate': ffn2_w_gate, 'ffn2_w_up': ffn2_w_up, 'ffn2_w_down': ffn2_w_down,
    }
    depth = ffn1_norm.shape[0]
    return (_trunk(x_prompt, mem_prompt, p, depth), _trunk(x_sample, mem_sample, p, depth))
```

```python
import functools
import math

import numpy as np
import jax
import jax.numpy as jnp
from jax import lax
from jax.experimental import pallas as pl
from jax.experimental.pallas import tpu as pltpu

F32 = jnp.float32
BF16 = jnp.bfloat16
EPS = 1e-6

VMEM_LIMIT_BYTES = 56 * 1024 * 1024
BF16_SUBLANE_TILE = 16

D_MODEL = 1024
FNET_HEADS = 4
FNET_HEAD_DIM = 256
FFT_INNER = 64
SSM_HEADS = 32
SSM_HEAD_DIM = 64
SSM_STATE = 128
SSM_GROUPS = 4
SSM_HEADS_PER_GROUP = SSM_HEADS // SSM_GROUPS
SSM_INNER = SSM_HEADS * SSM_HEAD_DIM
SSM_GROUP_WIDTH = SSM_INNER // SSM_GROUPS
SSM_BC_WIDTH = SSM_GROUPS * SSM_STATE
SSM_CONV_CH = SSM_INNER + 2 * SSM_BC_WIDTH
SSM_CONV = 5
SSM_CHUNK = 128
POOL_WINDOWS = (2, 4, 8, 16)
POOL_GROUP_DIM = 256
DIFF_HEADS = 8
DIFF_HEAD_DIM = 64
DIFF_V_DIM = 128
ROT_DIM = 16
ROPE_THETA = 500000.0
CROSS_HEADS = 4
CROSS_HEAD_DIM = 256
HALO = BF16_SUBLANE_TILE


def _cparams(*semantics):
    return pltpu.CompilerParams(dimension_semantics=semantics, vmem_limit_bytes=VMEM_LIMIT_BYTES)


def _dot(a, b):
    return jnp.dot(a, b, preferred_element_type=F32)


def _dot_nt(a, b):
    return lax.dot_general(a, b, (((1,), (1,)), ((), ())), preferred_element_type=F32)


def _dot_tn(a, b):
    return lax.dot_general(a, b, (((0,), (0,)), ((), ())), preferred_element_type=F32)


def _rms(x, g):
    return x * lax.rsqrt(jnp.mean(x * x, axis=-1, keepdims=True) + EPS) * g


def _silu(x):
    return x * jax.nn.sigmoid(x)


def _row_tile(n, want):
    t = min(n, want)
    assert n % t == 0, (n, t)
    return t


def _ffn_body(x_ref, g_ref, wg_ref, wu_ref, wd_ref, o_ref, xn_ref, acc_ref):
    j = pl.program_id(1)

    @pl.when(j == 0)
    def _():
        xn_ref[...] = _rms(x_ref[...], g_ref[...]).astype(BF16)
        acc_ref[...] = jnp.zeros_like(acc_ref)

    xn = xn_ref[...]
    gate = _dot(xn, wg_ref[...])
    up = _dot(xn, wu_ref[...])
    h = (_silu(gate) * up).astype(BF16)
    acc_ref[...] += _dot(h, wd_ref[...])

    @pl.when(j == pl.num_programs(1) - 1)
    def _():
        o_ref[...] = x_ref[...] + 0.5 * acc_ref[...]


def _ffn(x, g, wg, wu, wd):
    t, d = x.shape
    f = wg.shape[1]
    tm = _row_tile(t, 512)
    tf = f // 2 if (f // 2) % 128 == 0 else f
    return pl.pallas_call(
        _ffn_body,
        out_shape=jax.ShapeDtypeStruct((t, d), F32),
        grid=(t // tm, f // tf),
        in_specs=[
            pl.BlockSpec((tm, d), lambda i, j: (i, 0)),
            pl.BlockSpec((1, d), lambda i, j: (0, 0)),
            pl.BlockSpec((d, tf), lambda i, j: (0, j)),
            pl.BlockSpec((d, tf), lambda i, j: (0, j)),
            pl.BlockSpec((tf, d), lambda i, j: (j, 0)),
        ],
        out_specs=pl.BlockSpec((tm, d), lambda i, j: (i, 0)),
        scratch_shapes=[pltpu.VMEM((tm, d), BF16), pltpu.VMEM((tm, d), F32)],
        compiler_params=_cparams("parallel", "arbitrary"),
        name="ffn",
    )(x, g.reshape(1, d), wg, wu, wd)


def _ab_in_body(x_ref, g_ref, w_ref, wdt_ref, uf_ref, z_ref, xbc_ref, dt_ref):
    xn = _rms(x_ref[...], g_ref[...]).astype(BF16)
    col = 0
    for ref in (uf_ref, z_ref, xbc_ref):
        width = ref.shape[1]
        for c in range(0, width, 1024):
            ref[:, c:c + 1024] = _dot(xn, w_ref[:, col + c:col + c + 1024]).astype(ref.dtype)
        col += width
    dt_ref[...] = _dot(xn, wdt_ref[...])


def _ab_in(x, g, w_main, w_dt):
    t, d = x.shape
    tm = _row_tile(t, 512)
    n_main = w_main.shape[1]
    row = lambda i: (i, 0)
    fixed = lambda i: (0, 0)
    return pl.pallas_call(
        _ab_in_body,
        out_shape=(
            jax.ShapeDtypeStruct((t, D_MODEL), BF16),
            jax.ShapeDtypeStruct((t, SSM_INNER), BF16),
            jax.ShapeDtypeStruct((t, SSM_CONV_CH), BF16),
            jax.ShapeDtypeStruct((t, 128), F32),
        ),
        grid=(t // tm,),
        in_specs=[
            pl.BlockSpec((tm, d), row),
            pl.BlockSpec((1, d), fixed),
            pl.BlockSpec((d, n_main), fixed),
            pl.BlockSpec((d, 128), fixed),
        ],
        out_specs=(
            pl.BlockSpec((tm, D_MODEL), row),
            pl.BlockSpec((tm, SSM_INNER), row),
            pl.BlockSpec((tm, SSM_CONV_CH), row),
            pl.BlockSpec((tm, 128), row),
        ),
        compiler_params=_cparams("parallel"),
        name="ab_in",
    )(x, g.reshape(1, d), w_main, w_dt)


def _dft_tables(length):
    l2 = FFT_INNER
    l1 = length // l2
    assert l1 * l2 == length
    k1 = np.arange(l1)
    ang1 = 2.0 * np.pi * ((k1[:, None] * k1[None, :]) % l1) / l1
    f1 = np.concatenate([np.cos(ang1), -np.sin(ang1)], axis=0)
    k2 = np.arange(l2)
    n2 = np.arange(l2)
    kk = k1[:, None, None] + l1 * k2[None, :, None]
    ang2 = 2.0 * np.pi * ((kk * n2[None, None, :]) % length) / length
    mr, mi = np.cos(ang2), -np.sin(ang2)
    m2 = np.concatenate([np.concatenate([mr, -mi], axis=2),
                         np.concatenate([mi, mr], axis=2)], axis=1)
    c = np.arange(FNET_HEAD_DIM)
    angc = 2.0 * np.pi * ((c[:, None] * c[None, :]) % FNET_HEAD_DIM) / FNET_HEAD_DIM
    fc = np.concatenate([np.cos(angc), np.sin(angc)], axis=0)
    return (jnp.asarray(f1, dtype=BF16), jnp.asarray(m2, dtype=BF16), jnp.asarray(fc, dtype=BF16))


def _fft1_body(f_ref, x_ref, o_ref):
    o_ref[0] = _dot(f_ref[...], x_ref[0]).astype(o_ref.dtype)


def _fft2_body(scale, m_ref, fc_ref, t_ref, o_ref):
    l2 = t_ref.shape[2]
    t = jnp.concatenate([t_ref[0, 0], t_ref[0, 1]], axis=0)
    y = _dot(m_ref[0], t)
    yr, yi = y[:l2].astype(BF16), y[l2:].astype(BF16)
    fc = fc_ref[...]
    outs = []
    for h in range(FNET_HEADS):
        sl = slice(h * FNET_HEAD_DIM, (h + 1) * FNET_HEAD_DIM)
        outs.append(_dot(jnp.concatenate([yr[:, sl], yi[:, sl]], axis=1), fc))
    o_ref[0] = (jnp.concatenate(outs, axis=1) * scale).astype(o_ref.dtype)


def _fourier(uf, nb, length):
    c = D_MODEL
    l2 = FFT_INNER
    l1 = length // l2
    f1, m2, fc = _dft_tables(length)
    x1 = uf.reshape(nb, l1, l2 * c)
    tcol = min(l2 * c, 8192)
    t = pl.pallas_call(
        _fft1_body,
        out_shape=jax.ShapeDtypeStruct((nb, 2 * l1, l2 * c), BF16),
        grid=(nb, (l2 * c) // tcol),
        in_specs=[pl.BlockSpec((2 * l1, l1), lambda b, j: (0, 0)),
                  pl.BlockSpec((1, l1, tcol), lambda b, j: (b, 0, j))],
        out_specs=pl.BlockSpec((1, 2 * l1, tcol), lambda b, j: (b, 0, j)),
        compiler_params=_cparams("parallel", "parallel"),
        name="fft_stage1",
    )(f1, x1)
    t5 = t.reshape(nb, 2, l1, l2, c)
    scale = 1.0 / math.sqrt(length * FNET_HEAD_DIM)
    y = pl.pallas_call(
        functools.partial(_fft2_body, scale),
        out_shape=jax.ShapeDtypeStruct((nb, l2, l1 * c), BF16),
        grid=(nb, l1),
        in_specs=[pl.BlockSpec((1, 2 * l2, 2 * l2), lambda b, k: (k, 0, 0)),
                  pl.BlockSpec((2 * FNET_HEAD_DIM, FNET_HEAD_DIM), lambda b, k: (0, 0)),
                  pl.BlockSpec((1, 2, None, l2, c), lambda b, k: (b, 0, k, 0, 0))],
        out_specs=pl.BlockSpec((1, l2, c), lambda b, k: (b, 0, k)),
        compiler_params=_cparams("parallel", "parallel"),
        name="fft_stage2",
    )(m2, fc, t5)
    return y.reshape(nb * length, c)


def _halo_specs(tr, width, length):
    per = tr // HALO
    last = length // HALO - 1
    return [
        pl.BlockSpec((1, HALO, width), lambda b, i: (b, jnp.maximum(i * per - 1, 0), 0)),
        pl.BlockSpec((1, tr, width), lambda b, i: (b, i, 0)),
        pl.BlockSpec((1, HALO, width), lambda b, i: (b, jnp.minimum((i + 1) * per, last), 0)),
    ]


def _with_halo(prev_ref, main_ref, next_ref, cols):
    i = pl.program_id(1)
    keep_prev = (i > 0).astype(F32)
    keep_next = (i < pl.num_programs(1) - 1).astype(F32)
    return jnp.concatenate([
        prev_ref[0, :, cols].astype(F32) * keep_prev,
        main_ref[0, :, cols].astype(F32),
        next_ref[0, :, cols].astype(F32) * keep_next,
    ], axis=0)


def _shift_rows(x, k):
    n = x.shape[0]
    return x if k % n == 0 else pltpu.roll(x, (-k) % n, 0)


def _conv_body(prev_ref, main_ref, next_ref, w_ref, b_ref, o_ref):
    tr = main_ref.shape[1]
    half = SSM_CONV // 2
    for c in range(0, SSM_CONV_CH, 512):
        cols = slice(c, c + 512)
        ext = _with_halo(prev_ref, main_ref, next_ref, cols)
        acc = jnp.zeros((tr, 512), F32)
        for j in range(SSM_CONV):
            acc = acc + _shift_rows(ext, j - half)[HALO:HALO + tr] * w_ref[j:j + 1, cols]
        o_ref[0, :, cols] = _silu(acc + b_ref[:, cols]).astype(o_ref.dtype)


def _conv_silu(xbc, conv_w, conv_b):
    nb, length, ch = xbc.shape
    tr = _row_tile(length, 256)
    fixed = lambda b, i: (0, 0)
    return pl.pallas_call(
        _conv_body,
        out_shape=jax.ShapeDtypeStruct((nb, length, ch), BF16),
        grid=(nb, length // tr),
        in_specs=_halo_specs(tr, ch, length) + [pl.BlockSpec((SSM_CONV, ch), fixed),
                                                pl.BlockSpec((1, ch), fixed)],
        out_specs=pl.BlockSpec((1, tr, ch), lambda b, i: (b, i, 0)),
        compiler_params=_cparams("parallel", "parallel"),
        name="ssd_conv",
    )(xbc, xbc, xbc, conv_w, conv_b.reshape(1, ch))


def _split_bf16(v, pieces):
    out = []
    for _ in range(pieces):
        p = v.astype(BF16)
        out.append(p)
        v = v - p.astype(F32)
    return out


def _ssd_chunk(reverse, x_ref, b_ref, c_ref, dt_ref, dtb_ref, alog_ref, expand_ref, state_ref):
    q = SSM_CHUNK
    n = SSM_STATE
    assert q == 128 and n == 128
    ch0 = SSM_HEADS * (1 if reverse else 0)
    x_b = x_ref[0]
    raw = dt_ref[0] + dtb_ref[...]
    e = jnp.exp(-jnp.abs(raw))
    u = 1.0 + e
    um1 = u - 1.0
    dt = jnp.maximum(raw, 0.0) + jnp.where(um1 == 0.0, e, jnp.log(u) * (e / jnp.where(um1 == 0.0, 1.0, um1)))
    da = dt * (-LOG2E * jnp.exp(alog_ref[...]))
    row = lax.broadcasted_iota(jnp.int32, (q, q), 0)
    col = lax.broadcasted_iota(jnp.int32, (q, q), 1)
    mask = (col >= row) if reverse else (col <= row)
    tri = jnp.where(mask, 1.0, 0.0).astype(BF16)
    acum = sum(_dot(tri, p) for p in _split_bf16(da, 3))
    src_t = (acum - jnp.log2(dt)).T
    total = acum[0:1] if reverse else acum[q - 1:q]
    expand = expand_ref[...]
    step_w = _dot((dt * jnp.exp2(total - acum)).astype(BF16), expand)
    xw = (x_b.astype(F32) * step_w).astype(BF16)
    e_total = jnp.exp2(jnp.broadcast_to(total, (8, 128)))
    e_total = sum(_dot(p, expand) for p in _split_bf16(e_total, 3))[0:1]

    first_head = lax.broadcasted_iota(jnp.int32, (q, 2 * SSM_HEAD_DIM), 1) < SSM_HEAD_DIM
    ys = []
    for g in range(SSM_GROUPS):
        bg = b_ref[0, :, g * n:(g + 1) * n]
        cg = c_ref[0, :, g * n:(g + 1) * n]
        scores = _dot_nt(cg, bg)
        cg_f = cg.astype(F32)
        state = state_ref[g]
        state_b = state.astype(BF16)
        pairs = []
        for j in range(SSM_HEADS_PER_GROUP // 2):
            lhs = []
            for h in (2 * j, 2 * j + 1):
                ch = ch0 + g * SSM_HEADS_PER_GROUP + h
                a_l = jnp.broadcast_to(acum[:, ch:ch + 1], (q, q))
                decay_dt = jnp.exp2(jnp.where(mask, a_l - src_t[ch:ch + 1, :], -jnp.inf))
                s_h = (scores * decay_dt).astype(BF16)
                c_h = (cg_f * jnp.exp2(a_l)).astype(BF16)
                lhs.append(jnp.concatenate([s_h, c_h], axis=1))
            lanes = slice(g * SSM_GROUP_WIDTH + 128 * j, g * SSM_GROUP_WIDTH + 128 * (j + 1))
            rhs = jnp.concatenate([x_b[:, lanes], state_b[:, 128 * j:128 * (j + 1)]], axis=0)
            out = _dot(jnp.concatenate(lhs, axis=0), rhs)
            pairs.append(jnp.where(first_head, out[:q], out[q:]))
        ys.append(jnp.concatenate(pairs, axis=1))
        gcols = slice(g * SSM_GROUP_WIDTH, (g + 1) * SSM_GROUP_WIDTH)
        state_ref[g] = state * e_total[:, gcols] + _dot_tn(bg, xw[:, gcols])
    return jnp.concatenate(ys, axis=1)


def _ssd_body(xf_ref, bf_ref, cf_ref, dtf_ref, xr_ref, br_ref, cr_ref, dtr_ref, dtb_ref, alog_ref,
              ef_ref, er_ref, dskip_ref, yf_ref, yr_ref, sf_ref, sr_ref):
    @pl.when(pl.program_id(1) == 0)
    def _():
        sf_ref[...] = jnp.zeros_like(sf_ref)
        sr_ref[...] = jnp.zeros_like(sr_ref)

    yf = _ssd_chunk(False, xf_ref, bf_ref, cf_ref, dtf_ref, dtb_ref, alog_ref, ef_ref, sf_ref)
    yr = _ssd_chunk(True, xr_ref, br_ref, cr_ref, dtr_ref, dtb_ref, alog_ref, er_ref, sr_ref)
    yf_ref[0] = yf.astype(yf_ref.dtype)
    yr_ref[0] = (yr + dskip_ref[...] * xr_ref[0].astype(F32)).astype(yr_ref.dtype)


def _ssd(xbc, dt, dt_bias, a_log, d_skip):
    nb, length, _ = xbc.shape
    q = SSM_CHUNK
    nc = length // q
    fixed = lambda b, c: (0, 0)
    state = pltpu.VMEM((SSM_GROUPS, SSM_STATE, SSM_GROUP_WIDTH), F32)
    pad = 128 - 2 * SSM_HEADS
    dtb = jnp.pad(dt_bias.reshape(1, 2 * SSM_HEADS), ((0, 0), (0, pad)))
    alog = jnp.pad(a_log.reshape(1, 2 * SSM_HEADS), ((0, 0), (0, pad)))

    def head_expand(direction):
        e = np.zeros((128, SSM_INNER), np.float32)
        for h in range(SSM_HEADS):
            e[direction * SSM_HEADS + h, h * SSM_HEAD_DIM:(h + 1) * SSM_HEAD_DIM] = 1.0
        return jnp.asarray(e, BF16)

    def specs(chunk):
        return [
            pl.BlockSpec((1, q, SSM_INNER), lambda b, c: (b, chunk(c), 0)),
            pl.BlockSpec((1, q, SSM_BC_WIDTH), lambda b, c: (b, chunk(c), SSM_INNER // SSM_BC_WIDTH)),
            pl.BlockSpec((1, q, SSM_BC_WIDTH), lambda b, c: (b, chunk(c), SSM_INNER // SSM_BC_WIDTH + 1)),
            pl.BlockSpec((1, q, 128), lambda b, c: (b, chunk(c), 0)),
        ]

    fw = lambda c: c
    rv = lambda c: nc - 1 - c
    dskip = jnp.repeat(d_skip, SSM_HEAD_DIM).reshape(1, SSM_INNER)
    out = jax.ShapeDtypeStruct((nb, length, SSM_INNER), BF16)
    return pl.pallas_call(
        _ssd_body,
        out_shape=(out, out),
        grid=(nb, nc),
        in_specs=specs(fw) + specs(rv) + [
            pl.BlockSpec((1, 128), fixed),
            pl.BlockSpec((1, 128), fixed),
            pl.BlockSpec((128, SSM_INNER), fixed),
            pl.BlockSpec((128, SSM_INNER), fixed),
            pl.BlockSpec((1, SSM_INNER), fixed),
        ],
        out_specs=(pl.BlockSpec((1, q, SSM_INNER), lambda b, c: (b, fw(c), 0)),
                   pl.BlockSpec((1, q, SSM_INNER), lambda b, c: (b, rv(c), 0))),
        scratch_shapes=[state, state],
        compiler_params=_cparams("parallel", "arbitrary"),
        name="ssd_scan",
    )(xbc, xbc, xbc, dt, xbc, xbc, xbc, dt, dtb, alog, head_expand(0), head_expand(1), dskip)


def _ab_out_body(x_ref, yf_ref, sf_ref, sr_ref, z_ref, gn_ref, w_ref, o_ref):
    gw = SSM_GROUP_WIDTH
    acc = x_ref[...] + _dot(yf_ref[...], w_ref[0:D_MODEL, :])
    for g in range(SSM_GROUPS):
        cols = slice(g * gw, (g + 1) * gw)
        y = (sf_ref[:, cols].astype(F32) + sr_ref[:, cols].astype(F32)) * _silu(z_ref[:, cols].astype(F32))
        yn = _rms(y, gn_ref[:, cols]).astype(BF16)
        acc = acc + _dot(yn, w_ref[D_MODEL + g * gw:D_MODEL + (g + 1) * gw, :])
    o_ref[...] = acc


def _ab_out(x, y_four, y_fw, y_bw, z, gate_norm, w_out):
    t, d = x.shape
    tm = _row_tile(t, 512)
    row = lambda i: (i, 0)
    fixed = lambda i: (0, 0)
    return pl.pallas_call(
        _ab_out_body,
        out_shape=jax.ShapeDtypeStruct((t, d), F32),
        grid=(t // tm,),
        in_specs=[
            pl.BlockSpec((tm, d), row),
            pl.BlockSpec((tm, D_MODEL), row),
            pl.BlockSpec((tm, SSM_INNER), row),
            pl.BlockSpec((tm, SSM_INNER), row),
            pl.BlockSpec((tm, SSM_INNER), row),
            pl.BlockSpec((1, SSM_INNER), fixed),
            pl.BlockSpec(w_out.shape, fixed),
        ],
        out_specs=pl.BlockSpec((tm, d), row),
        compiler_params=_cparams("parallel"),
        name="ab_out",
    )(x, y_four, y_fw, y_bw, z, gate_norm.reshape(1, SSM_INNER), w_out)


def _rope_tables(length):
    inv = ROPE_THETA ** (-jnp.arange(0, ROT_DIM, 2, dtype=F32) / ROT_DIM)
    ang = jnp.arange(length, dtype=F32)[:, None] * inv[None, :]
    cos, sin = jnp.cos(ang), jnp.sin(ang)
    half = ROT_DIM // 2
    pad = DIFF_HEAD_DIM - ROT_DIM
    ones = jnp.ones((length, pad), F32)
    zeros = jnp.zeros((length, pad), F32)
    zh = jnp.zeros((length, half), F32)
    c_self = jnp.concatenate([cos, cos, ones], axis=1)
    c_up = jnp.concatenate([-sin, zh, zeros], axis=1)
    c_down = jnp.concatenate([zh, sin, zeros], axis=1)
    rep = 128 // DIFF_HEAD_DIM
    return tuple(jnp.tile(tb, (1, rep)) for tb in (c_self, c_up, c_down))


def _cd_in_body(x_ref, g_ref, w_ref, ones_ref, qg_ref, kg_ref, cs_ref, cu_ref, cd_ref,
                up_ref, q_ref, k_ref, v_ref):
    xn = _rms(x_ref[0], g_ref[...]).astype(BF16)
    d = D_MODEL
    half = ROT_DIM // 2
    rep = d // 128
    c_self = jnp.tile(cs_ref[...], (1, rep))
    c_up = jnp.tile(cu_ref[...], (1, rep))
    c_down = jnp.tile(cd_ref[...], (1, rep))

    def qk_norm_rope(t, gain):
        sq = (t * t).astype(BF16)
        ms = jnp.concatenate([_dot(sq[:, c:c + 256], ones_ref[...]) for c in range(0, d, 256)], axis=1)
        t = t * lax.rsqrt(ms * (1.0 / DIFF_HEAD_DIM) + EPS) * gain
        return t * c_self + pltpu.roll(t, d - half, 1) * c_up + pltpu.roll(t, half, 1) * c_down

    up_ref[0] = _dot(xn, w_ref[:, 0:d]).astype(BF16)
    q = qk_norm_rope(_dot(xn, w_ref[:, d:2 * d]), qg_ref[...])
    q_ref[0] = (q * (LOG2E * DIFF_HEAD_DIM ** -0.5)).astype(BF16)
    k_ref[0] = qk_norm_rope(_dot(xn, w_ref[:, 2 * d:3 * d]), kg_ref[...]).astype(BF16)
    v_ref[0] = _dot(xn, w_ref[:, 3 * d:4 * d]).astype(BF16)


def _cd_in(x, g, w_in, q_norm, k_norm):
    nb, length, d = x.shape
    tm = _row_tile(length, 512)
    fixed = lambda b, i: (0, 0)
    tile = lambda b, i: (b, i, 0)
    pos = lambda b, i: (i, 0)
    ones_blk = jnp.asarray(np.kron(np.eye(256 // DIFF_HEAD_DIM), np.ones((DIFF_HEAD_DIM, DIFF_HEAD_DIM))), BF16)
    qg = jnp.tile(q_norm, d // DIFF_HEAD_DIM).reshape(1, d)
    kg = jnp.tile(k_norm, d // DIFF_HEAD_DIM).reshape(1, d)
    out = jax.ShapeDtypeStruct((nb, length, d), BF16)
    return pl.pallas_call(
        _cd_in_body,
        out_shape=(out, out, out, out),
        grid=(nb, length // tm),
        in_specs=[
            pl.BlockSpec((1, tm, d), tile),
            pl.BlockSpec((1, d), fixed),
            pl.BlockSpec(w_in.shape, fixed),
            pl.BlockSpec((256, 256), fixed),
            pl.BlockSpec((1, d), fixed),
            pl.BlockSpec((1, d), fixed),
            pl.BlockSpec((tm, 128), pos),
            pl.BlockSpec((tm, 128), pos),
            pl.BlockSpec((tm, 128), pos),
        ],
        out_specs=tuple(pl.BlockSpec((1, tm, d), tile) for _ in range(4)),
        compiler_params=_cparams("parallel", "parallel"),
        name="cd_in",
    )(x, g.reshape(1, d), w_in, ones_blk, qg, kg, *_rope_tables(length))


def _pool_body(length, prev_ref, main_ref, next_ref, w_ref, s_ref, o_ref):
    tr = main_ref.shape[1]
    gd = POOL_GROUP_DIM
    pos = pl.program_id(1) * tr + lax.broadcasted_iota(jnp.int32, (tr, 1), 0)
    ext = _with_halo(prev_ref, main_ref, next_ref, slice(None))
    win = ext + _shift_rows(ext, -1)
    outs = []
    for g, w in enumerate(POOL_WINDOWS):
        if g > 0:
            win = win[:, gd:]
            win = _shift_rows(win, -(w // 4)) + _shift_rows(win, w // 4)
        lo = jnp.maximum(pos - w // 2, 0)
        hi = jnp.minimum(pos + w // 2 - 1, length - 1)
        mean = win[HALO:HALO + tr, :gd] / (hi - lo + 1).astype(F32)
        centred = (mean - ext[HALO:HALO + tr, g * gd:(g + 1) * gd]).astype(BF16)
        outs.append(_dot(centred, w_ref[g]))
    o_ref[0] = (jnp.concatenate(outs, axis=1) * s_ref[...]).astype(o_ref.dtype)


def _pool(up, pool_w, pool_scale):
    nb, length, d = up.shape
    tr = _row_tile(length, 256)
    return pl.pallas_call(
        functools.partial(_pool_body, length),
        out_shape=jax.ShapeDtypeStruct((nb, length, d), BF16),
        grid=(nb, length // tr),
        in_specs=_halo_specs(tr, d, length) + [
            pl.BlockSpec(pool_w.shape, lambda b, i: (0, 0, 0)),
            pl.BlockSpec((1, d), lambda b, i: (0, 0)),
        ],
        out_specs=pl.BlockSpec((1, tr, d), lambda b, i: (b, i, 0)),
        compiler_params=_cparams("parallel", "parallel"),
        name="pool",
    )(up, up, up, pool_w, pool_scale.reshape(1, d))


ATTN_Q_TILE = 1024
ATTN_KV_TILE = 2048
ATTN_ROW_BLOCK = 256
LOG2E = math.log2(math.e)


def _diff_attn_body(lambda_init, q_ref, k_ref, v_ref, lam_ref, sub_ref, o_ref, qs_ref, m_ref, acc_ref):
    kv = pl.program_id(3)
    tq = q_ref.shape[1]
    tk = k_ref.shape[1]
    vd = DIFF_V_DIM

    @pl.when(kv == 0)
    def _():
        q = q_ref[0]
        lane = lax.broadcasted_iota(jnp.int32, q.shape, 1)
        zero = jnp.zeros_like(q)
        qs_ref[0:tq] = jnp.where(lane < DIFF_HEAD_DIM, q, zero)
        qs_ref[tq:2 * tq] = jnp.where(lane >= DIFF_HEAD_DIM, q, zero)
        m_ref[...] = jnp.full_like(m_ref, -jnp.inf)
        acc_ref[...] = jnp.zeros_like(acc_ref)

    k = k_ref[0]
    v_ext = jnp.concatenate([v_ref[0], jnp.ones((tk, vd), BF16)], axis=1)
    rb = min(ATTN_ROW_BLOCK, 2 * tq)
    for r in range(0, 2 * tq, rb):
        rows = slice(r, r + rb)
        s = _dot_nt(qs_ref[rows], k)
        m_prev = m_ref[rows]
        m_next = jnp.maximum(m_prev, jnp.max(s, axis=1, keepdims=True))
        alpha = jnp.exp2(m_prev - m_next)
        p = jnp.exp2(s - jnp.concatenate([m_next] * (tk // 128), axis=1))
        acc_ref[rows] = acc_ref[rows] * jnp.concatenate([alpha, alpha], axis=1) + _dot(p.astype(BF16), v_ext)
        m_ref[rows] = m_next

    @pl.when(kv == pl.num_programs(3) - 1)
    def _():
        acc = acc_ref[...]
        o = acc[:, 0:vd] / acc[:, vd:2 * vd]
        lv = lam_ref[...]
        lam = (jnp.exp(jnp.sum(lv[0:1] * lv[1:2], axis=-1, keepdims=True))
               - jnp.exp(jnp.sum(lv[2:3] * lv[3:4], axis=-1, keepdims=True)) + lambda_init)
        diff = o[0:tq] - lam * o[tq:2 * tq]
        o_ref[0] = (_rms(diff, sub_ref[...]) * (1.0 - lambda_init)).astype(o_ref.dtype)


def _diff_attn(q, k, v, lam_vecs, sub_norm, lambda_init):
    nb, length, d = q.shape
    tq = _row_tile(length, ATTN_Q_TILE)
    tk = _row_tile(length, ATTN_KV_TILE)
    hw = 2 * DIFF_HEAD_DIM
    assert hw == 128 and DIFF_V_DIM == 128
    return pl.pallas_call(
        functools.partial(_diff_attn_body, lambda_init),
        out_shape=jax.ShapeDtypeStruct((nb, length, d), BF16),
        grid=(nb, DIFF_HEADS, length // tq, length // tk),
        in_specs=[
            pl.BlockSpec((1, tq, hw), lambda b, h, i, j: (b, i, h)),
            pl.BlockSpec((1, tk, hw), lambda b, h, i, j: (b, j, h)),
            pl.BlockSpec((1, tk, DIFF_V_DIM), lambda b, h, i, j: (b, j, h)),
            pl.BlockSpec((4, DIFF_HEAD_DIM), lambda b, h, i, j: (0, 0)),
            pl.BlockSpec((1, DIFF_V_DIM), lambda b, h, i, j: (0, 0)),
        ],
        out_specs=pl.BlockSpec((1, tq, DIFF_V_DIM), lambda b, h, i, j: (b, i, h)),
        scratch_shapes=[
            pltpu.VMEM((2 * tq, hw), BF16),
            pltpu.VMEM((2 * tq, 128), F32),
            pltpu.VMEM((2 * tq, 2 * DIFF_V_DIM), F32),
        ],
        compiler_params=_cparams("parallel", "parallel", "parallel", "arbitrary"),
        name="diff_attn",
    )(q, k, v, lam_vecs, sub_norm.reshape(1, DIFF_V_DIM))


def _cd_out_body(x_ref, yp_ref, o_ref_in, w_ref, o_ref):
    d = D_MODEL
    o_ref[...] = x_ref[...] + _dot(yp_ref[...], w_ref[0:d, :]) + _dot(o_ref_in[...], w_ref[d:2 * d, :])


def _cd_out(x, y_pool, o, w_out):
    t, d = x.shape
    tm = _row_tile(t, 512)
    row = lambda i: (i, 0)
    return pl.pallas_call(
        _cd_out_body,
        out_shape=jax.ShapeDtypeStruct((t, d), F32),
        grid=(t // tm,),
        in_specs=[pl.BlockSpec((tm, d), row), pl.BlockSpec((tm, d), row), pl.BlockSpec((tm, d), row),
                  pl.BlockSpec(w_out.shape, lambda i: (0, 0))],
        out_specs=pl.BlockSpec((tm, d), row),
        compiler_params=_cparams("parallel"),
        name="cd_out",
    )(x, y_pool, o, w_out)


def _mem_kv_body(m_ref, g_ref, w_ref, kg_ref, k_ref, v_ref):
    d = D_MODEL
    mn = _rms(m_ref[0], g_ref[...]).astype(BF16)
    k = _dot(mn, w_ref[:, 0:d])
    hd = CROSS_HEAD_DIM
    k_ref[0] = jnp.concatenate(
        [_rms(k[:, h * hd:(h + 1) * hd], kg_ref[...]) for h in range(CROSS_HEADS)], axis=1).astype(BF16)
    v_ref[0] = _dot(mn, w_ref[:, d:2 * d]).astype(BF16)


def _mem_kv(mem, g, w_kv, k_norm):
    nb, n_mem, d = mem.shape
    fixed = lambda b: (0, 0)
    out = jax.ShapeDtypeStruct((nb, n_mem, d), BF16)
    blk = pl.BlockSpec((1, n_mem, d), lambda b: (b, 0, 0))
    return pl.pallas_call(
        _mem_kv_body,
        out_shape=(out, out),
        grid=(nb,),
        in_specs=[blk, pl.BlockSpec((1, d), fixed), pl.BlockSpec(w_kv.shape, fixed),
                  pl.BlockSpec((1, CROSS_HEAD_DIM), fixed)],
        out_specs=(blk, blk),
        compiler_params=_cparams("parallel"),
        name="cross_mem_kv",
    )(mem, g.reshape(1, d), w_kv, k_norm.reshape(1, CROSS_HEAD_DIM))


def _cross_body(x_ref, g_ref, wq_ref, qg_ref, k_ref, v_ref, wo_ref, o_ref):
    hd = CROSS_HEAD_DIM
    x = x_ref[0]
    q = _dot(_rms(x, g_ref[...]).astype(BF16), wq_ref[...])
    heads = []
    for h in range(CROSS_HEADS):
        cols = slice(h * hd, (h + 1) * hd)
        qh = (_rms(q[:, cols], qg_ref[...]) * (hd ** -0.5)).astype(BF16)
        s = _dot_nt(qh, k_ref[0, :, cols])
        p = jnp.exp(s - jnp.max(s, axis=-1, keepdims=True))
        p = p / jnp.sum(p, axis=-1, keepdims=True)
        heads.append(_dot(p.astype(BF16), v_ref[0, :, cols]).astype(BF16))
    o_ref[0] = x + _dot(jnp.concatenate(heads, axis=1), wo_ref[...])


def _cross(x, g, w_q, q_norm, k, v, w_o):
    nb, length, d = x.shape
    n_mem = k.shape[1]
    tm = _row_tile(length, 512)
    fixed = lambda b, i: (0, 0)
    tile = lambda b, i: (b, i, 0)
    per_batch = lambda b, i: (b, 0, 0)
    return pl.pallas_call(
        _cross_body,
        out_shape=jax.ShapeDtypeStruct((nb, length, d), F32),
        grid=(nb, length // tm),
        in_specs=[
            pl.BlockSpec((1, tm, d), tile),
            pl.BlockSpec((1, d), fixed),
            pl.BlockSpec((d, d), fixed),
            pl.BlockSpec((1, CROSS_HEAD_DIM), fixed),
            pl.BlockSpec((1, n_mem, d), per_batch),
            pl.BlockSpec((1, n_mem, d), per_batch),
            pl.BlockSpec((d, d), fixed),
        ],
        out_specs=pl.BlockSpec((1, tm, d), tile),
        compiler_params=_cparams("parallel", "parallel"),
        name="cross_attn",
    )(x, g.reshape(1, d), w_q, q_norm.reshape(1, CROSS_HEAD_DIM), k, v, w_o)


def _lambda_init(layer_idx):
    return 0.8 - 0.6 * math.exp(-0.3 * layer_idx)


def _mixer_ab(x, p, i):
    nb, length, d = x.shape
    t = nb * length
    w_in = p['ab_w_in'][i]
    n_main = D_MODEL + SSM_INNER + SSM_CONV_CH
    w_dt = jnp.pad(w_in[:, n_main:], ((0, 0), (0, 128 - 2 * SSM_HEADS)))
    uf, z, xbc, dt = _ab_in(x.reshape(t, d), p['mix_norm_l'], w_in[:, :n_main].astype(BF16), w_dt.astype(BF16))
    y_four = _fourier(uf, nb, length)
    xbc = _conv_silu(xbc.reshape(nb, length, SSM_CONV_CH), p['ab_conv_w'][i], p['ab_conv_b'][i])
    y_fw, y_bw = _ssd(xbc, dt.reshape(nb, length, 128), p['ab_dt_bias'][i], p['ab_a_log'][i], p['ab_d_skip'][i])
    out = _ab_out(x.reshape(t, d), y_four, y_fw.reshape(t, SSM_INNER), y_bw.reshape(t, SSM_INNER), z,
                  p['ab_gate_norm'][i], p['ab_w_out'][i].astype(BF16))
    return out.reshape(nb, length, d)


def _mixer_cd(x, p, i, layer_idx):
    nb, length, d = x.shape
    t = nb * length
    up, q, k, v = _cd_in(x, p['mix_norm_l'], p['cd_w_in'][i].astype(BF16), p['cd_q_norm'][i], p['cd_k_norm'][i])
    y_pool = _pool(up, p['cd_pool_w'][i].astype(BF16), p['cd_pool_scale'][i])
    lam_vecs = jnp.stack([p['cd_lambda_q1'][i], p['cd_lambda_k1'][i], p['cd_lambda_q2'][i], p['cd_lambda_k2'][i]])
    o = _diff_attn(q, k, v, lam_vecs, p['cd_sub_norm'][i], _lambda_init(layer_idx))
    out = _cd_out(x.reshape(t, d), y_pool.reshape(t, d), o.reshape(t, d), p['cd_w_out'][i].astype(BF16))
    return out.reshape(nb, length, d)


def _trunk(x, mem, p, depth):
    nb, length, d = x.shape
    t = nb * length
    for l in range(depth):
        x = _ffn(x.reshape(t, d), p['ffn1_norm'][l], p['ffn1_w_gate'][l].astype(BF16),
                 p['ffn1_w_up'][l].astype(BF16), p['ffn1_w_down'][l].astype(BF16)).reshape(nb, length, d)
        pl_ = dict(p, mix_norm_l=p['mix_norm'][l])
        if l % 2 == 0:
            x = _mixer_ab(x, pl_, l // 2)
        else:
            x = _mixer_cd(x, pl_, l // 2, l)
        mk, mv = _mem_kv(mem, p['cross_mem_norm'][l], p['cross_w_kv'][l].astype(BF16), p['cross_k_norm'][l])
        x = _cross(x, p['cross_norm'][l], p['cross_w_q'][l].astype(BF16), p['cross_q_norm'][l], mk, mv,
                   p['cross_w_o'][l].astype(BF16))
        x = _ffn(x.reshape(t, d), p['ffn2_norm'][l], p['ffn2_w_gate'][l].astype(BF16),
                 p['ffn2_w_up'][l].astype(BF16), p['ffn2_w_down'][l].astype(BF16)).reshape(nb, length, d)
    return x


def kernel(x_prompt, x_sample, mem_prompt, mem_sample, ffn1_norm, ffn1_w_gate, ffn1_w_up, ffn1_w_down, mix_norm, ab_w_in, ab_conv_w, ab_conv_b, ab_dt_bias, ab_a_log, ab_d_skip, ab_gate_norm, ab_w_out, cd_w_in, cd_pool_w, cd_pool_scale, cd_q_norm, cd_k_norm, cd_lambda_q1, cd_lambda_k1, cd_lambda_q2, cd_lambda_k2, cd_sub_norm, cd_w_out, cross_norm, cross_mem_norm, cross_w_q, cross_w_kv, cross_q_norm, cross_k_norm, cross_w_o, ffn2_norm, ffn2_w_gate, ffn2_w_up, ffn2_w_down):
    p = {
        'ffn1_norm': ffn1_norm, 'ffn1_w_gate': ffn1_w_gate, 'ffn1_w_up': ffn1_w_up, 'ffn1_w_down': ffn1_w_down,
        'mix_norm': mix_norm,
        'ab_w_in': ab_w_in, 'ab_conv_w': ab_conv_w, 'ab_conv_b': ab_conv_b, 'ab_dt_bias': ab_dt_bias,
        'ab_a_log': ab_a_log, 'ab_d_skip': ab_d_skip, 'ab_gate_norm': ab_gate_norm, 'ab_w_out': ab_w_out,
        'cd_w_in': cd_w_in, 'cd_pool_w': cd_pool_w, 'cd_pool_scale': cd_pool_scale, 'cd_q_norm': cd_q_norm,
        'cd_k_norm': cd_k_norm, 'cd_lambda_q1': cd_lambda_q1, 'cd_lambda_k1': cd_lambda_k1,
        'cd_lambda_q2': cd_lambda_q2, 'cd_lambda_k2': cd_lambda_k2, 'cd_sub_norm': cd_sub_norm,
        'cd_w_out': cd_w_out,
        'cross_norm': cross_norm, 'cross_mem_norm': cross_mem_norm, 'cross_w_q': cross_w_q,
        'cross_w_kv': cross_w_kv, 'cross_q_norm': cross_q_norm, 'cross_k_norm': cross_k_norm,
        'cross_w_o': cross_w_o,
        'ffn2_norm': ffn2_norm, 'ffn2_w_gate': ffn2_w_gate, 'ffn2_w_up': ffn2_w_up, 'ffn2_w_down': ffn2_w_down,
    }
    depth = ffn1_norm.shape[0]
    return (_trunk(x_prompt, mem_prompt, p, depth), _trunk(x_sample, mem_sample, p, depth))
```

```python
import functools
import math

import numpy as np
import jax
import jax.numpy as jnp
from jax import lax
from jax.experimental import pallas as pl
from jax.experimental.pallas import tpu as pltpu

F32 = jnp.float32
BF16 = jnp.bfloat16
EPS = 1e-6

VMEM_LIMIT_BYTES = 56 * 1024 * 1024
F32_SUBLANE_TILE = 8

D_MODEL = 1024
FNET_HEADS = 4
FNET_HEAD_DIM = 256
FFT_INNER = 64
SSM_HEADS = 32
SSM_HEAD_DIM = 64
SSM_STATE = 128
SSM_GROUPS = 4
SSM_HEADS_PER_GROUP = SSM_HEADS // SSM_GROUPS
SSM_INNER = SSM_HEADS * SSM_HEAD_DIM
SSM_GROUP_WIDTH = SSM_INNER // SSM_GROUPS
SSM_BC_WIDTH = SSM_GROUPS * SSM_STATE
SSM_CONV_CH = SSM_INNER + 2 * SSM_BC_WIDTH
SSM_CONV = 5
SSM_CHUNK = 128
POOL_WINDOWS = (2, 4, 8, 16)
POOL_GROUP_DIM = 256
DIFF_HEADS = 8
DIFF_HEAD_DIM = 64
DIFF_V_DIM = 128
ROT_DIM = 16
ROPE_THETA = 500000.0
CROSS_HEADS = 4
CROSS_HEAD_DIM = 256


def _cparams(*semantics):
    return pltpu.CompilerParams(dimension_semantics=semantics, vmem_limit_bytes=VMEM_LIMIT_BYTES)


def _dot(a, b):
    return jnp.dot(a, b, preferred_element_type=F32)


def _dot_nt(a, b):
    return lax.dot_general(a, b, (((1,), (1,)), ((), ())), preferred_element_type=F32)


def _dot_tn(a, b):
    return lax.dot_general(a, b, (((0,), (0,)), ((), ())), preferred_element_type=F32)


def _rms(x, g):
    return x * lax.rsqrt(jnp.mean(x * x, axis=-1, keepdims=True) + EPS) * g


def _silu(x):
    return x * jax.nn.sigmoid(x)


def _row_tile(n, want):
    t = min(n, want)
    assert n % t == 0, (n, t)
    return t


def _ffn_body(x_ref, g_ref, wg_ref, wu_ref, wd_ref, o_ref, xn_ref, acc_ref):
    j = pl.program_id(1)

    @pl.when(j == 0)
    def _():
        xn_ref[...] = _rms(x_ref[...], g_ref[...]).astype(BF16)
        acc_ref[...] = jnp.zeros_like(acc_ref)

    xn = xn_ref[...]
    gate = _dot(xn, wg_ref[...])
    up = _dot(xn, wu_ref[...])
    h = (_silu(gate) * up).astype(BF16)
    acc_ref[...] += _dot(h, wd_ref[...])

    @pl.when(j == pl.num_programs(1) - 1)
    def _():
        o_ref[...] = x_ref[...] + 0.5 * acc_ref[...]


def _ffn(x, g, wg, wu, wd):
    t, d = x.shape
    f = wg.shape[1]
    tm = _row_tile(t, 512)
    tf = f // 2 if (f // 2) % 128 == 0 else f
    return pl.pallas_call(
        _ffn_body,
        out_shape=jax.ShapeDtypeStruct((t, d), F32),
        grid=(t // tm, f // tf),
        in_specs=[
            pl.BlockSpec((tm, d), lambda i, j: (i, 0)),
            pl.BlockSpec((1, d), lambda i, j: (0, 0)),
            pl.BlockSpec((d, tf), lambda i, j: (0, j)),
            pl.BlockSpec((d, tf), lambda i, j: (0, j)),
            pl.BlockSpec((tf, d), lambda i, j: (j, 0)),
        ],
        out_specs=pl.BlockSpec((tm, d), lambda i, j: (i, 0)),
        scratch_shapes=[pltpu.VMEM((tm, d), BF16), pltpu.VMEM((tm, d), F32)],
        compiler_params=_cparams("parallel", "arbitrary"),
        name="ffn",
    )(x, g.reshape(1, d), wg, wu, wd)


CONV_LANES = 512


def _ab_in_body(tiles_per_seq, xp_ref, x_ref, xq_ref, g_ref, w_ref, wdt_ref, cw_ref, cb_ref,
                uf_ref, z_ref, xbc_ref, dt_ref, ext_ref):
    tm = x_ref.shape[0]
    hr = F32_SUBLANE_TILE
    i = pl.program_id(0)
    keep_prev = (i % tiles_per_seq != 0).astype(F32)
    keep_next = (i % tiles_per_seq != tiles_per_seq - 1).astype(F32)
    xn = _rms(x_ref[...], g_ref[...]).astype(BF16)
    halo = _rms(jnp.concatenate([xp_ref[...], xq_ref[...]], axis=0), g_ref[...]).astype(BF16)
    col = 0
    for ref in (uf_ref, z_ref):
        width = ref.shape[1]
        for c in range(0, width, 1024):
            ref[:, c:c + 1024] = _dot(xn, w_ref[:, col + c:col + c + 1024]).astype(ref.dtype)
        col += width
    dt_ref[...] = _dot(xn, wdt_ref[...])
    half = SSM_CONV // 2
    for n, c in enumerate(range(0, SSM_CONV_CH, CONV_LANES)):
        cols = slice(c, c + CONV_LANES)
        w = w_ref[:, col + c:col + c + CONV_LANES]
        edge = _dot(halo, w)
        ext = ext_ref.at[n % ext_ref.shape[0]]
        ext[0:hr] = edge[0:hr] * keep_prev
        ext[hr:hr + tm] = _dot(xn, w)
        ext[hr + tm:2 * hr + tm] = edge[hr:2 * hr] * keep_next
        acc = cb_ref[:, cols] + ext[hr - half:hr - half + tm] * cw_ref[0:1, cols]
        for j in range(1, SSM_CONV):
            acc = acc + ext[hr - half + j:hr - half + j + tm] * cw_ref[j:j + 1, cols]
        xbc_ref[:, cols] = _silu(acc).astype(xbc_ref.dtype)


def _ab_in(x, g, w_main, w_dt, conv_w, conv_b, length):
    t, d = x.shape
    tm = _row_tile(length, 512)
    hr = F32_SUBLANE_TILE
    per = tm // hr
    last = t // hr - 1
    n_main = w_main.shape[1]
    row = lambda i: (i, 0)
    fixed = lambda i: (0, 0)
    return pl.pallas_call(
        functools.partial(_ab_in_body, length // tm),
        out_shape=(
            jax.ShapeDtypeStruct((t, D_MODEL), BF16),
            jax.ShapeDtypeStruct((t, SSM_INNER), BF16),
            jax.ShapeDtypeStruct((t, SSM_CONV_CH), BF16),
            jax.ShapeDtypeStruct((t, 128), F32),
        ),
        grid=(t // tm,),
        in_specs=[
            pl.BlockSpec((hr, d), lambda i: (jnp.maximum(i * per - 1, 0), 0)),
            pl.BlockSpec((tm, d), row),
            pl.BlockSpec((hr, d), lambda i: (jnp.minimum((i + 1) * per, last), 0)),
            pl.BlockSpec((1, d), fixed),
            pl.BlockSpec((d, n_main), fixed),
            pl.BlockSpec((d, 128), fixed),
            pl.BlockSpec((SSM_CONV, SSM_CONV_CH), fixed),
            pl.BlockSpec((1, SSM_CONV_CH), fixed),
        ],
        out_specs=(
            pl.BlockSpec((tm, D_MODEL), row),
            pl.BlockSpec((tm, SSM_INNER), row),
            pl.BlockSpec((tm, SSM_CONV_CH), row),
            pl.BlockSpec((tm, 128), row),
        ),
        scratch_shapes=[pltpu.VMEM((2, tm + 2 * hr, CONV_LANES), F32)],
        compiler_params=_cparams("parallel"),
        name="ab_in",
    )(x, x, x, g.reshape(1, d), w_main, w_dt, conv_w, conv_b.reshape(1, SSM_CONV_CH))


def _dft_tables(length):
    l2 = FFT_INNER
    l1 = length // l2
    assert l1 * l2 == length
    k1 = np.arange(l1)
    ang1 = 2.0 * np.pi * ((k1[:, None] * k1[None, :]) % l1) / l1
    f1 = np.concatenate([np.cos(ang1), -np.sin(ang1)], axis=0)
    k2 = np.arange(l2)
    n2 = np.arange(l2)
    kk = k1[:, None, None] + l1 * k2[None, :, None]
    ang2 = 2.0 * np.pi * ((kk * n2[None, None, :]) % length) / length
    mr, mi = np.cos(ang2), -np.sin(ang2)
    m2 = np.concatenate([np.concatenate([mr, -mi], axis=2),
                         np.concatenate([mi, mr], axis=2)], axis=1)
    c = np.arange(FNET_HEAD_DIM)
    angc = 2.0 * np.pi * ((c[:, None] * c[None, :]) % FNET_HEAD_DIM) / FNET_HEAD_DIM
    fc = np.concatenate([np.cos(angc), np.sin(angc)], axis=0)
    return (jnp.asarray(f1, dtype=BF16), jnp.asarray(m2, dtype=BF16), jnp.asarray(fc, dtype=BF16))


def _fft1_body(f_ref, x_ref, o_ref):
    o_ref[0] = _dot(f_ref[...], x_ref[0]).astype(o_ref.dtype)


FFT_K1_PER_STEP = 8


def _fft2_body(scale, m_ref, fc_ref, t_ref, o_ref):
    nk, l2, c = t_ref.shape[2:]
    fc = fc_ref[...]
    for j in range(nk):
        t = jnp.concatenate([t_ref[0, 0, j], t_ref[0, 1, j]], axis=0)
        y = _dot(m_ref[j], t)
        yr, yi = y[:l2].astype(BF16), y[l2:].astype(BF16)
        outs = []
        for h in range(FNET_HEADS):
            sl = slice(h * FNET_HEAD_DIM, (h + 1) * FNET_HEAD_DIM)
            outs.append(_dot(jnp.concatenate([yr[:, sl], yi[:, sl]], axis=1), fc))
        o_ref[0, :, j * c:(j + 1) * c] = (jnp.concatenate(outs, axis=1) * scale).astype(o_ref.dtype)


def _fourier(uf, nb, length):
    c = D_MODEL
    l2 = FFT_INNER
    l1 = length // l2
    f1, m2, fc = _dft_tables(length)
    x1 = uf.reshape(nb, l1, l2 * c)
    tcol = min(l2 * c, 8192)
    t = pl.pallas_call(
        _fft1_body,
        out_shape=jax.ShapeDtypeStruct((nb, 2 * l1, l2 * c), BF16),
        grid=(nb, (l2 * c) // tcol),
        in_specs=[pl.BlockSpec((2 * l1, l1), lambda b, j: (0, 0)),
                  pl.BlockSpec((1, l1, tcol), lambda b, j: (b, 0, j))],
        out_specs=pl.BlockSpec((1, 2 * l1, tcol), lambda b, j: (b, 0, j)),
        compiler_params=_cparams("parallel", "parallel"),
        name="fft_stage1",
    )(f1, x1)
    t5 = t.reshape(nb, 2, l1, l2, c)
    nk = math.gcd(l1, FFT_K1_PER_STEP)
    scale = 1.0 / math.sqrt(length * FNET_HEAD_DIM)
    y = pl.pallas_call(
        functools.partial(_fft2_body, scale),
        out_shape=jax.ShapeDtypeStruct((nb, l2, l1 * c), BF16),
        grid=(nb, l1 // nk),
        in_specs=[pl.BlockSpec((nk, 2 * l2, 2 * l2), lambda b, k: (k, 0, 0)),
                  pl.BlockSpec((2 * FNET_HEAD_DIM, FNET_HEAD_DIM), lambda b, k: (0, 0)),
                  pl.BlockSpec((1, 2, nk, l2, c), lambda b, k: (b, 0, k, 0, 0))],
        out_specs=pl.BlockSpec((1, l2, nk * c), lambda b, k: (b, 0, k)),
        compiler_params=_cparams("parallel", "parallel"),
        name="fft_stage2",
    )(m2, fc, t5)
    return y.reshape(nb * length, c)


def _shift_rows(x, k):
    n = x.shape[0]
    return x if k % n == 0 else pltpu.roll(x, (-k) % n, 0)


def _split_bf16(v, pieces):
    out = []
    for _ in range(pieces):
        p = v.astype(BF16)
        out.append(p)
        v = v - p.astype(F32)
    return out


def _ssd_chunk(reverse, x_ref, b_ref, c_ref, dt_ref, dtb_ref, alog_ref, expand_ref, state_ref):
    q = SSM_CHUNK
    n = SSM_STATE
    assert q == 128 and n == 128
    ch0 = SSM_HEADS * (1 if reverse else 0)
    x_b = x_ref[0]
    raw = dt_ref[0] + dtb_ref[...]
    e = jnp.exp(-jnp.abs(raw))
    u = 1.0 + e
    um1 = u - 1.0
    dt = jnp.maximum(raw, 0.0) + jnp.where(um1 == 0.0, e, jnp.log(u) * (e / jnp.where(um1 == 0.0, 1.0, um1)))
    da = dt * (-LOG2E * jnp.exp(alog_ref[...]))
    row = lax.broadcasted_iota(jnp.int32, (q, q), 0)
    col = lax.broadcasted_iota(jnp.int32, (q, q), 1)
    mask = (col >= row) if reverse else (col <= row)
    tri = jnp.where(mask, 1.0, 0.0).astype(BF16)
    acum = sum(_dot(tri, p) for p in _split_bf16(da, 3))
    src_t = (acum - jnp.log2(dt)).T
    total = acum[0:1] if reverse else acum[q - 1:q]
    expand = expand_ref[...]
    step_w = _dot((dt * jnp.exp2(total - acum)).astype(BF16), expand)
    xw = (x_b.astype(F32) * step_w).astype(BF16)
    e_total = jnp.exp2(jnp.broadcast_to(total, (8, 128)))
    e_total = sum(_dot(p, expand) for p in _split_bf16(e_total, 3))[0:1]

    first_head = lax.broadcasted_iota(jnp.int32, (q, 2 * SSM_HEAD_DIM), 1) < SSM_HEAD_DIM
    ys = []
    for g in range(SSM_GROUPS):
        bg = b_ref[0, :, g * n:(g + 1) * n]
        cg = c_ref[0, :, g * n:(g + 1) * n]
        scores = _dot_nt(cg, bg)
        cg_f = cg.astype(F32)
        state = state_ref[g]
        state_b = state.astype(BF16)
        pairs = []
        for j in range(SSM_HEADS_PER_GROUP // 2):
            lhs = []
            for h in (2 * j, 2 * j + 1):
                ch = ch0 + g * SSM_HEADS_PER_GROUP + h
                a_l = jnp.broadcast_to(acum[:, ch:ch + 1], (q, q))
                decay_dt = jnp.exp2(jnp.where(mask, a_l - src_t[ch:ch + 1, :], -jnp.inf))
                s_h = (scores * decay_dt).astype(BF16)
                c_h = (cg_f * jnp.exp2(a_l)).astype(BF16)
                lhs.append(jnp.concatenate([s_h, c_h], axis=1))
            lanes = slice(g * SSM_GROUP_WIDTH + 128 * j, g * SSM_GROUP_WIDTH + 128 * (j + 1))
            rhs = jnp.concatenate([x_b[:, lanes], state_b[:, 128 * j:128 * (j + 1)]], axis=0)
            out = _dot(jnp.concatenate(lhs, axis=0), rhs)
            pairs.append(jnp.where(first_head, out[:q], out[q:]))
        ys.append(jnp.concatenate(pairs, axis=1))
        gcols = slice(g * SSM_GROUP_WIDTH, (g + 1) * SSM_GROUP_WIDTH)
        state_ref[g] = state * e_total[:, gcols] + _dot_tn(bg, xw[:, gcols])
    return jnp.concatenate(ys, axis=1)


def _ssd_body(xf_ref, bf_ref, cf_ref, dtf_ref, xr_ref, br_ref, cr_ref, dtr_ref, dtb_ref, alog_ref,
              ef_ref, er_ref, dskip_ref, yf_ref, yr_ref, sf_ref, sr_ref):
    @pl.when(pl.program_id(1) == 0)
    def _():
        sf_ref[...] = jnp.zeros_like(sf_ref)
        sr_ref[...] = jnp.zeros_like(sr_ref)

    yf = _ssd_chunk(False, xf_ref, bf_ref, cf_ref, dtf_ref, dtb_ref, alog_ref, ef_ref, sf_ref)
    yr = _ssd_chunk(True, xr_ref, br_ref, cr_ref, dtr_ref, dtb_ref, alog_ref, er_ref, sr_ref)
    yf_ref[0] = yf.astype(yf_ref.dtype)
    yr_ref[0] = (yr + dskip_ref[...] * xr_ref[0].astype(F32)).astype(yr_ref.dtype)


def _ssd(xbc, dt, dt_bias, a_log, d_skip):
    nb, length, _ = xbc.shape
    q = SSM_CHUNK
    nc = length // q
    fixed = lambda b, c: (0, 0)
    state = pltpu.VMEM((SSM_GROUPS, SSM_STATE, SSM_GROUP_WIDTH), F32)
    pad = 128 - 2 * SSM_HEADS
    dtb = jnp.pad(dt_bias.reshape(1, 2 * SSM_HEADS), ((0, 0), (0, pad)))
    alog = jnp.pad(a_log.reshape(1, 2 * SSM_HEADS), ((0, 0), (0, pad)))

    def head_expand(direction):
        e = np.zeros((128, SSM_INNER), np.float32)
        for h in range(SSM_HEADS):
            e[direction * SSM_HEADS + h, h * SSM_HEAD_DIM:(h + 1) * SSM_HEAD_DIM] = 1.0
        return jnp.asarray(e, BF16)

    def specs(chunk):
        return [
            pl.BlockSpec((1, q, SSM_INNER), lambda b, c: (b, chunk(c), 0)),
            pl.BlockSpec((1, q, SSM_BC_WIDTH), lambda b, c: (b, chunk(c), SSM_INNER // SSM_BC_WIDTH)),
            pl.BlockSpec((1, q, SSM_BC_WIDTH), lambda b, c: (b, chunk(c), SSM_INNER // SSM_BC_WIDTH + 1)),
            pl.BlockSpec((1, q, 128), lambda b, c: (b, chunk(c), 0)),
        ]

    fw = lambda c: c
    rv = lambda c: nc - 1 - c
    dskip = jnp.repeat(d_skip, SSM_HEAD_DIM).reshape(1, SSM_INNER)
    out = jax.ShapeDtypeStruct((nb, length, SSM_INNER), BF16)
    return pl.pallas_call(
        _ssd_body,
        out_shape=(out, out),
        grid=(nb, nc),
        in_specs=specs(fw) + specs(rv) + [
            pl.BlockSpec((1, 128), fixed),
            pl.BlockSpec((1, 128), fixed),
            pl.BlockSpec((128, SSM_INNER), fixed),
            pl.BlockSpec((128, SSM_INNER), fixed),
            pl.BlockSpec((1, SSM_INNER), fixed),
        ],
        out_specs=(pl.BlockSpec((1, q, SSM_INNER), lambda b, c: (b, fw(c), 0)),
                   pl.BlockSpec((1, q, SSM_INNER), lambda b, c: (b, rv(c), 0))),
        scratch_shapes=[state, state],
        compiler_params=_cparams("parallel", "arbitrary"),
        name="ssd_scan",
    )(xbc, xbc, xbc, dt, xbc, xbc, xbc, dt, dtb, alog, head_expand(0), head_expand(1), dskip)


def _ab_out_body(x_ref, yf_ref, sf_ref, sr_ref, z_ref, gn_ref, w_ref, o_ref):
    gw = SSM_GROUP_WIDTH
    acc = x_ref[...] + _dot(yf_ref[...], w_ref[0:D_MODEL, :])
    for g in range(SSM_GROUPS):
        cols = slice(g * gw, (g + 1) * gw)
        y = (sf_ref[:, cols].astype(F32) + sr_ref[:, cols].astype(F32)) * _silu(z_ref[:, cols].astype(F32))
        yn = _rms(y, gn_ref[:, cols]).astype(BF16)
        acc = acc + _dot(yn, w_ref[D_MODEL + g * gw:D_MODEL + (g + 1) * gw, :])
    o_ref[...] = acc


def _ab_out(x, y_four, y_fw, y_bw, z, gate_norm, w_out):
    t, d = x.shape
    tm = _row_tile(t, 512)
    row = lambda i: (i, 0)
    fixed = lambda i: (0, 0)
    return pl.pallas_call(
        _ab_out_body,
        out_shape=jax.ShapeDtypeStruct((t, d), F32),
        grid=(t // tm,),
        in_specs=[
            pl.BlockSpec((tm, d), row),
            pl.BlockSpec((tm, D_MODEL), row),
            pl.BlockSpec((tm, SSM_INNER), row),
            pl.BlockSpec((tm, SSM_INNER), row),
            pl.BlockSpec((tm, SSM_INNER), row),
            pl.BlockSpec((1, SSM_INNER), fixed),
            pl.BlockSpec(w_out.shape, fixed),
        ],
        out_specs=pl.BlockSpec((tm, d), row),
        compiler_params=_cparams("parallel"),
        name="ab_out",
    )(x, y_four, y_fw, y_bw, z, gate_norm.reshape(1, SSM_INNER), w_out)


def _rope_tables(length):
    inv = ROPE_THETA ** (-jnp.arange(0, ROT_DIM, 2, dtype=F32) / ROT_DIM)
    ang = jnp.arange(length, dtype=F32)[:, None] * inv[None, :]
    cos, sin = jnp.cos(ang), jnp.sin(ang)
    half = ROT_DIM // 2
    pad = DIFF_HEAD_DIM - ROT_DIM
    ones = jnp.ones((length, pad), F32)
    zeros = jnp.zeros((length, pad), F32)
    zh = jnp.zeros((length, half), F32)
    c_self = jnp.concatenate([cos, cos, ones], axis=1)
    c_up = jnp.concatenate([-sin, zh, zeros], axis=1)
    c_down = jnp.concatenate([zh, sin, zeros], axis=1)
    rep = 128 // DIFF_HEAD_DIM
    return tuple(jnp.tile(tb, (1, rep)) for tb in (c_self, c_up, c_down))


def _pooled(length, ext, w_ref, s_ref):
    hr = F32_SUBLANE_TILE
    tm = ext.shape[0] - 2 * hr
    gd = POOL_GROUP_DIM
    pos = pl.program_id(1) * tm + lax.broadcasted_iota(jnp.int32, (tm, 1), 0)
    win = ext + _shift_rows(ext, -1)
    outs = []
    for g, w in enumerate(POOL_WINDOWS):
        if g > 0:
            win = win[:, gd:]
            win = _shift_rows(win, -(w // 4)) + _shift_rows(win, w // 4)
        lo = jnp.maximum(pos - w // 2, 0)
        hi = jnp.minimum(pos + w // 2 - 1, length - 1)
        mean = win[hr:hr + tm, :gd] / (hi - lo + 1).astype(F32)
        centred = (mean - ext[hr:hr + tm, g * gd:(g + 1) * gd]).astype(BF16)
        outs.append(_dot(centred, w_ref[g]))
    return jnp.concatenate(outs, axis=1) * s_ref[...]


def _cd_in_body(length, xp_ref, x_ref, xq_ref, g_ref, w_ref, ones_ref, qg_ref, kg_ref, cs_ref, cu_ref, cd_ref,
                pw_ref, ps_ref, yp_ref, q_ref, k_ref, v_ref):
    hr = F32_SUBLANE_TILE
    i = pl.program_id(1)
    keep_prev = (i > 0).astype(F32)
    keep_next = (i < pl.num_programs(1) - 1).astype(F32)
    xn = _rms(x_ref[0], g_ref[...]).astype(BF16)
    halo = _rms(jnp.concatenate([xp_ref[0], xq_ref[0]], axis=0), g_ref[...]).astype(BF16)
    d = D_MODEL
    half = ROT_DIM // 2
    rep = d // 128
    c_self = jnp.tile(cs_ref[...], (1, rep))
    c_up = jnp.tile(cu_ref[...], (1, rep))
    c_down = jnp.tile(cd_ref[...], (1, rep))

    def qk_norm_rope(t, gain):
        sq = (t * t).astype(BF16)
        ms = jnp.concatenate([_dot(sq[:, c:c + 256], ones_ref[...]) for c in range(0, d, 256)], axis=1)
        t = t * lax.rsqrt(ms * (1.0 / DIFF_HEAD_DIM) + EPS) * gain
        return t * c_self + pltpu.roll(t, d - half, 1) * c_up + pltpu.roll(t, half, 1) * c_down

    edge = _dot(halo, w_ref[:, 0:d])
    ext = jnp.concatenate([edge[0:hr] * keep_prev, _dot(xn, w_ref[:, 0:d]), edge[hr:2 * hr] * keep_next], axis=0)
    yp_ref[0] = _pooled(length, ext, pw_ref, ps_ref).astype(BF16)
    q = qk_norm_rope(_dot(xn, w_ref[:, d:2 * d]), qg_ref[...])
    q_ref[0] = (q * (LOG2E * DIFF_HEAD_DIM ** -0.5)).astype(BF16)
    k_ref[0] = qk_norm_rope(_dot(xn, w_ref[:, 2 * d:3 * d]), kg_ref[...]).astype(BF16)
    v_ref[0] = _dot(xn, w_ref[:, 3 * d:4 * d]).astype(BF16)


def _cd_in(x, g, w_in, q_norm, k_norm, pool_w, pool_scale):
    nb, length, d = x.shape
    tm = _row_tile(length, 512)
    hr = F32_SUBLANE_TILE
    per = tm // hr
    last = length // hr - 1
    fixed = lambda b, i: (0, 0)
    tile = lambda b, i: (b, i, 0)
    pos = lambda b, i: (i, 0)
    ones_blk = jnp.asarray(np.kron(np.eye(256 // DIFF_HEAD_DIM), np.ones((DIFF_HEAD_DIM, DIFF_HEAD_DIM))), BF16)
    qg = jnp.tile(q_norm, d // DIFF_HEAD_DIM).reshape(1, d)
    kg = jnp.tile(k_norm, d // DIFF_HEAD_DIM).reshape(1, d)
    out = jax.ShapeDtypeStruct((nb, length, d), BF16)
    return pl.pallas_call(
        functools.partial(_cd_in_body, length),
        out_shape=(out, out, out, out),
        grid=(nb, length // tm),
        in_specs=[
            pl.BlockSpec((1, hr, d), lambda b, i: (b, jnp.maximum(i * per - 1, 0), 0)),
            pl.BlockSpec((1, tm, d), tile),
            pl.BlockSpec((1, hr, d), lambda b, i: (b, jnp.minimum((i + 1) * per, last), 0)),
            pl.BlockSpec((1, d), fixed),
            pl.BlockSpec(w_in.shape, fixed),
            pl.BlockSpec((256, 256), fixed),
            pl.BlockSpec((1, d), fixed),
            pl.BlockSpec((1, d), fixed),
            pl.BlockSpec((tm, 128), pos),
            pl.BlockSpec((tm, 128), pos),
            pl.BlockSpec((tm, 128), pos),
            pl.BlockSpec(pool_w.shape, lambda b, i: (0, 0, 0)),
            pl.BlockSpec((1, d), fixed),
        ],
        out_specs=tuple(pl.BlockSpec((1, tm, d), tile) for _ in range(4)),
        compiler_params=_cparams("parallel", "parallel"),
        name="cd_in",
    )(x, x, x, g.reshape(1, d), w_in, ones_blk, qg, kg, *_rope_tables(length), pool_w, pool_scale.reshape(1, d))


ATTN_Q_TILE = 2048
ATTN_KV_TILE = 2048
ATTN_ROW_BLOCK = 256
LOG2E = math.log2(math.e)


def _diff_attn_body(lambda_init, q_ref, k_ref, v_ref, lam_ref, sub_ref, o_ref, qs_ref, m_ref, acc_ref):
    kv = pl.program_id(3)
    tq = q_ref.shape[1]
    tk = k_ref.shape[1]
    vd = DIFF_V_DIM

    @pl.when(kv == 0)
    def _():
        q = q_ref[0]
        lane = lax.broadcasted_iota(jnp.int32, q.shape, 1)
        zero = jnp.zeros_like(q)
        qs_ref[0:tq] = jnp.where(lane < DIFF_HEAD_DIM, q, zero)
        qs_ref[tq:2 * tq] = jnp.where(lane >= DIFF_HEAD_DIM, q, zero)
        m_ref[...] = jnp.full_like(m_ref, -jnp.inf)
        acc_ref[...] = jnp.zeros_like(acc_ref)

    k = k_ref[0]
    v_ext = jnp.concatenate([v_ref[0], jnp.ones((tk, vd), BF16)], axis=1)
    rb = min(ATTN_ROW_BLOCK, 2 * tq)
    for r in range(0, 2 * tq, rb):
        rows = slice(r, r + rb)
        s = _dot_nt(qs_ref[rows], k)
        m_prev = m_ref[rows]
        m_next = jnp.maximum(m_prev, jnp.max(s, axis=1, keepdims=True))
        alpha = jnp.exp2(m_prev - m_next)
        p = jnp.exp2(s - jnp.concatenate([m_next] * (tk // 128), axis=1))
        acc_ref[rows] = acc_ref[rows] * jnp.concatenate([alpha, alpha], axis=1) + _dot(p.astype(BF16), v_ext)
        m_ref[rows] = m_next

    @pl.when(kv == pl.num_programs(3) - 1)
    def _():
        acc = acc_ref[...]
        o = acc[:, 0:vd] / acc[:, vd:2 * vd]
        lv = lam_ref[...]
        lam = (jnp.exp(jnp.sum(lv[0:1] * lv[1:2], axis=-1, keepdims=True))
               - jnp.exp(jnp.sum(lv[2:3] * lv[3:4], axis=-1, keepdims=True)) + lambda_init)
        diff = o[0:tq] - lam * o[tq:2 * tq]
        o_ref[0] = (_rms(diff, sub_ref[...]) * (1.0 - lambda_init)).astype(o_ref.dtype)


def _diff_attn(q, k, v, lam_vecs, sub_norm, lambda_init):
    nb, length, d = q.shape
    tq = _row_tile(length, ATTN_Q_TILE)
    tk = _row_tile(length, ATTN_KV_TILE)
    hw = 2 * DIFF_HEAD_DIM
    assert hw == 128 and DIFF_V_DIM == 128
    return pl.pallas_call(
        functools.partial(_diff_attn_body, lambda_init),
        out_shape=jax.ShapeDtypeStruct((nb, length, d), BF16),
        grid=(nb, DIFF_HEADS, length // tq, length // tk),
        in_specs=[
            pl.BlockSpec((1, tq, hw), lambda b, h, i, j: (b, i, h)),
            pl.BlockSpec((1, tk, hw), lambda b, h, i, j: (b, j, h)),
            pl.BlockSpec((1, tk, DIFF_V_DIM), lambda b, h, i, j: (b, j, h)),
            pl.BlockSpec((4, DIFF_HEAD_DIM), lambda b, h, i, j: (0, 0)),
            pl.BlockSpec((1, DIFF_V_DIM), lambda b, h, i, j: (0, 0)),
        ],
        out_specs=pl.BlockSpec((1, tq, DIFF_V_DIM), lambda b, h, i, j: (b, i, h)),
        scratch_shapes=[
            pltpu.VMEM((2 * tq, hw), BF16),
            pltpu.VMEM((2 * tq, 128), F32),
            pltpu.VMEM((2 * tq, 2 * DIFF_V_DIM), F32),
        ],
        compiler_params=_cparams("parallel", "parallel", "parallel", "arbitrary"),
        name="diff_attn",
    )(q, k, v, lam_vecs, sub_norm.reshape(1, DIFF_V_DIM))


def _cd_out_body(x_ref, yp_ref, o_ref_in, w_ref, o_ref):
    d = D_MODEL
    o_ref[...] = x_ref[...] + _dot(yp_ref[...], w_ref[0:d, :]) + _dot(o_ref_in[...], w_ref[d:2 * d, :])


def _cd_out(x, y_pool, o, w_out):
    t, d = x.shape
    tm = _row_tile(t, 512)
    row = lambda i: (i, 0)
    return pl.pallas_call(
        _cd_out_body,
        out_shape=jax.ShapeDtypeStruct((t, d), F32),
        grid=(t // tm,),
        in_specs=[pl.BlockSpec((tm, d), row), pl.BlockSpec((tm, d), row), pl.BlockSpec((tm, d), row),
                  pl.BlockSpec(w_out.shape, lambda i: (0, 0))],
        out_specs=pl.BlockSpec((tm, d), row),
        compiler_params=_cparams("parallel"),
        name="cd_out",
    )(x, y_pool, o, w_out)


def _mem_kv_body(m_ref, g_ref, w_ref, kg_ref, k_ref, v_ref):
    d = D_MODEL
    mn = _rms(m_ref[0], g_ref[...]).astype(BF16)
    k = _dot(mn, w_ref[:, 0:d])
    hd = CROSS_HEAD_DIM
    k_ref[0] = jnp.concatenate(
        [_rms(k[:, h * hd:(h + 1) * hd], kg_ref[...]) for h in range(CROSS_HEADS)], axis=1).astype(BF16)
    v_ref[0] = _dot(mn, w_ref[:, d:2 * d]).astype(BF16)


def _mem_kv(mem, g, w_kv, k_norm):
    nb, n_mem, d = mem.shape
    fixed = lambda b: (0, 0)
    out = jax.ShapeDtypeStruct((nb, n_mem, d), BF16)
    blk = pl.BlockSpec((1, n_mem, d), lambda b: (b, 0, 0))
    return pl.pallas_call(
        _mem_kv_body,
        out_shape=(out, out),
        grid=(nb,),
        in_specs=[blk, pl.BlockSpec((1, d), fixed), pl.BlockSpec(w_kv.shape, fixed),
                  pl.BlockSpec((1, CROSS_HEAD_DIM), fixed)],
        out_specs=(blk, blk),
        compiler_params=_cparams("parallel"),
        name="cross_mem_kv",
    )(mem, g.reshape(1, d), w_kv, k_norm.reshape(1, CROSS_HEAD_DIM))


def _cross_body(x_ref, g_ref, wq_ref, qg_ref, k_ref, v_ref, wo_ref, o_ref):
    hd = CROSS_HEAD_DIM
    x = x_ref[0]
    q = _dot(_rms(x, g_ref[...]).astype(BF16), wq_ref[...])
    heads = []
    for h in range(CROSS_HEADS):
        cols = slice(h * hd, (h + 1) * hd)
        qh = (_rms(q[:, cols], qg_ref[...]) * (hd ** -0.5)).astype(BF16)
        s = _dot_nt(qh, k_ref[0, :, cols])
        p = jnp.exp(s - jnp.max(s, axis=-1, keepdims=True))
        p = p / jnp.sum(p, axis=-1, keepdims=True)
        heads.append(_dot(p.astype(BF16), v_ref[0, :, cols]).astype(BF16))
    o_ref[0] = x + _dot(jnp.concatenate(heads, axis=1), wo_ref[...])


def _cross(x, g, w_q, q_norm, k, v, w_o):
    nb, length, d = x.shape
    n_mem = k.shape[1]
    tm = _row_tile(length, 512)
    fixed = lambda b, i: (0, 0)
    tile = lambda b, i: (b, i, 0)
    per_batch = lambda b, i: (b, 0, 0)
    return pl.pallas_call(
        _cross_body,
        out_shape=jax.ShapeDtypeStruct((nb, length, d), F32),
        grid=(nb, length // tm),
        in_specs=[
            pl.BlockSpec((1, tm, d), tile),
            pl.BlockSpec((1, d), fixed),
            pl.BlockSpec((d, d), fixed),
            pl.BlockSpec((1, CROSS_HEAD_DIM), fixed),
            pl.BlockSpec((1, n_mem, d), per_batch),
            pl.BlockSpec((1, n_mem, d), per_batch),
            pl.BlockSpec((d, d), fixed),
        ],
        out_specs=pl.BlockSpec((1, tm, d), tile),
        compiler_params=_cparams("parallel", "parallel"),
        name="cross_attn",
    )(x, g.reshape(1, d), w_q, q_norm.reshape(1, CROSS_HEAD_DIM), k, v, w_o)


def _lambda_init(layer_idx):
    return 0.8 - 0.6 * math.exp(-0.3 * layer_idx)


def _mixer_ab(x, p, i):
    nb, length, d = x.shape
    t = nb * length
    w_in = p['ab_w_in'][i]
    n_main = D_MODEL + SSM_INNER + SSM_CONV_CH
    w_dt = jnp.pad(w_in[:, n_main:], ((0, 0), (0, 128 - 2 * SSM_HEADS)))
    uf, z, xbc, dt = _ab_in(x.reshape(t, d), p['mix_norm_l'], w_in[:, :n_main].astype(BF16), w_dt.astype(BF16),
                            p['ab_conv_w'][i], p['ab_conv_b'][i], length)
    y_four = _fourier(uf, nb, length)
    y_fw, y_bw = _ssd(xbc.reshape(nb, length, SSM_CONV_CH), dt.reshape(nb, length, 128), p['ab_dt_bias'][i],
                      p['ab_a_log'][i], p['ab_d_skip'][i])
    out = _ab_out(x.reshape(t, d), y_four, y_fw.reshape(t, SSM_INNER), y_bw.reshape(t, SSM_INNER), z,
                  p['ab_gate_norm'][i], p['ab_w_out'][i].astype(BF16))
    return out.reshape(nb, length, d)


def _mixer_cd(x, p, i, layer_idx):
    nb, length, d = x.shape
    t = nb * length
    y_pool, q, k, v = _cd_in(x, p['mix_norm_l'], p['cd_w_in'][i].astype(BF16), p['cd_q_norm'][i], p['cd_k_norm'][i],
                             p['cd_pool_w'][i].astype(BF16), p['cd_pool_scale'][i])
    lam_vecs = jnp.stack([p['cd_lambda_q1'][i], p['cd_lambda_k1'][i], p['cd_lambda_q2'][i], p['cd_lambda_k2'][i]])
    o = _diff_attn(q, k, v, lam_vecs, p['cd_sub_norm'][i], _lambda_init(layer_idx))
    out = _cd_out(x.reshape(t, d), y_pool.reshape(t, d), o.reshape(t, d), p['cd_w_out'][i].astype(BF16))
    return out.reshape(nb, length, d)


def _trunk(x, mem, p, depth):
    nb, length, d = x.shape
    t = nb * length
    for l in range(depth):
        x = _ffn(x.reshape(t, d), p['ffn1_norm'][l], p['ffn1_w_gate'][l].astype(BF16),
                 p['ffn1_w_up'][l].astype(BF16), p['ffn1_w_down'][l].astype(BF16)).reshape(nb, length, d)
        pl_ = dict(p, mix_norm_l=p['mix_norm'][l])
        if l % 2 == 0:
            x = _mixer_ab(x, pl_, l // 2)
        else:
            x = _mixer_cd(x, pl_, l // 2, l)
        mk, mv = _mem_kv(mem, p['cross_mem_norm'][l], p['cross_w_kv'][l].astype(BF16), p['cross_k_norm'][l])
        x = _cross(x, p['cross_norm'][l], p['cross_w_q'][l].astype(BF16), p['cross_q_norm'][l], mk, mv,
                   p['cross_w_o'][l].astype(BF16))
        x = _ffn(x.reshape(t, d), p['ffn2_norm'][l], p['ffn2_w_gate'][l].astype(BF16),
                 p['ffn2_w_up'][l].astype(BF16), p['ffn2_w_down'][l].astype(BF16)).reshape(nb, length, d)
    return x


def kernel(x_prompt, x_sample, mem_prompt, mem_sample, ffn1_norm, ffn1_w_gate, ffn1_w_up, ffn1_w_down, mix_norm, ab_w_in, ab_conv_w, ab_conv_b, ab_dt_bias, ab_a_log, ab_d_skip, ab_gate_norm, ab_w_out, cd_w_in, cd_pool_w, cd_pool_scale, cd_q_norm, cd_k_norm, cd_lambda_q1, cd_lambda_k1, cd_lambda_q2, cd_lambda_k2, cd_sub_norm, cd_w_out, cross_norm, cross_mem_norm, cross_w_q, cross_w_kv, cross_q_norm, cross_k_norm, cross_w_o, ffn2_norm, ffn2_w_gate, ffn2_w_up, ffn2_w_down):
    p = {
        'ffn1_norm': ffn1_norm, 'ffn1_w_gate': ffn1_w_gate, 'ffn1_w_up': ffn1_w_up, 'ffn1_w_down': ffn1_w_down,
        'mix_norm': mix_norm,
        'ab_w_in': ab_w_in, 'ab_conv_w': ab_conv_w, 'ab_conv_b': ab_conv_b, 'ab_dt_bias': ab_dt_bias,
        'ab_a_log': ab_a_log, 'ab_d_skip': ab_d_skip, 'ab_gate_norm': ab_gate_norm, 'ab_w_out': ab_w_out,
        'cd_w_in': cd_w_in, 'cd_pool_w': cd_pool_w, 'cd_pool_scale': cd_pool_scale, 'cd_q_norm': cd_q_norm,
        'cd_k_norm': cd_k_norm, 'cd_lambda_q1': cd_lambda_q1, 'cd_lambda_k1': cd_lambda_k1,
        'cd_lambda_q2': cd_lambda_q2, 'cd_lambda_k2': cd_lambda_k2, 'cd_sub_norm': cd_sub_norm,
        'cd_w_out': cd_w_out,
        'cross_norm': cross_norm, 'cross_mem_norm': cross_mem_norm, 'cross_w_q': cross_w_q,
        'cross_w_kv': cross_w_kv, 'cross_q_norm': cross_q_norm, 'cross_k_norm': cross_k_norm,
        'cross_w_o': cross_w_o,
        'ffn2_norm': ffn2_norm, 'ffn2_w_gate': ffn2_w_gate, 'ffn2_w_up': ffn2_w_up, 'ffn2_w_down': ffn2_w_down,
    }
    depth = ffn1_norm.shape[0]
    return (_trunk(x_prompt, mem_prompt, p, depth), _trunk(x_sample, mem_sample, p, depth))
```

```python
import functools
import math

import numpy as np
import jax
import jax.numpy as jnp
from jax import lax
from jax.experimental import pallas as pl
from jax.experimental.pallas import tpu as pltpu

F32 = jnp.float32
BF16 = jnp.bfloat16
EPS = 1e-6

VMEM_LIMIT_BYTES = 56 * 1024 * 1024
F32_SUBLANE_TILE = 8

D_MODEL = 1024
FNET_HEADS = 4
FNET_HEAD_DIM = 256
FFT_INNER = 64
SSM_HEADS = 32
SSM_HEAD_DIM = 64
SSM_STATE = 128
SSM_GROUPS = 4
SSM_HEADS_PER_GROUP = SSM_HEADS // SSM_GROUPS
SSM_INNER = SSM_HEADS * SSM_HEAD_DIM
SSM_GROUP_WIDTH = SSM_INNER // SSM_GROUPS
SSM_BC_WIDTH = SSM_GROUPS * SSM_STATE
SSM_CONV_CH = SSM_INNER + 2 * SSM_BC_WIDTH
SSM_CONV = 5
SSM_CHUNK = 128
POOL_WINDOWS = (2, 4, 8, 16)
POOL_GROUP_DIM = 256
DIFF_HEADS = 8
DIFF_HEAD_DIM = 64
DIFF_V_DIM = 128
ROT_DIM = 16
ROPE_THETA = 500000.0
CROSS_HEADS = 4
CROSS_HEAD_DIM = 256


def _cparams(*semantics):
    return pltpu.CompilerParams(dimension_semantics=semantics, vmem_limit_bytes=VMEM_LIMIT_BYTES)


def _dot(a, b):
    return jnp.dot(a, b, preferred_element_type=F32)


def _dot_nt(a, b):
    return lax.dot_general(a, b, (((1,), (1,)), ((), ())), preferred_element_type=F32)


def _dot_tn(a, b):
    return lax.dot_general(a, b, (((0,), (0,)), ((), ())), preferred_element_type=F32)


def _rms(x, g):
    return x * lax.rsqrt(jnp.mean(x * x, axis=-1, keepdims=True) + EPS) * g


def _silu(x):
    return x * jax.nn.sigmoid(x)


def _row_tile(n, want):
    t = min(n, want)
    assert n % t == 0, (n, t)
    return t


FFN_ROW_TILE = 512


def _ffn_body(x_ref, g_ref, wg_ref, wu_ref, wd_ref, o_ref):
    x = x_ref[...]
    xn = _rms(x, g_ref[...]).astype(BF16)
    gate = _dot(xn, wg_ref[...])
    up = _dot(xn, wu_ref[...])
    h = (_silu(gate) * up).astype(BF16)
    o_ref[...] = x + 0.5 * _dot(h, wd_ref[...])


def _ffn(x, g, wg, wu, wd):
    t, d = x.shape
    f = wg.shape[1]
    tm = _row_tile(t, FFN_ROW_TILE)
    fixed = lambda i: (0, 0)
    return pl.pallas_call(
        _ffn_body,
        out_shape=jax.ShapeDtypeStruct((t, d), F32),
        grid=(t // tm,),
        in_specs=[
            pl.BlockSpec((tm, d), lambda i: (i, 0)),
            pl.BlockSpec((1, d), fixed),
            pl.BlockSpec((d, f), fixed),
            pl.BlockSpec((d, f), fixed),
            pl.BlockSpec((f, d), fixed),
        ],
        out_specs=pl.BlockSpec((tm, d), lambda i: (i, 0)),
        compiler_params=_cparams("parallel"),
        name="ffn",
    )(x, g.reshape(1, d), wg, wu, wd)


CONV_LANES = 512


def _ab_in_body(tiles_per_seq, xp_ref, x_ref, xq_ref, g_ref, w_ref, wdt_ref, cw_ref, cb_ref,
                uf_ref, z_ref, xbc_ref, dt_ref, xn_ref, ext_ref):
    tm = x_ref.shape[0]
    hr = F32_SUBLANE_TILE
    i = pl.program_id(0)
    keep_prev = (i % tiles_per_seq != 0).astype(F32)
    keep_next = (i % tiles_per_seq != tiles_per_seq - 1).astype(F32)
    xn_ref[0:tm] = _rms(x_ref[...], g_ref[...]).astype(BF16)
    xn_ref[tm:tm + 2 * hr] = _rms(jnp.concatenate([xp_ref[...], xq_ref[...]], axis=0), g_ref[...]).astype(BF16)
    half = SSM_CONV // 2
    col = uf_ref.shape[1] + z_ref.shape[1]
    for n, c in enumerate(range(0, SSM_CONV_CH, CONV_LANES)):
        cols = slice(c, c + CONV_LANES)
        pre = _dot(xn_ref[...], w_ref[:, col + c:col + c + CONV_LANES])
        ext = ext_ref.at[n % ext_ref.shape[0]]
        ext[0:hr] = pre[tm:tm + hr] * keep_prev
        ext[hr:hr + tm] = pre[0:tm]
        ext[hr + tm:2 * hr + tm] = pre[tm + hr:tm + 2 * hr] * keep_next
        acc = cb_ref[:, cols] + ext[hr - half:hr - half + tm] * cw_ref[0:1, cols]
        for j in range(1, SSM_CONV):
            acc = acc + ext[hr - half + j:hr - half + j + tm] * cw_ref[j:j + 1, cols]
        xbc_ref[:, cols] = _silu(acc).astype(xbc_ref.dtype)
    dt_ref[...] = _dot(xn_ref[0:tm], wdt_ref[...])
    col = 0
    for ref in (uf_ref, z_ref):
        width = ref.shape[1]
        for c in range(0, width, 1024):
            ref[:, c:c + 1024] = _dot(xn_ref[0:tm], w_ref[:, col + c:col + c + 1024]).astype(ref.dtype)
        col += width


def _ab_in(x, g, w_main, w_dt, conv_w, conv_b, length):
    t, d = x.shape
    tm = _row_tile(length, 512)
    hr = F32_SUBLANE_TILE
    per = tm // hr
    last = t // hr - 1
    n_main = w_main.shape[1]
    row = lambda i: (i, 0)
    fixed = lambda i: (0, 0)
    return pl.pallas_call(
        functools.partial(_ab_in_body, length // tm),
        out_shape=(
            jax.ShapeDtypeStruct((t, D_MODEL), BF16),
            jax.ShapeDtypeStruct((t, SSM_INNER), BF16),
            jax.ShapeDtypeStruct((t, SSM_CONV_CH), BF16),
            jax.ShapeDtypeStruct((t, 128), F32),
        ),
        grid=(t // tm,),
        in_specs=[
            pl.BlockSpec((hr, d), lambda i: (jnp.maximum(i * per - 1, 0), 0)),
            pl.BlockSpec((tm, d), row),
            pl.BlockSpec((hr, d), lambda i: (jnp.minimum((i + 1) * per, last), 0)),
            pl.BlockSpec((1, d), fixed),
            pl.BlockSpec((d, n_main), fixed),
            pl.BlockSpec((d, 128), fixed),
            pl.BlockSpec((SSM_CONV, SSM_CONV_CH), fixed),
            pl.BlockSpec((1, SSM_CONV_CH), fixed),
        ],
        out_specs=(
            pl.BlockSpec((tm, D_MODEL), row),
            pl.BlockSpec((tm, SSM_INNER), row),
            pl.BlockSpec((tm, SSM_CONV_CH), row),
            pl.BlockSpec((tm, 128), row),
        ),
        scratch_shapes=[pltpu.VMEM((tm + 2 * hr, d), BF16), pltpu.VMEM((2, tm + 2 * hr, CONV_LANES), F32)],
        compiler_params=_cparams("parallel"),
        name="ab_in",
    )(x, x, x, g.reshape(1, d), w_main, w_dt, conv_w, conv_b.reshape(1, SSM_CONV_CH))


def _dft_tables(length):
    l2 = FFT_INNER
    l1 = length // l2
    assert l1 * l2 == length
    k1 = np.arange(l1)
    ang1 = 2.0 * np.pi * ((k1[:, None] * k1[None, :]) % l1) / l1
    f1 = np.concatenate([np.cos(ang1), -np.sin(ang1)], axis=0)
    k2 = np.arange(l2)
    n2 = np.arange(l2)
    kk = k1[:, None, None] + l1 * k2[None, :, None]
    ang2 = 2.0 * np.pi * ((kk * n2[None, None, :]) % length) / length
    mr, mi = np.cos(ang2), -np.sin(ang2)
    m2 = np.concatenate([np.concatenate([mr, -mi], axis=2),
                         np.concatenate([mi, mr], axis=2)], axis=1)
    c = np.arange(FNET_HEAD_DIM)
    angc = 2.0 * np.pi * ((c[:, None] * c[None, :]) % FNET_HEAD_DIM) / FNET_HEAD_DIM
    fc = np.concatenate([np.cos(angc), np.sin(angc)], axis=0)
    return (jnp.asarray(f1, dtype=BF16), jnp.asarray(m2, dtype=BF16), jnp.asarray(fc, dtype=BF16))


def _fft1_body(f_ref, x_ref, o_ref):
    o_ref[0] = _dot(f_ref[...], x_ref[0]).astype(o_ref.dtype)


FFT_K1_PER_STEP = 8


def _fft2_body(scale, m_ref, fc_ref, t_ref, o_ref):
    nk, l2, c = t_ref.shape[2:]
    fc = fc_ref[...]
    for j in range(nk):
        t = jnp.concatenate([t_ref[0, 0, j], t_ref[0, 1, j]], axis=0)
        y = _dot(m_ref[j], t)
        yr, yi = y[:l2].astype(BF16), y[l2:].astype(BF16)
        outs = []
        for h in range(FNET_HEADS):
            sl = slice(h * FNET_HEAD_DIM, (h + 1) * FNET_HEAD_DIM)
            outs.append(_dot(jnp.concatenate([yr[:, sl], yi[:, sl]], axis=1), fc))
        o_ref[0, :, j * c:(j + 1) * c] = (jnp.concatenate(outs, axis=1) * scale).astype(o_ref.dtype)


def _fourier(uf, nb, length):
    c = D_MODEL
    l2 = FFT_INNER
    l1 = length // l2
    f1, m2, fc = _dft_tables(length)
    x1 = uf.reshape(nb, l1, l2 * c)
    tcol = min(l2 * c, 8192)
    t = pl.pallas_call(
        _fft1_body,
        out_shape=jax.ShapeDtypeStruct((nb, 2 * l1, l2 * c), BF16),
        grid=(nb, (l2 * c) // tcol),
        in_specs=[pl.BlockSpec((2 * l1, l1), lambda b, j: (0, 0)),
                  pl.BlockSpec((1, l1, tcol), lambda b, j: (b, 0, j))],
        out_specs=pl.BlockSpec((1, 2 * l1, tcol), lambda b, j: (b, 0, j)),
        compiler_params=_cparams("parallel", "parallel"),
        name="fft_stage1",
    )(f1, x1)
    t5 = t.reshape(nb, 2, l1, l2, c)
    nk = math.gcd(l1, FFT_K1_PER_STEP)
    scale = 1.0 / math.sqrt(length * FNET_HEAD_DIM)
    y = pl.pallas_call(
        functools.partial(_fft2_body, scale),
        out_shape=jax.ShapeDtypeStruct((nb, l2, l1 * c), BF16),
        grid=(nb, l1 // nk),
        in_specs=[pl.BlockSpec((nk, 2 * l2, 2 * l2), lambda b, k: (k, 0, 0)),
                  pl.BlockSpec((2 * FNET_HEAD_DIM, FNET_HEAD_DIM), lambda b, k: (0, 0)),
                  pl.BlockSpec((1, 2, nk, l2, c), lambda b, k: (b, 0, k, 0, 0))],
        out_specs=pl.BlockSpec((1, l2, nk * c), lambda b, k: (b, 0, k)),
        compiler_params=_cparams("parallel", "parallel"),
        name="fft_stage2",
    )(m2, fc, t5)
    return y.reshape(nb * length, c)


def _shift_rows(x, k):
    n = x.shape[0]
    return x if k % n == 0 else pltpu.roll(x, (-k) % n, 0)


def _split_bf16(v, pieces):
    out = []
    for _ in range(pieces):
        p = v.astype(BF16)
        out.append(p)
        v = v - p.astype(F32)
    return out


def _ssd_chunk(reverse, x_ref, b_ref, c_ref, dt_ref, dtb_ref, alog_ref, expand_ref, state_ref):
    q = SSM_CHUNK
    n = SSM_STATE
    assert q == 128 and n == 128
    ch0 = SSM_HEADS * (1 if reverse else 0)
    x_b = x_ref[0]
    raw = dt_ref[0] + dtb_ref[...]
    e = jnp.exp(-jnp.abs(raw))
    u = 1.0 + e
    um1 = u - 1.0
    dt = jnp.maximum(raw, 0.0) + jnp.where(um1 == 0.0, e, jnp.log(u) * (e / jnp.where(um1 == 0.0, 1.0, um1)))
    da = dt * (-LOG2E * jnp.exp(alog_ref[...]))
    row = lax.broadcasted_iota(jnp.int32, (q, q), 0)
    col = lax.broadcasted_iota(jnp.int32, (q, q), 1)
    mask = (col >= row) if reverse else (col <= row)
    tri = jnp.where(mask, 1.0, 0.0).astype(BF16)
    acum = sum(_dot(tri, p) for p in _split_bf16(da, 3))
    src_t = (acum - jnp.log2(dt)).T
    total = acum[0:1] if reverse else acum[q - 1:q]
    expand = expand_ref[...]
    step_w = _dot((dt * jnp.exp2(total - acum)).astype(BF16), expand)
    xw = (x_b.astype(F32) * step_w).astype(BF16)
    e_total = jnp.exp2(jnp.broadcast_to(total, (8, 128)))
    e_total = sum(_dot(p, expand) for p in _split_bf16(e_total, 3))[0:1]

    first_head = lax.broadcasted_iota(jnp.int32, (q, 2 * SSM_HEAD_DIM), 1) < SSM_HEAD_DIM
    ys = []
    for g in range(SSM_GROUPS):
        bg = b_ref[0, :, g * n:(g + 1) * n]
        cg = c_ref[0, :, g * n:(g + 1) * n]
        scores = _dot_nt(cg, bg)
        cg_f = cg.astype(F32)
        state = state_ref[g]
        state_b = state.astype(BF16)
        pairs = []
        for j in range(SSM_HEADS_PER_GROUP // 2):
            lhs = []
            for h in (2 * j, 2 * j + 1):
                ch = ch0 + g * SSM_HEADS_PER_GROUP + h
                a_l = jnp.broadcast_to(acum[:, ch:ch + 1], (q, q))
                decay_dt = jnp.exp2(jnp.where(mask, a_l - src_t[ch:ch + 1, :], -jnp.inf))
                s_h = (scores * decay_dt).astype(BF16)
                c_h = (cg_f * jnp.exp2(a_l)).astype(BF16)
                lhs.append(jnp.concatenate([s_h, c_h], axis=1))
            lanes = slice(g * SSM_GROUP_WIDTH + 128 * j, g * SSM_GROUP_WIDTH + 128 * (j + 1))
            rhs = jnp.concatenate([x_b[:, lanes], state_b[:, 128 * j:128 * (j + 1)]], axis=0)
            out = _dot(jnp.concatenate(lhs, axis=0), rhs)
            pairs.append(jnp.where(first_head, out[:q], out[q:]))
        ys.append(jnp.concatenate(pairs, axis=1))
        gcols = slice(g * SSM_GROUP_WIDTH, (g + 1) * SSM_GROUP_WIDTH)
        state_ref[g] = state * e_total[:, gcols] + _dot_tn(bg, xw[:, gcols])
    return jnp.concatenate(ys, axis=1)


def _ssd_body(xf_ref, bf_ref, cf_ref, dtf_ref, xr_ref, br_ref, cr_ref, dtr_ref, dtb_ref, alog_ref,
              ef_ref, er_ref, dskip_ref, yf_ref, yr_ref, sf_ref, sr_ref):
    @pl.when(pl.program_id(1) == 0)
    def _():
        sf_ref[...] = jnp.zeros_like(sf_ref)
        sr_ref[...] = jnp.zeros_like(sr_ref)

    yf = _ssd_chunk(False, xf_ref, bf_ref, cf_ref, dtf_ref, dtb_ref, alog_ref, ef_ref, sf_ref)
    yr = _ssd_chunk(True, xr_ref, br_ref, cr_ref, dtr_ref, dtb_ref, alog_ref, er_ref, sr_ref)
    yf_ref[0] = yf.astype(yf_ref.dtype)
    yr_ref[0] = (yr + dskip_ref[...] * xr_ref[0].astype(F32)).astype(yr_ref.dtype)


def _ssd(xbc, dt, dt_bias, a_log, d_skip):
    nb, length, _ = xbc.shape
    q = SSM_CHUNK
    nc = length // q
    fixed = lambda b, c: (0, 0)
    state = pltpu.VMEM((SSM_GROUPS, SSM_STATE, SSM_GROUP_WIDTH), F32)
    pad = 128 - 2 * SSM_HEADS
    dtb = jnp.pad(dt_bias.reshape(1, 2 * SSM_HEADS), ((0, 0), (0, pad)))
    alog = jnp.pad(a_log.reshape(1, 2 * SSM_HEADS), ((0, 0), (0, pad)))

    def head_expand(direction):
        e = np.zeros((128, SSM_INNER), np.float32)
        for h in range(SSM_HEADS):
            e[direction * SSM_HEADS + h, h * SSM_HEAD_DIM:(h + 1) * SSM_HEAD_DIM] = 1.0
        return jnp.asarray(e, BF16)

    def specs(chunk):
        return [
            pl.BlockSpec((1, q, SSM_INNER), lambda b, c: (b, chunk(c), 0)),
            pl.BlockSpec((1, q, SSM_BC_WIDTH), lambda b, c: (b, chunk(c), SSM_INNER // SSM_BC_WIDTH)),
            pl.BlockSpec((1, q, SSM_BC_WIDTH), lambda b, c: (b, chunk(c), SSM_INNER // SSM_BC_WIDTH + 1)),
            pl.BlockSpec((1, q, 128), lambda b, c: (b, chunk(c), 0)),
        ]

    fw = lambda c: c
    rv = lambda c: nc - 1 - c
    dskip = jnp.repeat(d_skip, SSM_HEAD_DIM).reshape(1, SSM_INNER)
    out = jax.ShapeDtypeStruct((nb, length, SSM_INNER), BF16)
    return pl.pallas_call(
        _ssd_body,
        out_shape=(out, out),
        grid=(nb, nc),
        in_specs=specs(fw) + specs(rv) + [
            pl.BlockSpec((1, 128), fixed),
            pl.BlockSpec((1, 128), fixed),
            pl.BlockSpec((128, SSM_INNER), fixed),
            pl.BlockSpec((128, SSM_INNER), fixed),
            pl.BlockSpec((1, SSM_INNER), fixed),
        ],
        out_specs=(pl.BlockSpec((1, q, SSM_INNER), lambda b, c: (b, fw(c), 0)),
                   pl.BlockSpec((1, q, SSM_INNER), lambda b, c: (b, rv(c), 0))),
        scratch_shapes=[state, state],
        compiler_params=_cparams("parallel", "arbitrary"),
        name="ssd_scan",
    )(xbc, xbc, xbc, dt, xbc, xbc, xbc, dt, dtb, alog, head_expand(0), head_expand(1), dskip)


def _ab_out_body(x_ref, yf_ref, sf_ref, sr_ref, z_ref, gn_ref, w_ref, o_ref):
    gw = SSM_GROUP_WIDTH
    acc = x_ref[...] + _dot(yf_ref[...], w_ref[0:D_MODEL, :])
    for g in range(SSM_GROUPS):
        cols = slice(g * gw, (g + 1) * gw)
        y = (sf_ref[:, cols].astype(F32) + sr_ref[:, cols].astype(F32)) * _silu(z_ref[:, cols].astype(F32))
        yn = _rms(y, gn_ref[:, cols]).astype(BF16)
        acc = acc + _dot(yn, w_ref[D_MODEL + g * gw:D_MODEL + (g + 1) * gw, :])
    o_ref[...] = acc


def _ab_out(x, y_four, y_fw, y_bw, z, gate_norm, w_out):
    t, d = x.shape
    tm = _row_tile(t, 512)
    row = lambda i: (i, 0)
    fixed = lambda i: (0, 0)
    return pl.pallas_call(
        _ab_out_body,
        out_shape=jax.ShapeDtypeStruct((t, d), F32),
        grid=(t // tm,),
        in_specs=[
            pl.BlockSpec((tm, d), row),
            pl.BlockSpec((tm, D_MODEL), row),
            pl.BlockSpec((tm, SSM_INNER), row),
            pl.BlockSpec((tm, SSM_INNER), row),
            pl.BlockSpec((tm, SSM_INNER), row),
            pl.BlockSpec((1, SSM_INNER), fixed),
            pl.BlockSpec(w_out.shape, fixed),
        ],
        out_specs=pl.BlockSpec((tm, d), row),
        compiler_params=_cparams("parallel"),
        name="ab_out",
    )(x, y_four, y_fw, y_bw, z, gate_norm.reshape(1, SSM_INNER), w_out)


def _rope_tables(length):
    inv = ROPE_THETA ** (-jnp.arange(0, ROT_DIM, 2, dtype=F32) / ROT_DIM)
    ang = jnp.arange(length, dtype=F32)[:, None] * inv[None, :]
    cos, sin = jnp.cos(ang), jnp.sin(ang)
    half = ROT_DIM // 2
    pad = DIFF_HEAD_DIM - ROT_DIM
    ones = jnp.ones((length, pad), F32)
    zeros = jnp.zeros((length, pad), F32)
    zh = jnp.zeros((length, half), F32)
    c_self = jnp.concatenate([cos, cos, ones], axis=1)
    c_up = jnp.concatenate([-sin, zh, zeros], axis=1)
    c_down = jnp.concatenate([zh, sin, zeros], axis=1)
    rep = 128 // DIFF_HEAD_DIM
    return tuple(jnp.tile(tb, (1, rep)) for tb in (c_self, c_up, c_down))


def _pooled(length, ext, w_ref, s_ref):
    hr = F32_SUBLANE_TILE
    tm = ext.shape[0] - 2 * hr
    gd = POOL_GROUP_DIM
    pos = pl.program_id(1) * tm + lax.broadcasted_iota(jnp.int32, (tm, 1), 0)
    win = ext + _shift_rows(ext, -1)
    outs = []
    for g, w in enumerate(POOL_WINDOWS):
        if g > 0:
            win = win[:, gd:]
            win = _shift_rows(win, -(w // 4)) + _shift_rows(win, w // 4)
        lo = jnp.maximum(pos - w // 2, 0)
        hi = jnp.minimum(pos + w // 2 - 1, length - 1)
        mean = win[hr:hr + tm, :gd] / (hi - lo + 1).astype(F32)
        centred = (mean - ext[hr:hr + tm, g * gd:(g + 1) * gd]).astype(BF16)
        outs.append(_dot(centred, w_ref[g]))
    return jnp.concatenate(outs, axis=1) * s_ref[...]


def _cd_in_body(length, xp_ref, x_ref, xq_ref, g_ref, w_ref, ones_ref, qg_ref, kg_ref, cs_ref, cu_ref, cd_ref,
                pw_ref, ps_ref, yp_ref, q_ref, k_ref, v_ref):
    hr = F32_SUBLANE_TILE
    i = pl.program_id(1)
    keep_prev = (i > 0).astype(F32)
    keep_next = (i < pl.num_programs(1) - 1).astype(F32)
    xn = _rms(x_ref[0], g_ref[...]).astype(BF16)
    halo = _rms(jnp.concatenate([xp_ref[0], xq_ref[0]], axis=0), g_ref[...]).astype(BF16)
    d = D_MODEL
    half = ROT_DIM // 2
    rep = d // 128
    c_self = jnp.tile(cs_ref[...], (1, rep))
    c_up = jnp.tile(cu_ref[...], (1, rep))
    c_down = jnp.tile(cd_ref[...], (1, rep))

    def qk_norm_rope(t, gain):
        sq = (t * t).astype(BF16)
        ms = jnp.concatenate([_dot(sq[:, c:c + 256], ones_ref[...]) for c in range(0, d, 256)], axis=1)
        t = t * lax.rsqrt(ms * (1.0 / DIFF_HEAD_DIM) + EPS) * gain
        return t * c_self + pltpu.roll(t, d - half, 1) * c_up + pltpu.roll(t, half, 1) * c_down

    edge = _dot(halo, w_ref[:, 0:d])
    ext = jnp.concatenate([edge[0:hr] * keep_prev, _dot(xn, w_ref[:, 0:d]), edge[hr:2 * hr] * keep_next], axis=0)
    yp_ref[0] = _pooled(length, ext, pw_ref, ps_ref).astype(BF16)
    q = qk_norm_rope(_dot(xn, w_ref[:, d:2 * d]), qg_ref[...])
    q_ref[0] = (q * (LOG2E * DIFF_HEAD_DIM ** -0.5)).astype(BF16)
    k_ref[0] = qk_norm_rope(_dot(xn, w_ref[:, 2 * d:3 * d]), kg_ref[...]).astype(BF16)
    v_ref[0] = _dot(xn, w_ref[:, 3 * d:4 * d]).astype(BF16)


def _cd_in(x, g, w_in, q_norm, k_norm, pool_w, pool_scale):
    nb, length, d = x.shape
    tm = _row_tile(length, 512)
    hr = F32_SUBLANE_TILE
    per = tm // hr
    last = length // hr - 1
    fixed = lambda b, i: (0, 0)
    tile = lambda b, i: (b, i, 0)
    pos = lambda b, i: (i, 0)
    ones_blk = jnp.asarray(np.kron(np.eye(256 // DIFF_HEAD_DIM), np.ones((DIFF_HEAD_DIM, DIFF_HEAD_DIM))), BF16)
    qg = jnp.tile(q_norm, d // DIFF_HEAD_DIM).reshape(1, d)
    kg = jnp.tile(k_norm, d // DIFF_HEAD_DIM).reshape(1, d)
    out = jax.ShapeDtypeStruct((nb, length, d), BF16)
    return pl.pallas_call(
        functools.partial(_cd_in_body, length),
        out_shape=(out, out, out, out),
        grid=(nb, length // tm),
        in_specs=[
            pl.BlockSpec((1, hr, d), lambda b, i: (b, jnp.maximum(i * per - 1, 0), 0)),
            pl.BlockSpec((1, tm, d), tile),
            pl.BlockSpec((1, hr, d), lambda b, i: (b, jnp.minimum((i + 1) * per, last), 0)),
            pl.BlockSpec((1, d), fixed),
            pl.BlockSpec(w_in.shape, fixed),
            pl.BlockSpec((256, 256), fixed),
            pl.BlockSpec((1, d), fixed),
            pl.BlockSpec((1, d), fixed),
            pl.BlockSpec((tm, 128), pos),
            pl.BlockSpec((tm, 128), pos),
            pl.BlockSpec((tm, 128), pos),
            pl.BlockSpec(pool_w.shape, lambda b, i: (0, 0, 0)),
            pl.BlockSpec((1, d), fixed),
        ],
        out_specs=tuple(pl.BlockSpec((1, tm, d), tile) for _ in range(4)),
        compiler_params=_cparams("parallel", "parallel"),
        name="cd_in",
    )(x, x, x, g.reshape(1, d), w_in, ones_blk, qg, kg, *_rope_tables(length), pool_w, pool_scale.reshape(1, d))


ATTN_Q_TILE = 2048
ATTN_KV_TILE = 2048
ATTN_ROW_BLOCK = 256
LOG2E = math.log2(math.e)


def _diff_attn_body(lambda_init, q_ref, k_ref, v_ref, lam_ref, sub_ref, o_ref, qs_ref, m_ref, acc_ref):
    kv = pl.program_id(3)
    tq = q_ref.shape[1]
    tk = k_ref.shape[1]
    vd = DIFF_V_DIM

    @pl.when(kv == 0)
    def _():
        q = q_ref[0]
        lane = lax.broadcasted_iota(jnp.int32, q.shape, 1)
        zero = jnp.zeros_like(q)
        qs_ref[0:tq] = jnp.where(lane < DIFF_HEAD_DIM, q, zero)
        qs_ref[tq:2 * tq] = jnp.where(lane >= DIFF_HEAD_DIM, q, zero)
        m_ref[...] = jnp.full_like(m_ref, -jnp.inf)
        acc_ref[...] = jnp.zeros_like(acc_ref)

    k = k_ref[0]
    v_ext = jnp.concatenate([v_ref[0], jnp.ones((tk, vd), BF16)], axis=1)
    rb = min(ATTN_ROW_BLOCK, 2 * tq)
    for r in range(0, 2 * tq, rb):
        rows = slice(r, r + rb)
        s = _dot_nt(qs_ref[rows], k)
        m_prev = m_ref[rows]
        m_next = jnp.maximum(m_prev, jnp.max(s, axis=1, keepdims=True))
        alpha = jnp.exp2(m_prev - m_next)
        p = jnp.exp2(s - jnp.concatenate([m_next] * (tk // 128), axis=1))
        acc_ref[rows] = acc_ref[rows] * jnp.concatenate([alpha, alpha], axis=1) + _dot(p.astype(BF16), v_ext)
        m_ref[rows] = m_next

    @pl.when(kv == pl.num_programs(3) - 1)
    def _():
        acc = acc_ref[...]
        o = acc[:, 0:vd] / acc[:, vd:2 * vd]
        lv = lam_ref[...]
        lam = (jnp.exp(jnp.sum(lv[0:1] * lv[1:2], axis=-1, keepdims=True))
               - jnp.exp(jnp.sum(lv[2:3] * lv[3:4], axis=-1, keepdims=True)) + lambda_init)
        diff = o[0:tq] - lam * o[tq:2 * tq]
        o_ref[0] = (_rms(diff, sub_ref[...]) * (1.0 - lambda_init)).astype(o_ref.dtype)


def _diff_attn(q, k, v, lam_vecs, sub_norm, lambda_init):
    nb, length, d = q.shape
    tq = _row_tile(length, ATTN_Q_TILE)
    tk = _row_tile(length, ATTN_KV_TILE)
    hw = 2 * DIFF_HEAD_DIM
    assert hw == 128 and DIFF_V_DIM == 128
    return pl.pallas_call(
        functools.partial(_diff_attn_body, lambda_init),
        out_shape=jax.ShapeDtypeStruct((nb, length, d), BF16),
        grid=(nb, DIFF_HEADS, length // tq, length // tk),
        in_specs=[
            pl.BlockSpec((1, tq, hw), lambda b, h, i, j: (b, i, h)),
            pl.BlockSpec((1, tk, hw), lambda b, h, i, j: (b, j, h)),
            pl.BlockSpec((1, tk, DIFF_V_DIM), lambda b, h, i, j: (b, j, h)),
            pl.BlockSpec((4, DIFF_HEAD_DIM), lambda b, h, i, j: (0, 0)),
            pl.BlockSpec((1, DIFF_V_DIM), lambda b, h, i, j: (0, 0)),
        ],
        out_specs=pl.BlockSpec((1, tq, DIFF_V_DIM), lambda b, h, i, j: (b, i, h)),
        scratch_shapes=[
            pltpu.VMEM((2 * tq, hw), BF16),
            pltpu.VMEM((2 * tq, 128), F32),
            pltpu.VMEM((2 * tq, 2 * DIFF_V_DIM), F32),
        ],
        compiler_params=_cparams("parallel", "parallel", "parallel", "arbitrary"),
        name="diff_attn",
    )(q, k, v, lam_vecs, sub_norm.reshape(1, DIFF_V_DIM))


def _cd_out_body(x_ref, yp_ref, o_ref_in, w_ref, o_ref):
    d = D_MODEL
    o_ref[...] = x_ref[...] + _dot(yp_ref[...], w_ref[0:d, :]) + _dot(o_ref_in[...], w_ref[d:2 * d, :])


def _cd_out(x, y_pool, o, w_out):
    t, d = x.shape
    tm = _row_tile(t, 512)
    row = lambda i: (i, 0)
    return pl.pallas_call(
        _cd_out_body,
        out_shape=jax.ShapeDtypeStruct((t, d), F32),
        grid=(t // tm,),
        in_specs=[pl.BlockSpec((tm, d), row), pl.BlockSpec((tm, d), row), pl.BlockSpec((tm, d), row),
                  pl.BlockSpec(w_out.shape, lambda i: (0, 0))],
        out_specs=pl.BlockSpec((tm, d), row),
        compiler_params=_cparams("parallel"),
        name="cd_out",
    )(x, y_pool, o, w_out)


def _mem_kv_body(m_ref, g_ref, w_ref, kg_ref, k_ref, v_ref):
    d = D_MODEL
    mn = _rms(m_ref[0], g_ref[...]).astype(BF16)
    k = _dot(mn, w_ref[:, 0:d])
    hd = CROSS_HEAD_DIM
    k_ref[0] = jnp.concatenate(
        [_rms(k[:, h * hd:(h + 1) * hd], kg_ref[...]) for h in range(CROSS_HEADS)], axis=1).astype(BF16)
    v_ref[0] = _dot(mn, w_ref[:, d:2 * d]).astype(BF16)


def _mem_kv(mem, g, w_kv, k_norm):
    nb, n_mem, d = mem.shape
    fixed = lambda b: (0, 0)
    out = jax.ShapeDtypeStruct((nb, n_mem, d), BF16)
    blk = pl.BlockSpec((1, n_mem, d), lambda b: (b, 0, 0))
    return pl.pallas_call(
        _mem_kv_body,
        out_shape=(out, out),
        grid=(nb,),
        in_specs=[blk, pl.BlockSpec((1, d), fixed), pl.BlockSpec(w_kv.shape, fixed),
                  pl.BlockSpec((1, CROSS_HEAD_DIM), fixed)],
        out_specs=(blk, blk),
        compiler_params=_cparams("parallel"),
        name="cross_mem_kv",
    )(mem, g.reshape(1, d), w_kv, k_norm.reshape(1, CROSS_HEAD_DIM))


def _cross_body(x_ref, g_ref, wq_ref, qg_ref, k_ref, v_ref, wo_ref, o_ref):
    hd = CROSS_HEAD_DIM
    x = x_ref[0]
    q = _dot(_rms(x, g_ref[...]).astype(BF16), wq_ref[...])
    heads = []
    for h in range(CROSS_HEADS):
        cols = slice(h * hd, (h + 1) * hd)
        qh = (_rms(q[:, cols], qg_ref[...]) * (hd ** -0.5)).astype(BF16)
        s = _dot_nt(qh, k_ref[0, :, cols])
        p = jnp.exp(s - jnp.max(s, axis=-1, keepdims=True))
        p = p / jnp.sum(p, axis=-1, keepdims=True)
        heads.append(_dot(p.astype(BF16), v_ref[0, :, cols]).astype(BF16))
    o_ref[0] = x + _dot(jnp.concatenate(heads, axis=1), wo_ref[...])


def _cross(x, g, w_q, q_norm, k, v, w_o):
    nb, length, d = x.shape
    n_mem = k.shape[1]
    tm = _row_tile(length, 512)
    fixed = lambda b, i: (0, 0)
    tile = lambda b, i: (b, i, 0)
    per_batch = lambda b, i: (b, 0, 0)
    return pl.pallas_call(
        _cross_body,
        out_shape=jax.ShapeDtypeStruct((nb, length, d), F32),
        grid=(nb, length // tm),
        in_specs=[
            pl.BlockSpec((1, tm, d), tile),
            pl.BlockSpec((1, d), fixed),
            pl.BlockSpec((d, d), fixed),
            pl.BlockSpec((1, CROSS_HEAD_DIM), fixed),
            pl.BlockSpec((1, n_mem, d), per_batch),
            pl.BlockSpec((1, n_mem, d), per_batch),
            pl.BlockSpec((d, d), fixed),
        ],
        out_specs=pl.BlockSpec((1, tm, d), tile),
        compiler_params=_cparams("parallel", "parallel"),
        name="cross_attn",
    )(x, g.reshape(1, d), w_q, q_norm.reshape(1, CROSS_HEAD_DIM), k, v, w_o)


def _lambda_init(layer_idx):
    return 0.8 - 0.6 * math.exp(-0.3 * layer_idx)


def _mixer_ab(x, p, i):
    nb, length, d = x.shape
    t = nb * length
    w_in = p['ab_w_in'][i]
    n_main = D_MODEL + SSM_INNER + SSM_CONV_CH
    w_dt = jnp.pad(w_in[:, n_main:], ((0, 0), (0, 128 - 2 * SSM_HEADS)))
    uf, z, xbc, dt = _ab_in(x.reshape(t, d), p['mix_norm_l'], w_in[:, :n_main].astype(BF16), w_dt.astype(BF16),
                            p['ab_conv_w'][i], p['ab_conv_b'][i], length)
    y_four = _fourier(uf, nb, length)
    y_fw, y_bw = _ssd(xbc.reshape(nb, length, SSM_CONV_CH), dt.reshape(nb, length, 128), p['ab_dt_bias'][i],
                      p['ab_a_log'][i], p['ab_d_skip'][i])
    out = _ab_out(x.reshape(t, d), y_four, y_fw.reshape(t, SSM_INNER), y_bw.reshape(t, SSM_INNER), z,
                  p['ab_gate_norm'][i], p['ab_w_out'][i].astype(BF16))
    return out.reshape(nb, length, d)


def _mixer_cd(x, p, i, layer_idx):
    nb, length, d = x.shape
    t = nb * length
    y_pool, q, k, v = _cd_in(x, p['mix_norm_l'], p['cd_w_in'][i].astype(BF16), p['cd_q_norm'][i], p['cd_k_norm'][i],
                             p['cd_pool_w'][i].astype(BF16), p['cd_pool_scale'][i])
    lam_vecs = jnp.stack([p['cd_lambda_q1'][i], p['cd_lambda_k1'][i], p['cd_lambda_q2'][i], p['cd_lambda_k2'][i]])
    o = _diff_attn(q, k, v, lam_vecs, p['cd_sub_norm'][i], _lambda_init(layer_idx))
    out = _cd_out(x.reshape(t, d), y_pool.reshape(t, d), o.reshape(t, d), p['cd_w_out'][i].astype(BF16))
    return out.reshape(nb, length, d)


def _trunk(x, mem, p, depth):
    nb, length, d = x.shape
    t = nb * length
    for l in range(depth):
        x = _ffn(x.reshape(t, d), p['ffn1_norm'][l], p['ffn1_w_gate'][l].astype(BF16),
                 p['ffn1_w_up'][l].astype(BF16), p['ffn1_w_down'][l].astype(BF16)).reshape(nb, length, d)
        pl_ = dict(p, mix_norm_l=p['mix_norm'][l])
        if l % 2 == 0:
            x = _mixer_ab(x, pl_, l // 2)
        else:
            x = _mixer_cd(x, pl_, l // 2, l)
        mk, mv = _mem_kv(mem, p['cross_mem_norm'][l], p['cross_w_kv'][l].astype(BF16), p['cross_k_norm'][l])
        x = _cross(x, p['cross_norm'][l], p['cross_w_q'][l].astype(BF16), p['cross_q_norm'][l], mk, mv,
                   p['cross_w_o'][l].astype(BF16))
        x = _ffn(x.reshape(t, d), p['ffn2_norm'][l], p['ffn2_w_gate'][l].astype(BF16),
                 p['ffn2_w_up'][l].astype(BF16), p['ffn2_w_down'][l].astype(BF16)).reshape(nb, length, d)
    return x


def kernel(x_prompt, x_sample, mem_prompt, mem_sample, ffn1_norm, ffn1_w_gate, ffn1_w_up, ffn1_w_down, mix_norm, ab_w_in, ab_conv_w, ab_conv_b, ab_dt_bias, ab_a_log, ab_d_skip, ab_gate_norm, ab_w_out, cd_w_in, cd_pool_w, cd_pool_scale, cd_q_norm, cd_k_norm, cd_lambda_q1, cd_lambda_k1, cd_lambda_q2, cd_lambda_k2, cd_sub_norm, cd_w_out, cross_norm, cross_mem_norm, cross_w_q, cross_w_kv, cross_q_norm, cross_k_norm, cross_w_o, ffn2_norm, ffn2_w_gate, ffn2_w_up, ffn2_w_down):
    p = {
        'ffn1_norm': ffn1_norm, 'ffn1_w_gate': ffn1_w_gate, 'ffn1_w_up': ffn1_w_up, 'ffn1_w_down': ffn1_w_down,
        'mix_norm': mix_norm,
        'ab_w_in': ab_w_in, 'ab_conv_w': ab_conv_w, 'ab_conv_b': ab_conv_b, 'ab_dt_bias': ab_dt_bias,
        'ab_a_log': ab_a_log, 'ab_d_skip': ab_d_skip, 'ab_gate_norm': ab_gate_norm, 'ab_w_out': ab_w_out,
        'cd_w_in': cd_w_in, 'cd_pool_w': cd_pool_w, 'cd_pool_scale': cd_pool_scale, 'cd_q_norm': cd_q_norm,
        'cd_k_norm': cd_k_norm, 'cd_lambda_q1': cd_lambda_q1, 'cd_lambda_k1': cd_lambda_k1,
        'cd_lambda_q2': cd_lambda_q2, 'cd_lambda_k2': cd_lambda_k2, 'cd_sub_norm': cd_sub_norm,
        'cd_w_out': cd_w_out,
        'cross_norm': cross_norm, 'cross_mem_norm': cross_mem_norm, 'cross_w_q': cross_w_q,
        'cross_w_kv': cross_w_kv, 'cross_q_norm': cross_q_norm, 'cross_k_norm': cross_k_norm,
        'cross_w_o': cross_w_o,
        'ffn2_norm': ffn2_norm, 'ffn2_w_gate': ffn2_w_gate, 'ffn2_w_up': ffn2_w_up, 'ffn2_w_down': ffn2_w_down,
    }
    depth = ffn1_norm.shape[0]
    return (_trunk(x_prompt, mem_prompt, p, depth), _trunk(x_sample, mem_sample, p, depth))
```

```python
import functools
import math

import numpy as np
import jax
import jax.numpy as jnp
from jax import lax
from jax.experimental import pallas as pl
from jax.experimental.pallas import tpu as pltpu

F32 = jnp.float32
BF16 = jnp.bfloat16
EPS = 1e-6

VMEM_LIMIT_BYTES = 56 * 1024 * 1024
F32_SUBLANE_TILE = 8

D_MODEL = 1024
FNET_HEADS = 4
FNET_HEAD_DIM = 256
FFT_INNER = 64
SSM_HEADS = 32
SSM_HEAD_DIM = 64
SSM_STATE = 128
SSM_GROUPS = 4
SSM_HEADS_PER_GROUP = SSM_HEADS // SSM_GROUPS
SSM_INNER = SSM_HEADS * SSM_HEAD_DIM
SSM_GROUP_WIDTH = SSM_INNER // SSM_GROUPS
SSM_BC_WIDTH = SSM_GROUPS * SSM_STATE
SSM_CONV_CH = SSM_INNER + 2 * SSM_BC_WIDTH
SSM_CONV = 5
SSM_CHUNK = 128
POOL_WINDOWS = (2, 4, 8, 16)
POOL_GROUP_DIM = 256
DIFF_HEADS = 8
DIFF_HEAD_DIM = 64
DIFF_V_DIM = 128
ROT_DIM = 16
ROPE_THETA = 500000.0
CROSS_HEADS = 4
CROSS_HEAD_DIM = 256


def _cparams(*semantics):
    return pltpu.CompilerParams(dimension_semantics=semantics, vmem_limit_bytes=VMEM_LIMIT_BYTES)


def _dot(a, b):
    return jnp.dot(a, b, preferred_element_type=F32)


def _dot_nt(a, b):
    return lax.dot_general(a, b, (((1,), (1,)), ((), ())), preferred_element_type=F32)


def _dot_tn(a, b):
    return lax.dot_general(a, b, (((0,), (0,)), ((), ())), preferred_element_type=F32)


def _rms(x, g):
    return x * lax.rsqrt(jnp.mean(x * x, axis=-1, keepdims=True) + EPS) * g


def _silu(x):
    return x * jax.nn.sigmoid(x)


def _row_tile(n, want):
    t = min(n, want)
    assert n % t == 0, (n, t)
    return t


FFN_ROW_TILE = 512


def _ffn_body(x_ref, g_ref, wg_ref, wu_ref, wd_ref, o_ref):
    x = x_ref[...]
    xn = _rms(x, g_ref[...]).astype(BF16)
    gate = _dot(xn, wg_ref[...])
    up = _dot(xn, wu_ref[...])
    h = (_silu(gate) * up).astype(BF16)
    o_ref[...] = x + 0.5 * _dot(h, wd_ref[...])


def _ffn(x, g, wg, wu, wd):
    t, d = x.shape
    f = wg.shape[1]
    tm = _row_tile(t, FFN_ROW_TILE)
    fixed = lambda i: (0, 0)
    return pl.pallas_call(
        _ffn_body,
        out_shape=jax.ShapeDtypeStruct((t, d), F32),
        grid=(t // tm,),
        in_specs=[
            pl.BlockSpec((tm, d), lambda i: (i, 0)),
            pl.BlockSpec((1, d), fixed),
            pl.BlockSpec((d, f), fixed),
            pl.BlockSpec((d, f), fixed),
            pl.BlockSpec((f, d), fixed),
        ],
        out_specs=pl.BlockSpec((tm, d), lambda i: (i, 0)),
        compiler_params=_cparams("parallel"),
        name="ffn",
    )(x, g.reshape(1, d), wg, wu, wd)


CONV_LANES = 512


def _ab_in_body(tiles_per_seq, xp_ref, x_ref, xq_ref, g_ref, w_ref, wdt_ref, cw_ref, cb_ref,
                uf_ref, z_ref, xbc_ref, dt_ref, xn_ref, ext_ref):
    tm = x_ref.shape[0]
    hr = F32_SUBLANE_TILE
    i = pl.program_id(0)
    keep_prev = (i % tiles_per_seq != 0).astype(F32)
    keep_next = (i % tiles_per_seq != tiles_per_seq - 1).astype(F32)
    xn_ref[0:tm] = _rms(x_ref[...], g_ref[...]).astype(BF16)
    xn_ref[tm:tm + 2 * hr] = _rms(jnp.concatenate([xp_ref[...], xq_ref[...]], axis=0), g_ref[...]).astype(BF16)
    half = SSM_CONV // 2
    col = uf_ref.shape[1] + z_ref.shape[1]
    for n, c in enumerate(range(0, SSM_CONV_CH, CONV_LANES)):
        cols = slice(c, c + CONV_LANES)
        pre = _dot(xn_ref[...], w_ref[:, col + c:col + c + CONV_LANES])
        ext = ext_ref.at[n % ext_ref.shape[0]]
        ext[0:hr] = pre[tm:tm + hr] * keep_prev
        ext[hr:hr + tm] = pre[0:tm]
        ext[hr + tm:2 * hr + tm] = pre[tm + hr:tm + 2 * hr] * keep_next
        acc = cb_ref[:, cols] + ext[hr - half:hr - half + tm] * cw_ref[0:1, cols]
        for j in range(1, SSM_CONV):
            acc = acc + ext[hr - half + j:hr - half + j + tm] * cw_ref[j:j + 1, cols]
        xbc_ref[:, cols] = _silu(acc).astype(xbc_ref.dtype)
    dt_ref[...] = _dot(xn_ref[0:tm], wdt_ref[...])
    col = 0
    for ref in (uf_ref, z_ref):
        width = ref.shape[1]
        for c in range(0, width, 1024):
            ref[:, c:c + 1024] = _dot(xn_ref[0:tm], w_ref[:, col + c:col + c + 1024]).astype(ref.dtype)
        col += width


def _ab_in(x, g, w_main, w_dt, conv_w, conv_b, length):
    t, d = x.shape
    tm = _row_tile(length, 512)
    hr = F32_SUBLANE_TILE
    per = tm // hr
    last = t // hr - 1
    n_main = w_main.shape[1]
    row = lambda i: (i, 0)
    fixed = lambda i: (0, 0)
    return pl.pallas_call(
        functools.partial(_ab_in_body, length // tm),
        out_shape=(
            jax.ShapeDtypeStruct((t, D_MODEL), BF16),
            jax.ShapeDtypeStruct((t, SSM_INNER), BF16),
            jax.ShapeDtypeStruct((t, SSM_CONV_CH), BF16),
            jax.ShapeDtypeStruct((t, 128), F32),
        ),
        grid=(t // tm,),
        in_specs=[
            pl.BlockSpec((hr, d), lambda i: (jnp.maximum(i * per - 1, 0), 0)),
            pl.BlockSpec((tm, d), row),
            pl.BlockSpec((hr, d), lambda i: (jnp.minimum((i + 1) * per, last), 0)),
            pl.BlockSpec((1, d), fixed),
            pl.BlockSpec((d, n_main), fixed),
            pl.BlockSpec((d, 128), fixed),
            pl.BlockSpec((SSM_CONV, SSM_CONV_CH), fixed),
            pl.BlockSpec((1, SSM_CONV_CH), fixed),
        ],
        out_specs=(
            pl.BlockSpec((tm, D_MODEL), row),
            pl.BlockSpec((tm, SSM_INNER), row),
            pl.BlockSpec((tm, SSM_CONV_CH), row),
            pl.BlockSpec((tm, 128), row),
        ),
        scratch_shapes=[pltpu.VMEM((tm + 2 * hr, d), BF16), pltpu.VMEM((2, tm + 2 * hr, CONV_LANES), F32)],
        compiler_params=_cparams("parallel"),
        name="ab_in",
    )(x, x, x, g.reshape(1, d), w_main, w_dt, conv_w, conv_b.reshape(1, SSM_CONV_CH))


def _dft_tables(length):
    l2 = FFT_INNER
    l1 = length // l2
    assert l1 * l2 == length
    k1 = np.arange(l1)
    ang1 = 2.0 * np.pi * ((k1[:, None] * k1[None, :]) % l1) / l1
    f1 = np.concatenate([np.cos(ang1), -np.sin(ang1)], axis=0)
    k2 = np.arange(l2)
    n2 = np.arange(l2)
    kk = k1[:, None, None] + l1 * k2[None, :, None]
    ang2 = 2.0 * np.pi * ((kk * n2[None, None, :]) % length) / length
    mr, mi = np.cos(ang2), -np.sin(ang2)
    m2 = np.concatenate([np.concatenate([mr, -mi], axis=2),
                         np.concatenate([mi, mr], axis=2)], axis=1)
    c = np.arange(FNET_HEAD_DIM)
    angc = 2.0 * np.pi * ((c[:, None] * c[None, :]) % FNET_HEAD_DIM) / FNET_HEAD_DIM
    fc = np.concatenate([np.cos(angc), np.sin(angc)], axis=0)
    return (jnp.asarray(f1, dtype=BF16), jnp.asarray(m2, dtype=BF16), jnp.asarray(fc, dtype=BF16))


def _fft1_body(f_ref, x_ref, o_ref):
    o_ref[0] = _dot(f_ref[...], x_ref[0]).astype(o_ref.dtype)


FFT_K1_PER_STEP = 8


def _fft2_body(scale, m_ref, fc_ref, t_ref, o_ref):
    nk, l2, c = t_ref.shape[2:]
    fc = fc_ref[...]
    for j in range(nk):
        t = jnp.concatenate([t_ref[0, 0, j], t_ref[0, 1, j]], axis=0)
        y = _dot(m_ref[j], t)
        yr, yi = y[:l2].astype(BF16), y[l2:].astype(BF16)
        outs = []
        for h in range(FNET_HEADS):
            sl = slice(h * FNET_HEAD_DIM, (h + 1) * FNET_HEAD_DIM)
            outs.append(_dot(jnp.concatenate([yr[:, sl], yi[:, sl]], axis=1), fc))
        o_ref[0, :, j * c:(j + 1) * c] = (jnp.concatenate(outs, axis=1) * scale).astype(o_ref.dtype)


def _fourier(uf, nb, length):
    c = D_MODEL
    l2 = FFT_INNER
    l1 = length // l2
    f1, m2, fc = _dft_tables(length)
    x1 = uf.reshape(nb, l1, l2 * c)
    tcol = min(l2 * c, 8192)
    t = pl.pallas_call(
        _fft1_body,
        out_shape=jax.ShapeDtypeStruct((nb, 2 * l1, l2 * c), BF16),
        grid=(nb, (l2 * c) // tcol),
        in_specs=[pl.BlockSpec((2 * l1, l1), lambda b, j: (0, 0)),
                  pl.BlockSpec((1, l1, tcol), lambda b, j: (b, 0, j))],
        out_specs=pl.BlockSpec((1, 2 * l1, tcol), lambda b, j: (b, 0, j)),
        compiler_params=_cparams("parallel", "parallel"),
        name="fft_stage1",
    )(f1, x1)
    t5 = t.reshape(nb, 2, l1, l2, c)
    nk = math.gcd(l1, FFT_K1_PER_STEP)
    scale = 1.0 / math.sqrt(length * FNET_HEAD_DIM)
    y = pl.pallas_call(
        functools.partial(_fft2_body, scale),
        out_shape=jax.ShapeDtypeStruct((nb, l2, l1 * c), BF16),
        grid=(nb, l1 // nk),
        in_specs=[pl.BlockSpec((nk, 2 * l2, 2 * l2), lambda b, k: (k, 0, 0)),
                  pl.BlockSpec((2 * FNET_HEAD_DIM, FNET_HEAD_DIM), lambda b, k: (0, 0)),
                  pl.BlockSpec((1, 2, nk, l2, c), lambda b, k: (b, 0, k, 0, 0))],
        out_specs=pl.BlockSpec((1, l2, nk * c), lambda b, k: (b, 0, k)),
        compiler_params=_cparams("parallel", "parallel"),
        name="fft_stage2",
    )(m2, fc, t5)
    return y.reshape(nb * length, c)


def _shift_rows(x, k):
    n = x.shape[0]
    return x if k % n == 0 else pltpu.roll(x, (-k) % n, 0)


def _split_bf16(v, pieces):
    out = []
    for _ in range(pieces):
        p = v.astype(BF16)
        out.append(p)
        v = v - p.astype(F32)
    return out


def _ssd_chunk(reverse, x_ref, b_ref, c_ref, dt_ref, dtb_ref, alog_ref, expand_ref, state_ref):
    q = SSM_CHUNK
    n = SSM_STATE
    assert q == 128 and n == 128
    ch0 = SSM_HEADS * (1 if reverse else 0)
    x_b = x_ref[0]
    raw = dt_ref[0] + dtb_ref[...]
    e = jnp.exp(-jnp.abs(raw))
    u = 1.0 + e
    um1 = u - 1.0
    dt = jnp.maximum(raw, 0.0) + jnp.where(um1 == 0.0, e, jnp.log(u) * (e / jnp.where(um1 == 0.0, 1.0, um1)))
    da = dt * (-LOG2E * jnp.exp(alog_ref[...]))
    row = lax.broadcasted_iota(jnp.int32, (q, q), 0)
    col = lax.broadcasted_iota(jnp.int32, (q, q), 1)
    mask = (col >= row) if reverse else (col <= row)
    tri = jnp.where(mask, 1.0, 0.0).astype(BF16)
    acum = sum(_dot(tri, p) for p in _split_bf16(da, 3))
    src_t = (acum - jnp.log2(dt)).T
    total = acum[0:1] if reverse else acum[q - 1:q]
    expand = expand_ref[...]
    step_w = _dot((dt * jnp.exp2(total - acum)).astype(BF16), expand)
    xw = (x_b.astype(F32) * step_w).astype(BF16)
    e_total = jnp.exp2(jnp.broadcast_to(total, (8, 128)))
    e_total = sum(_dot(p, expand) for p in _split_bf16(e_total, 3))[0:1]

    first_head = lax.broadcasted_iota(jnp.int32, (q, 2 * SSM_HEAD_DIM), 1) < SSM_HEAD_DIM
    ys = []
    for g in range(SSM_GROUPS):
        bg = b_ref[0, :, g * n:(g + 1) * n]
        cg = c_ref[0, :, g * n:(g + 1) * n]
        scores = _dot_nt(cg, bg)
        cg_f = cg.astype(F32)
        state = state_ref[g]
        state_b = state.astype(BF16)
        pairs = []
        for j in range(SSM_HEADS_PER_GROUP // 2):
            lhs = []
            for h in (2 * j, 2 * j + 1):
                ch = ch0 + g * SSM_HEADS_PER_GROUP + h
                a_l = jnp.broadcast_to(acum[:, ch:ch + 1], (q, q))
                decay_dt = jnp.exp2(jnp.where(mask, a_l - src_t[ch:ch + 1, :], -jnp.inf))
                s_h = (scores * decay_dt).astype(BF16)
                c_h = (cg_f * jnp.exp2(a_l)).astype(BF16)
                lhs.append(jnp.concatenate([s_h, c_h], axis=1))
            lanes = slice(g * SSM_GROUP_WIDTH + 128 * j, g * SSM_GROUP_WIDTH + 128 * (j + 1))
            rhs = jnp.concatenate([x_b[:, lanes], state_b[:, 128 * j:128 * (j + 1)]], axis=0)
            out = _dot(jnp.concatenate(lhs, axis=0), rhs)
            pairs.append(jnp.where(first_head, out[:q], out[q:]))
        ys.append(jnp.concatenate(pairs, axis=1))
        gcols = slice(g * SSM_GROUP_WIDTH, (g + 1) * SSM_GROUP_WIDTH)
        state_ref[g] = state * e_total[:, gcols] + _dot_tn(bg, xw[:, gcols])
    return jnp.concatenate(ys, axis=1)


def _ssd_body(xf_ref, bf_ref, cf_ref, dtf_ref, xr_ref, br_ref, cr_ref, dtr_ref, dtb_ref, alog_ref,
              ef_ref, er_ref, dskip_ref, yf_ref, yr_ref, sf_ref, sr_ref):
    @pl.when(pl.program_id(1) == 0)
    def _():
        sf_ref[...] = jnp.zeros_like(sf_ref)
        sr_ref[...] = jnp.zeros_like(sr_ref)

    yf = _ssd_chunk(False, xf_ref, bf_ref, cf_ref, dtf_ref, dtb_ref, alog_ref, ef_ref, sf_ref)
    yr = _ssd_chunk(True, xr_ref, br_ref, cr_ref, dtr_ref, dtb_ref, alog_ref, er_ref, sr_ref)
    yf_ref[0] = yf.astype(yf_ref.dtype)
    yr_ref[0] = (yr + dskip_ref[...] * xr_ref[0].astype(F32)).astype(yr_ref.dtype)


def _ssd(xbc, dt, dt_bias, a_log, d_skip):
    nb, length, _ = xbc.shape
    q = SSM_CHUNK
    nc = length // q
    fixed = lambda b, c: (0, 0)
    state = pltpu.VMEM((SSM_GROUPS, SSM_STATE, SSM_GROUP_WIDTH), F32)
    pad = 128 - 2 * SSM_HEADS
    dtb = jnp.pad(dt_bias.reshape(1, 2 * SSM_HEADS), ((0, 0), (0, pad)))
    alog = jnp.pad(a_log.reshape(1, 2 * SSM_HEADS), ((0, 0), (0, pad)))

    def head_expand(direction):
        e = np.zeros((128, SSM_INNER), np.float32)
        for h in range(SSM_HEADS):
            e[direction * SSM_HEADS + h, h * SSM_HEAD_DIM:(h + 1) * SSM_HEAD_DIM] = 1.0
        return jnp.asarray(e, BF16)

    def specs(chunk):
        return [
            pl.BlockSpec((1, q, SSM_INNER), lambda b, c: (b, chunk(c), 0)),
            pl.BlockSpec((1, q, SSM_BC_WIDTH), lambda b, c: (b, chunk(c), SSM_INNER // SSM_BC_WIDTH)),
            pl.BlockSpec((1, q, SSM_BC_WIDTH), lambda b, c: (b, chunk(c), SSM_INNER // SSM_BC_WIDTH + 1)),
            pl.BlockSpec((1, q, 128), lambda b, c: (b, chunk(c), 0)),
        ]

    fw = lambda c: c
    rv = lambda c: nc - 1 - c
    dskip = jnp.repeat(d_skip, SSM_HEAD_DIM).reshape(1, SSM_INNER)
    out = jax.ShapeDtypeStruct((nb, length, SSM_INNER), BF16)
    return pl.pallas_call(
        _ssd_body,
        out_shape=(out, out),
        grid=(nb, nc),
        in_specs=specs(fw) + specs(rv) + [
            pl.BlockSpec((1, 128), fixed),
            pl.BlockSpec((1, 128), fixed),
            pl.BlockSpec((128, SSM_INNER), fixed),
            pl.BlockSpec((128, SSM_INNER), fixed),
            pl.BlockSpec((1, SSM_INNER), fixed),
        ],
        out_specs=(pl.BlockSpec((1, q, SSM_INNER), lambda b, c: (b, fw(c), 0)),
                   pl.BlockSpec((1, q, SSM_INNER), lambda b, c: (b, rv(c), 0))),
        scratch_shapes=[state, state],
        compiler_params=_cparams("parallel", "arbitrary"),
        name="ssd_scan",
    )(xbc, xbc, xbc, dt, xbc, xbc, xbc, dt, dtb, alog, head_expand(0), head_expand(1), dskip)


def _ab_out_body(x_ref, yf_ref, sf_ref, sr_ref, z_ref, gn_ref, w_ref, o_ref):
    gw = SSM_GROUP_WIDTH
    acc = x_ref[...] + _dot(yf_ref[...], w_ref[0:D_MODEL, :])
    for g in range(SSM_GROUPS):
        cols = slice(g * gw, (g + 1) * gw)
        y = (sf_ref[:, cols].astype(F32) + sr_ref[:, cols].astype(F32)) * _silu(z_ref[:, cols].astype(F32))
        yn = _rms(y, gn_ref[:, cols]).astype(BF16)
        acc = acc + _dot(yn, w_ref[D_MODEL + g * gw:D_MODEL + (g + 1) * gw, :])
    o_ref[...] = acc


def _ab_out(x, y_four, y_fw, y_bw, z, gate_norm, w_out):
    t, d = x.shape
    tm = _row_tile(t, 512)
    row = lambda i: (i, 0)
    fixed = lambda i: (0, 0)
    return pl.pallas_call(
        _ab_out_body,
        out_shape=jax.ShapeDtypeStruct((t, d), F32),
        grid=(t // tm,),
        in_specs=[
            pl.BlockSpec((tm, d), row),
            pl.BlockSpec((tm, D_MODEL), row),
            pl.BlockSpec((tm, SSM_INNER), row),
            pl.BlockSpec((tm, SSM_INNER), row),
            pl.BlockSpec((tm, SSM_INNER), row),
            pl.BlockSpec((1, SSM_INNER), fixed),
            pl.BlockSpec(w_out.shape, fixed),
        ],
        out_specs=pl.BlockSpec((tm, d), row),
        compiler_params=_cparams("parallel"),
        name="ab_out",
    )(x, y_four, y_fw, y_bw, z, gate_norm.reshape(1, SSM_INNER), w_out)


def _rope_tables(length):
    inv = ROPE_THETA ** (-jnp.arange(0, ROT_DIM, 2, dtype=F32) / ROT_DIM)
    ang = jnp.arange(length, dtype=F32)[:, None] * inv[None, :]
    cos, sin = jnp.cos(ang), jnp.sin(ang)
    half = ROT_DIM // 2
    pad = DIFF_HEAD_DIM - ROT_DIM
    ones = jnp.ones((length, pad), F32)
    zeros = jnp.zeros((length, pad), F32)
    zh = jnp.zeros((length, half), F32)
    c_self = jnp.concatenate([cos, cos, ones], axis=1)
    c_up = jnp.concatenate([-sin, zh, zeros], axis=1)
    c_down = jnp.concatenate([zh, sin, zeros], axis=1)
    rep = 128 // DIFF_HEAD_DIM
    return tuple(jnp.tile(tb, (1, rep)) for tb in (c_self, c_up, c_down))


def _pooled(length, ext, w_ref, s_ref):
    hr = F32_SUBLANE_TILE
    tm = ext.shape[0] - 2 * hr
    gd = POOL_GROUP_DIM
    pos = pl.program_id(1) * tm + lax.broadcasted_iota(jnp.int32, (tm, 1), 0)
    win = ext + _shift_rows(ext, -1)
    outs = []
    for g, w in enumerate(POOL_WINDOWS):
        if g > 0:
            win = win[:, gd:]
            win = _shift_rows(win, -(w // 4)) + _shift_rows(win, w // 4)
        lo = jnp.maximum(pos - w // 2, 0)
        hi = jnp.minimum(pos + w // 2 - 1, length - 1)
        mean = win[hr:hr + tm, :gd] / (hi - lo + 1).astype(F32)
        centred = (mean - ext[hr:hr + tm, g * gd:(g + 1) * gd]).astype(BF16)
        outs.append(_dot(centred, w_ref[g]))
    return jnp.concatenate(outs, axis=1) * s_ref[...]


def _cd_in_body(length, xp_ref, x_ref, xq_ref, g_ref, w_ref, ones_ref, qg_ref, kg_ref, cs_ref, cu_ref, cd_ref,
                pw_ref, ps_ref, yp_ref, q_ref, k_ref, v_ref):
    hr = F32_SUBLANE_TILE
    i = pl.program_id(1)
    keep_prev = (i > 0).astype(F32)
    keep_next = (i < pl.num_programs(1) - 1).astype(F32)
    xn = _rms(x_ref[0], g_ref[...]).astype(BF16)
    halo = _rms(jnp.concatenate([xp_ref[0], xq_ref[0]], axis=0), g_ref[...]).astype(BF16)
    d = D_MODEL
    half = ROT_DIM // 2
    rep = d // 128
    c_self = jnp.tile(cs_ref[...], (1, rep))
    c_up = jnp.tile(cu_ref[...], (1, rep))
    c_down = jnp.tile(cd_ref[...], (1, rep))

    def qk_norm_rope(t, gain):
        sq = (t * t).astype(BF16)
        ms = jnp.concatenate([_dot(sq[:, c:c + 256], ones_ref[...]) for c in range(0, d, 256)], axis=1)
        t = t * lax.rsqrt(ms * (1.0 / DIFF_HEAD_DIM) + EPS) * gain
        return t * c_self + pltpu.roll(t, d - half, 1) * c_up + pltpu.roll(t, half, 1) * c_down

    edge = _dot(halo, w_ref[:, 0:d])
    ext = jnp.concatenate([edge[0:hr] * keep_prev, _dot(xn, w_ref[:, 0:d]), edge[hr:2 * hr] * keep_next], axis=0)
    yp_ref[0] = _pooled(length, ext, pw_ref, ps_ref).astype(BF16)
    q = qk_norm_rope(_dot(xn, w_ref[:, d:2 * d]), qg_ref[...])
    q_ref[0] = (q * (LOG2E * DIFF_HEAD_DIM ** -0.5)).astype(BF16)
    k_ref[0] = qk_norm_rope(_dot(xn, w_ref[:, 2 * d:3 * d]), kg_ref[...]).astype(BF16)
    v_ref[0] = _dot(xn, w_ref[:, 3 * d:4 * d]).astype(BF16)


def _cd_in(x, g, w_in, q_norm, k_norm, pool_w, pool_scale):
    nb, length, d = x.shape
    tm = _row_tile(length, 512)
    hr = F32_SUBLANE_TILE
    per = tm // hr
    last = length // hr - 1
    fixed = lambda b, i: (0, 0)
    tile = lambda b, i: (b, i, 0)
    pos = lambda b, i: (i, 0)
    ones_blk = jnp.asarray(np.kron(np.eye(256 // DIFF_HEAD_DIM), np.ones((DIFF_HEAD_DIM, DIFF_HEAD_DIM))), BF16)
    qg = jnp.tile(q_norm, d // DIFF_HEAD_DIM).reshape(1, d)
    kg = jnp.tile(k_norm, d // DIFF_HEAD_DIM).reshape(1, d)
    out = jax.ShapeDtypeStruct((nb, length, d), BF16)
    return pl.pallas_call(
        functools.partial(_cd_in_body, length),
        out_shape=(out, out, out, out),
        grid=(nb, length // tm),
        in_specs=[
            pl.BlockSpec((1, hr, d), lambda b, i: (b, jnp.maximum(i * per - 1, 0), 0)),
            pl.BlockSpec((1, tm, d), tile),
            pl.BlockSpec((1, hr, d), lambda b, i: (b, jnp.minimum((i + 1) * per, last), 0)),
            pl.BlockSpec((1, d), fixed),
            pl.BlockSpec(w_in.shape, fixed),
            pl.BlockSpec((256, 256), fixed),
            pl.BlockSpec((1, d), fixed),
            pl.BlockSpec((1, d), fixed),
            pl.BlockSpec((tm, 128), pos),
            pl.BlockSpec((tm, 128), pos),
            pl.BlockSpec((tm, 128), pos),
            pl.BlockSpec(pool_w.shape, lambda b, i: (0, 0, 0)),
            pl.BlockSpec((1, d), fixed),
        ],
        out_specs=tuple(pl.BlockSpec((1, tm, d), tile) for _ in range(4)),
        compiler_params=_cparams("parallel", "parallel"),
        name="cd_in",
    )(x, x, x, g.reshape(1, d), w_in, ones_blk, qg, kg, *_rope_tables(length), pool_w, pool_scale.reshape(1, d))


ATTN_Q_TILE = 2048
ATTN_KV_TILE = 2048
ATTN_COL_BLOCK = 1024
ATTN_KEY_BLOCK = 512
ATTN_SUM_ROWS = 16
LOG2E = math.log2(math.e)


def _diff_attn_body(lambda_init, q_ref, k_ref, v_ref, lam_ref, sub_ref, o_ref, qs_ref, m_ref, acc_ref, s_ref):
    kv = pl.program_id(3)
    tq = q_ref.shape[1]
    tk = k_ref.shape[1]
    vd = DIFF_V_DIM

    @pl.when(kv == 0)
    def _():
        q = q_ref[0]
        lane = lax.broadcasted_iota(jnp.int32, q.shape, 1)
        zero = jnp.zeros_like(q)
        qs_ref[0:tq] = jnp.where(lane < DIFF_HEAD_DIM, q, zero)
        qs_ref[tq:2 * tq] = jnp.where(lane >= DIFF_HEAD_DIM, q, zero)
        m_ref[...] = jnp.full_like(m_ref, -jnp.inf)
        acc_ref[...] = jnp.zeros_like(acc_ref)

    vt = jnp.concatenate([v_ref[0].T, jnp.ones((ATTN_SUM_ROWS, tk), BF16)], axis=0)
    cb = s_ref.shape[2]
    kb = math.gcd(tk, ATTN_KEY_BLOCK)
    blocks = [slice(c, c + cb) for c in range(0, 2 * tq, cb)]

    def scores(n):
        top = None
        for r in range(0, tk, kb):
            part = _dot_nt(k_ref[0, r:r + kb], qs_ref[blocks[n]])
            s_ref[n % 2, r:r + kb] = part
            part = jnp.max(part, axis=0, keepdims=True)
            top = part if top is None else jnp.maximum(top, part)
        return top

    top_next = scores(0)
    for n, cols in enumerate(blocks):
        top = top_next
        if n + 1 < len(blocks):
            top_next = scores(n + 1)
        m_prev = m_ref[:, cols]
        m_next = jnp.maximum(m_prev, top)
        alpha = jnp.exp2(m_prev - m_next)
        acc = acc_ref[:, cols] * alpha[0:1]
        for r in range(0, tk, kb):
            p = jnp.exp2(s_ref[n % 2, r:r + kb] - m_next[0:1]).astype(BF16)
            acc = acc + _dot(vt[:, r:r + kb], p)
        acc_ref[:, cols] = acc
        m_ref[:, cols] = m_next

    @pl.when(kv == pl.num_programs(3) - 1)
    def _():
        acc = acc_ref[...]
        o = acc[0:vd] / acc[vd:vd + 1]
        lv = lam_ref[...]
        lam = (jnp.exp(jnp.sum(lv[0:1] * lv[1:2], axis=-1, keepdims=True))
               - jnp.exp(jnp.sum(lv[2:3] * lv[3:4], axis=-1, keepdims=True)) + lambda_init)
        diff = (o[:, 0:tq] - lam * o[:, tq:2 * tq]).T
        o_ref[0] = (_rms(diff, sub_ref[...]) * (1.0 - lambda_init)).astype(o_ref.dtype)


def _diff_attn(q, k, v, lam_vecs, sub_norm, lambda_init):
    nb, length, d = q.shape
    tq = _row_tile(length, ATTN_Q_TILE)
    tk = _row_tile(length, ATTN_KV_TILE)
    hw = 2 * DIFF_HEAD_DIM
    assert hw == 128 and DIFF_V_DIM == 128
    return pl.pallas_call(
        functools.partial(_diff_attn_body, lambda_init),
        out_shape=jax.ShapeDtypeStruct((nb, length, d), BF16),
        grid=(nb, DIFF_HEADS, length // tq, length // tk),
        in_specs=[
            pl.BlockSpec((1, tq, hw), lambda b, h, i, j: (b, i, h)),
            pl.BlockSpec((1, tk, hw), lambda b, h, i, j: (b, j, h)),
            pl.BlockSpec((1, tk, DIFF_V_DIM), lambda b, h, i, j: (b, j, h)),
            pl.BlockSpec((4, DIFF_HEAD_DIM), lambda b, h, i, j: (0, 0)),
            pl.BlockSpec((1, DIFF_V_DIM), lambda b, h, i, j: (0, 0)),
        ],
        out_specs=pl.BlockSpec((1, tq, DIFF_V_DIM), lambda b, h, i, j: (b, i, h)),
        scratch_shapes=[
            pltpu.VMEM((2 * tq, hw), BF16),
            pltpu.VMEM((F32_SUBLANE_TILE, 2 * tq), F32),
            pltpu.VMEM((DIFF_V_DIM + ATTN_SUM_ROWS, 2 * tq), F32),
            pltpu.VMEM((2, tk, min(ATTN_COL_BLOCK, 2 * tq)), F32),
        ],
        compiler_params=_cparams("parallel", "parallel", "parallel", "arbitrary"),
        name="diff_attn",
    )(q, k, v, lam_vecs, sub_norm.reshape(1, DIFF_V_DIM))


def _cd_out_body(x_ref, yp_ref, o_ref_in, w_ref, o_ref):
    d = D_MODEL
    o_ref[...] = x_ref[...] + _dot(yp_ref[...], w_ref[0:d, :]) + _dot(o_ref_in[...], w_ref[d:2 * d, :])


def _cd_out(x, y_pool, o, w_out):
    t, d = x.shape
    tm = _row_tile(t, 512)
    row = lambda i: (i, 0)
    return pl.pallas_call(
        _cd_out_body,
        out_shape=jax.ShapeDtypeStruct((t, d), F32),
        grid=(t // tm,),
        in_specs=[pl.BlockSpec((tm, d), row), pl.BlockSpec((tm, d), row), pl.BlockSpec((tm, d), row),
                  pl.BlockSpec(w_out.shape, lambda i: (0, 0))],
        out_specs=pl.BlockSpec((tm, d), row),
        compiler_params=_cparams("parallel"),
        name="cd_out",
    )(x, y_pool, o, w_out)


def _mem_kv_body(m_ref, g_ref, w_ref, kg_ref, k_ref, v_ref):
    d = D_MODEL
    mn = _rms(m_ref[0], g_ref[...]).astype(BF16)
    k = _dot(mn, w_ref[:, 0:d])
    hd = CROSS_HEAD_DIM
    k_ref[0] = jnp.concatenate(
        [_rms(k[:, h * hd:(h + 1) * hd], kg_ref[...]) for h in range(CROSS_HEADS)], axis=1).astype(BF16)
    v_ref[0] = _dot(mn, w_ref[:, d:2 * d]).astype(BF16)


def _mem_kv(mem, g, w_kv, k_norm):
    nb, n_mem, d = mem.shape
    fixed = lambda b: (0, 0)
    out = jax.ShapeDtypeStruct((nb, n_mem, d), BF16)
    blk = pl.BlockSpec((1, n_mem, d), lambda b: (b, 0, 0))
    return pl.pallas_call(
        _mem_kv_body,
        out_shape=(out, out),
        grid=(nb,),
        in_specs=[blk, pl.BlockSpec((1, d), fixed), pl.BlockSpec(w_kv.shape, fixed),
                  pl.BlockSpec((1, CROSS_HEAD_DIM), fixed)],
        out_specs=(blk, blk),
        compiler_params=_cparams("parallel"),
        name="cross_mem_kv",
    )(mem, g.reshape(1, d), w_kv, k_norm.reshape(1, CROSS_HEAD_DIM))


def _cross_body(x_ref, g_ref, wq_ref, qg_ref, k_ref, v_ref, wo_ref, o_ref):
    hd = CROSS_HEAD_DIM
    x = x_ref[0]
    q = _dot(_rms(x, g_ref[...]).astype(BF16), wq_ref[...])
    heads = []
    for h in range(CROSS_HEADS):
        cols = slice(h * hd, (h + 1) * hd)
        qh = (_rms(q[:, cols], qg_ref[...]) * (hd ** -0.5)).astype(BF16)
        s = _dot_nt(qh, k_ref[0, :, cols])
        p = jnp.exp(s - jnp.max(s, axis=-1, keepdims=True))
        p = p / jnp.sum(p, axis=-1, keepdims=True)
        heads.append(_dot(p.astype(BF16), v_ref[0, :, cols]).astype(BF16))
    o_ref[0] = x + _dot(jnp.concatenate(heads, axis=1), wo_ref[...])


def _cross(x, g, w_q, q_norm, k, v, w_o):
    nb, length, d = x.shape
    n_mem = k.shape[1]
    tm = _row_tile(length, 512)
    fixed = lambda b, i: (0, 0)
    tile = lambda b, i: (b, i, 0)
    per_batch = lambda b, i: (b, 0, 0)
    return pl.pallas_call(
        _cross_body,
        out_shape=jax.ShapeDtypeStruct((nb, length, d), F32),
        grid=(nb, length // tm),
        in_specs=[
            pl.BlockSpec((1, tm, d), tile),
            pl.BlockSpec((1, d), fixed),
            pl.BlockSpec((d, d), fixed),
            pl.BlockSpec((1, CROSS_HEAD_DIM), fixed),
            pl.BlockSpec((1, n_mem, d), per_batch),
            pl.BlockSpec((1, n_mem, d), per_batch),
            pl.BlockSpec((d, d), fixed),
        ],
        out_specs=pl.BlockSpec((1, tm, d), tile),
        compiler_params=_cparams("parallel", "parallel"),
        name="cross_attn",
    )(x, g.reshape(1, d), w_q, q_norm.reshape(1, CROSS_HEAD_DIM), k, v, w_o)


def _lambda_init(layer_idx):
    return 0.8 - 0.6 * math.exp(-0.3 * layer_idx)


def _mixer_ab(x, p, i):
    nb, length, d = x.shape
    t = nb * length
    w_in = p['ab_w_in'][i]
    n_main = D_MODEL + SSM_INNER + SSM_CONV_CH
    w_dt = jnp.pad(w_in[:, n_main:], ((0, 0), (0, 128 - 2 * SSM_HEADS)))
    uf, z, xbc, dt = _ab_in(x.reshape(t, d), p['mix_norm_l'], w_in[:, :n_main].astype(BF16), w_dt.astype(BF16),
                            p['ab_conv_w'][i], p['ab_conv_b'][i], length)
    y_four = _fourier(uf, nb, length)
    y_fw, y_bw = _ssd(xbc.reshape(nb, length, SSM_CONV_CH), dt.reshape(nb, length, 128), p['ab_dt_bias'][i],
                      p['ab_a_log'][i], p['ab_d_skip'][i])
    out = _ab_out(x.reshape(t, d), y_four, y_fw.reshape(t, SSM_INNER), y_bw.reshape(t, SSM_INNER), z,
                  p['ab_gate_norm'][i], p['ab_w_out'][i].astype(BF16))
    return out.reshape(nb, length, d)


def _mixer_cd(x, p, i, layer_idx):
    nb, length, d = x.shape
    t = nb * length
    y_pool, q, k, v = _cd_in(x, p['mix_norm_l'], p['cd_w_in'][i].astype(BF16), p['cd_q_norm'][i], p['cd_k_norm'][i],
                             p['cd_pool_w'][i].astype(BF16), p['cd_pool_scale'][i])
    lam_vecs = jnp.stack([p['cd_lambda_q1'][i], p['cd_lambda_k1'][i], p['cd_lambda_q2'][i], p['cd_lambda_k2'][i]])
    o = _diff_attn(q, k, v, lam_vecs, p['cd_sub_norm'][i], _lambda_init(layer_idx))
    out = _cd_out(x.reshape(t, d), y_pool.reshape(t, d), o.reshape(t, d), p['cd_w_out'][i].astype(BF16))
    return out.reshape(nb, length, d)


def _trunk(x, mem, p, depth):
    nb, length, d = x.shape
    t = nb * length
    for l in range(depth):
        x = _ffn(x.reshape(t, d), p['ffn1_norm'][l], p['ffn1_w_gate'][l].astype(BF16),
                 p['ffn1_w_up'][l].astype(BF16), p['ffn1_w_down'][l].astype(BF16)).reshape(nb, length, d)
        pl_ = dict(p, mix_norm_l=p['mix_norm'][l])
        if l % 2 == 0:
            x = _mixer_ab(x, pl_, l // 2)
        else:
            x = _mixer_cd(x, pl_, l // 2, l)
        mk, mv = _mem_kv(mem, p['cross_mem_norm'][l], p['cross_w_kv'][l].astype(BF16), p['cross_k_norm'][l])
        x = _cross(x, p['cross_norm'][l], p['cross_w_q'][l].astype(BF16), p['cross_q_norm'][l], mk, mv,
                   p['cross_w_o'][l].astype(BF16))
        x = _ffn(x.reshape(t, d), p['ffn2_norm'][l], p['ffn2_w_gate'][l].astype(BF16),
                 p['ffn2_w_up'][l].astype(BF16), p['ffn2_w_down'][l].astype(BF16)).reshape(nb, length, d)
    return x


def kernel(x_prompt, x_sample, mem_prompt, mem_sample, ffn1_norm, ffn1_w_gate, ffn1_w_up, ffn1_w_down, mix_norm, ab_w_in, ab_conv_w, ab_conv_b, ab_dt_bias, ab_a_log, ab_d_skip, ab_gate_norm, ab_w_out, cd_w_in, cd_pool_w, cd_pool_scale, cd_q_norm, cd_k_norm, cd_lambda_q1, cd_lambda_k1, cd_lambda_q2, cd_lambda_k2, cd_sub_norm, cd_w_out, cross_norm, cross_mem_norm, cross_w_q, cross_w_kv, cross_q_norm, cross_k_norm, cross_w_o, ffn2_norm, ffn2_w_gate, ffn2_w_up, ffn2_w_down):
    p = {
        'ffn1_norm': ffn1_norm, 'ffn1_w_gate': ffn1_w_gate, 'ffn1_w_up': ffn1_w_up, 'ffn1_w_down': ffn1_w_down,
        'mix_norm': mix_norm,
        'ab_w_in': ab_w_in, 'ab_conv_w': ab_conv_w, 'ab_conv_b': ab_conv_b, 'ab_dt_bias': ab_dt_bias,
        'ab_a_log': ab_a_log, 'ab_d_skip': ab_d_skip, 'ab_gate_norm': ab_gate_norm, 'ab_w_out': ab_w_out,
        'cd_w_in': cd_w_in, 'cd_pool_w': cd_pool_w, 'cd_pool_scale': cd_pool_scale, 'cd_q_norm': cd_q_norm,
        'cd_k_norm': cd_k_norm, 'cd_lambda_q1': cd_lambda_q1, 'cd_lambda_k1': cd_lambda_k1,
        'cd_lambda_q2': cd_lambda_q2, 'cd_lambda_k2': cd_lambda_k2, 'cd_sub_norm': cd_sub_norm,
        'cd_w_out': cd_w_out,
        'cross_norm': cross_norm, 'cross_mem_norm': cross_mem_norm, 'cross_w_q': cross_w_q,
        'cross_w_kv': cross_w_kv, 'cross_q_norm': cross_q_norm, 'cross_k_norm': cross_k_norm,
        'cross_w_o': cross_w_o,
        'ffn2_norm': ffn2_norm, 'ffn2_w_gate': ffn2_w_gate, 'ffn2_w_up': ffn2_w_up, 'ffn2_w_down': ffn2_w_down,
    }
    depth = ffn1_norm.shape[0]
    return (_trunk(x_prompt, mem_prompt, p, depth), _trunk(x_sample, mem_sample, p, depth))
```

```python
import functools
import math

import numpy as np
import jax
import jax.numpy as jnp
from jax import lax
from jax.experimental import pallas as pl
from jax.experimental.pallas import tpu as pltpu

F32 = jnp.float32
BF16 = jnp.bfloat16
EPS = 1e-6

VMEM_LIMIT_BYTES = 56 * 1024 * 1024
F32_SUBLANE_TILE = 8
BF16_SUBLANE_TILE = 16

D_MODEL = 1024
FNET_HEADS = 4
FNET_HEAD_DIM = 256
FFT_INNER = 64
SSM_HEADS = 32
SSM_HEAD_DIM = 64
SSM_STATE = 128
SSM_GROUPS = 4
SSM_HEADS_PER_GROUP = SSM_HEADS // SSM_GROUPS
SSM_INNER = SSM_HEADS * SSM_HEAD_DIM
SSM_GROUP_WIDTH = SSM_INNER // SSM_GROUPS
SSM_BC_WIDTH = SSM_GROUPS * SSM_STATE
SSM_CONV_CH = SSM_INNER + 2 * SSM_BC_WIDTH
SSM_CONV = 5
SSM_CHUNK = 128
POOL_WINDOWS = (2, 4, 8, 16)
POOL_GROUP_DIM = 256
DIFF_HEADS = 8
DIFF_HEAD_DIM = 64
DIFF_V_DIM = 128
ROT_DIM = 16
ROPE_THETA = 500000.0
CROSS_HEADS = 4
CROSS_HEAD_DIM = 256


def _cparams(*semantics):
    return pltpu.CompilerParams(dimension_semantics=semantics, vmem_limit_bytes=VMEM_LIMIT_BYTES)


def _dot(a, b):
    return jnp.dot(a, b, preferred_element_type=F32)


def _dot_nt(a, b):
    return lax.dot_general(a, b, (((1,), (1,)), ((), ())), preferred_element_type=F32)


def _dot_tn(a, b):
    return lax.dot_general(a, b, (((0,), (0,)), ((), ())), preferred_element_type=F32)


def _rms(x, g):
    return x * lax.rsqrt(jnp.mean(x * x, axis=-1, keepdims=True) + EPS) * g


def _silu(x):
    return x * jax.nn.sigmoid(x)


def _row_tile(n, want):
    t = min(n, want)
    assert n % t == 0, (n, t)
    return t


FFN_ROW_TILE = 512


def _ffn_body(x_ref, g_ref, wg_ref, wu_ref, wd_ref, o_ref):
    x = x_ref[...]
    xn = _rms(x, g_ref[...]).astype(BF16)
    gate = _dot(xn, wg_ref[...])
    up = _dot(xn, wu_ref[...])
    h = (_silu(gate) * up).astype(BF16)
    o_ref[...] = x + 0.5 * _dot(h, wd_ref[...])


def _ffn(x, g, wg, wu, wd):
    t, d = x.shape
    f = wg.shape[1]
    tm = _row_tile(t, FFN_ROW_TILE)
    fixed = lambda i: (0, 0)
    return pl.pallas_call(
        _ffn_body,
        out_shape=jax.ShapeDtypeStruct((t, d), F32),
        grid=(t // tm,),
        in_specs=[
            pl.BlockSpec((tm, d), lambda i: (i, 0)),
            pl.BlockSpec((1, d), fixed),
            pl.BlockSpec((d, f), fixed),
            pl.BlockSpec((d, f), fixed),
            pl.BlockSpec((f, d), fixed),
        ],
        out_specs=pl.BlockSpec((tm, d), lambda i: (i, 0)),
        compiler_params=_cparams("parallel"),
        name="ffn",
    )(x, g.reshape(1, d), wg, wu, wd)


CONV_LANES = 512


def _ab_in_body(tiles_per_seq, xp_ref, x_ref, xq_ref, g_ref, w_ref, wdt_ref, cw_ref, cb_ref,
                uf_ref, z_ref, xbc_ref, dt_ref, xn_ref, ext_ref):
    tm = x_ref.shape[0]
    hr = F32_SUBLANE_TILE
    i = pl.program_id(0)
    keep_prev = (i % tiles_per_seq != 0).astype(F32)
    keep_next = (i % tiles_per_seq != tiles_per_seq - 1).astype(F32)
    xn_ref[0:tm] = _rms(x_ref[...], g_ref[...]).astype(BF16)
    xn_ref[tm:tm + 2 * hr] = _rms(jnp.concatenate([xp_ref[...], xq_ref[...]], axis=0), g_ref[...]).astype(BF16)
    half = SSM_CONV // 2
    col = uf_ref.shape[1] + z_ref.shape[1]
    for n, c in enumerate(range(0, SSM_CONV_CH, CONV_LANES)):
        cols = slice(c, c + CONV_LANES)
        pre = _dot(xn_ref[...], w_ref[:, col + c:col + c + CONV_LANES])
        ext = ext_ref.at[n % ext_ref.shape[0]]
        ext[0:hr] = pre[tm:tm + hr] * keep_prev
        ext[hr:hr + tm] = pre[0:tm]
        ext[hr + tm:2 * hr + tm] = pre[tm + hr:tm + 2 * hr] * keep_next
        acc = cb_ref[:, cols] + ext[hr - half:hr - half + tm] * cw_ref[0:1, cols]
        for j in range(1, SSM_CONV):
            acc = acc + ext[hr - half + j:hr - half + j + tm] * cw_ref[j:j + 1, cols]
        xbc_ref[:, cols] = _silu(acc).astype(xbc_ref.dtype)
    dt_ref[...] = _dot(xn_ref[0:tm], wdt_ref[...])
    col = 0
    for ref in (uf_ref, z_ref):
        width = ref.shape[1]
        for c in range(0, width, 1024):
            ref[:, c:c + 1024] = _dot(xn_ref[0:tm], w_ref[:, col + c:col + c + 1024]).astype(ref.dtype)
        col += width


def _ab_in(x, g, w_main, w_dt, conv_w, conv_b, length):
    t, d = x.shape
    tm = _row_tile(length, 512)
    hr = F32_SUBLANE_TILE
    per = tm // hr
    last = t // hr - 1
    n_main = w_main.shape[1]
    row = lambda i: (i, 0)
    fixed = lambda i: (0, 0)
    return pl.pallas_call(
        functools.partial(_ab_in_body, length // tm),
        out_shape=(
            jax.ShapeDtypeStruct((t, D_MODEL), BF16),
            jax.ShapeDtypeStruct((t, SSM_INNER), BF16),
            jax.ShapeDtypeStruct((t, SSM_CONV_CH), BF16),
            jax.ShapeDtypeStruct((t, 128), F32),
        ),
        grid=(t // tm,),
        in_specs=[
            pl.BlockSpec((hr, d), lambda i: (jnp.maximum(i * per - 1, 0), 0)),
            pl.BlockSpec((tm, d), row),
            pl.BlockSpec((hr, d), lambda i: (jnp.minimum((i + 1) * per, last), 0)),
            pl.BlockSpec((1, d), fixed),
            pl.BlockSpec((d, n_main), fixed),
            pl.BlockSpec((d, 128), fixed),
            pl.BlockSpec((SSM_CONV, SSM_CONV_CH), fixed),
            pl.BlockSpec((1, SSM_CONV_CH), fixed),
        ],
        out_specs=(
            pl.BlockSpec((tm, D_MODEL), row),
            pl.BlockSpec((tm, SSM_INNER), row),
            pl.BlockSpec((tm, SSM_CONV_CH), row),
            pl.BlockSpec((tm, 128), row),
        ),
        scratch_shapes=[pltpu.VMEM((tm + 2 * hr, d), BF16), pltpu.VMEM((2, tm + 2 * hr, CONV_LANES), F32)],
        compiler_params=_cparams("parallel"),
        name="ab_in",
    )(x, x, x, g.reshape(1, d), w_main, w_dt, conv_w, conv_b.reshape(1, SSM_CONV_CH))


def _dft_tables(length):
    l2 = FFT_INNER
    l1 = length // l2
    assert l1 * l2 == length
    k1 = np.arange(l1)
    ang1 = 2.0 * np.pi * ((k1[:, None] * k1[None, :]) % l1) / l1
    f1 = np.concatenate([np.cos(ang1), -np.sin(ang1)], axis=0)
    k2 = np.arange(l2)
    n2 = np.arange(l2)
    kk = k1[:, None, None] + l1 * k2[None, :, None]
    ang2 = 2.0 * np.pi * ((kk * n2[None, None, :]) % length) / length
    mr, mi = np.cos(ang2), -np.sin(ang2)
    m2 = np.concatenate([np.concatenate([mr, -mi], axis=2),
                         np.concatenate([mi, mr], axis=2)], axis=1)
    c = np.arange(FNET_HEAD_DIM)
    angc = 2.0 * np.pi * ((c[:, None] * c[None, :]) % FNET_HEAD_DIM) / FNET_HEAD_DIM
    fc = np.concatenate([np.cos(angc), np.sin(angc)], axis=0)
    return (jnp.asarray(f1, dtype=BF16), jnp.asarray(m2, dtype=BF16), jnp.asarray(fc, dtype=BF16))


def _fft1_body(f_ref, x_ref, o_ref):
    l1, tn, c = x_ref.shape[1:]
    x = x_ref[0].astype(F32).reshape(l1, tn * c).astype(BF16)
    t = _dot(f_ref[...], x)
    o_ref[0] = t.reshape(2, l1, tn, c).astype(o_ref.dtype)


FFT_K1_PER_STEP = 16


def _fft2_body(scale, m_ref, fc_ref, t_ref, o_ref):
    nk, l2, c = t_ref.shape[2:]
    fc = fc_ref[...]
    outs = []
    for j in range(nk):
        t = jnp.concatenate([t_ref[0, 0, j], t_ref[0, 1, j]], axis=0)
        y = _dot(m_ref[j], t)
        yr, yi = y[:l2].astype(BF16), y[l2:].astype(BF16)
        for h in range(FNET_HEADS):
            sl = slice(h * FNET_HEAD_DIM, (h + 1) * FNET_HEAD_DIM)
            outs.append(_dot(jnp.concatenate([yr[:, sl], yi[:, sl]], axis=1), fc))
    out = jnp.concatenate(outs, axis=1) * scale
    o_ref[0] = out.reshape(l2, nk, c).astype(o_ref.dtype)


def _fourier(uf, nb, length):
    c = D_MODEL
    l2 = FFT_INNER
    l1 = length // l2
    f1, m2, fc = _dft_tables(length)
    x4 = uf.reshape(nb, l1, l2, c)
    tn = math.gcd(l2, BF16_SUBLANE_TILE)
    t5 = pl.pallas_call(
        _fft1_body,
        out_shape=jax.ShapeDtypeStruct((nb, 2, l1, l2, c), BF16),
        grid=(nb, l2 // tn),
        in_specs=[pl.BlockSpec((2 * l1, l1), lambda b, j: (0, 0)),
                  pl.BlockSpec((1, l1, tn, c), lambda b, j: (b, 0, j, 0))],
        out_specs=pl.BlockSpec((1, 2, l1, tn, c), lambda b, j: (b, 0, 0, j, 0)),
        compiler_params=_cparams("parallel", "parallel"),
        name="fft_stage1",
    )(f1, x4)
    nk = math.gcd(l1, FFT_K1_PER_STEP)
    scale = 1.0 / math.sqrt(length * FNET_HEAD_DIM)
    y = pl.pallas_call(
        functools.partial(_fft2_body, scale),
        out_shape=jax.ShapeDtypeStruct((nb, l2, l1, c), BF16),
        grid=(nb, l1 // nk),
        in_specs=[pl.BlockSpec((nk, 2 * l2, 2 * l2), lambda b, k: (k, 0, 0)),
                  pl.BlockSpec((2 * FNET_HEAD_DIM, FNET_HEAD_DIM), lambda b, k: (0, 0)),
                  pl.BlockSpec((1, 2, nk, l2, c), lambda b, k: (b, 0, k, 0, 0))],
        out_specs=pl.BlockSpec((1, l2, nk, c), lambda b, k: (b, 0, k, 0)),
        compiler_params=_cparams("parallel", "parallel"),
        name="fft_stage2",
    )(m2, fc, t5)
    return y.reshape(nb * length, c)


def _shift_rows(x, k):
    n = x.shape[0]
    return x if k % n == 0 else pltpu.roll(x, (-k) % n, 0)


def _split_bf16(v, pieces):
    out = []
    for _ in range(pieces):
        p = v.astype(BF16)
        out.append(p)
        v = v - p.astype(F32)
    return out


def _ssd_chunk(reverse, rows, x_ref, b_ref, c_ref, dt_ref, dtb_ref, alog_ref, expand_ref, state_ref):
    q = SSM_CHUNK
    n = SSM_STATE
    assert q == 128 and n == 128
    ch0 = SSM_HEADS * (1 if reverse else 0)
    x_b = x_ref[0, rows]
    raw = dt_ref[0, rows] + dtb_ref[...]
    e = jnp.exp(-jnp.abs(raw))
    u = 1.0 + e
    um1 = u - 1.0
    dt = jnp.maximum(raw, 0.0) + jnp.where(um1 == 0.0, e, jnp.log(u) * (e / jnp.where(um1 == 0.0, 1.0, um1)))
    da = dt * (-LOG2E * jnp.exp(alog_ref[...]))
    row = lax.broadcasted_iota(jnp.int32, (q, q), 0)
    col = lax.broadcasted_iota(jnp.int32, (q, q), 1)
    mask = (col >= row) if reverse else (col <= row)
    tri = jnp.where(mask, 1.0, 0.0).astype(BF16)
    acum = sum(_dot(tri, p) for p in _split_bf16(da, 3))
    src_t = (acum - jnp.log2(dt)).T
    total = acum[0:1] if reverse else acum[q - 1:q]
    expand = expand_ref[...]
    step_w = _dot((dt * jnp.exp2(total - acum)).astype(BF16), expand)
    xw = (x_b.astype(F32) * step_w).astype(BF16)
    e_total = jnp.exp2(jnp.broadcast_to(total, (8, 128)))
    e_total = sum(_dot(p, expand) for p in _split_bf16(e_total, 3))[0:1]

    first_head = lax.broadcasted_iota(jnp.int32, (q, 2 * SSM_HEAD_DIM), 1) < SSM_HEAD_DIM
    ys = []
    for g in range(SSM_GROUPS):
        bg = b_ref[0, rows, g * n:(g + 1) * n]
        cg = c_ref[0, rows, g * n:(g + 1) * n]
        scores = _dot_nt(cg, bg)
        cg_f = cg.astype(F32)
        state = state_ref[g]
        state_b = state.astype(BF16)
        pairs = []
        for j in range(SSM_HEADS_PER_GROUP // 2):
            lhs = []
            for h in (2 * j, 2 * j + 1):
                ch = ch0 + g * SSM_HEADS_PER_GROUP + h
                a_l = jnp.broadcast_to(acum[:, ch:ch + 1], (q, q))
                decay_dt = jnp.exp2(jnp.where(mask, a_l - src_t[ch:ch + 1, :], -jnp.inf))
                s_h = (scores * decay_dt).astype(BF16)
                c_h = (cg_f * jnp.exp2(a_l)).astype(BF16)
                lhs.append(jnp.concatenate([s_h, c_h], axis=1))
            lanes = slice(g * SSM_GROUP_WIDTH + 128 * j, g * SSM_GROUP_WIDTH + 128 * (j + 1))
            rhs = jnp.concatenate([x_b[:, lanes], state_b[:, 128 * j:128 * (j + 1)]], axis=0)
            out = _dot(jnp.concatenate(lhs, axis=0), rhs)
            pairs.append(jnp.where(first_head, out[:q], out[q:]))
        ys.append(jnp.concatenate(pairs, axis=1))
        gcols = slice(g * SSM_GROUP_WIDTH, (g + 1) * SSM_GROUP_WIDTH)
        state_ref[g] = state * e_total[:, gcols] + _dot_tn(bg, xw[:, gcols])
    return jnp.concatenate(ys, axis=1)


SSD_CHUNKS_PER_STEP = 4


def _ssd_body(xf_ref, bf_ref, cf_ref, dtf_ref, xr_ref, br_ref, cr_ref, dtr_ref, dtb_ref, alog_ref,
              ef_ref, er_ref, dskip_ref, yf_ref, yr_ref, sf_ref, sr_ref):
    @pl.when(pl.program_id(1) == 0)
    def _():
        sf_ref[...] = jnp.zeros_like(sf_ref)
        sr_ref[...] = jnp.zeros_like(sr_ref)

    q = SSM_CHUNK
    per_step = xf_ref.shape[1] // q
    for j in range(per_step):
        rows = slice(j * q, (j + 1) * q)
        yf = _ssd_chunk(False, rows, xf_ref, bf_ref, cf_ref, dtf_ref, dtb_ref, alog_ref, ef_ref, sf_ref)
        yf_ref[0, rows] = yf.astype(yf_ref.dtype)
        rows = slice((per_step - 1 - j) * q, (per_step - j) * q)
        yr = _ssd_chunk(True, rows, xr_ref, br_ref, cr_ref, dtr_ref, dtb_ref, alog_ref, er_ref, sr_ref)
        yr_ref[0, rows] = (yr + dskip_ref[...] * xr_ref[0, rows].astype(F32)).astype(yr_ref.dtype)


def _ssd(xbc, dt, dt_bias, a_log, d_skip):
    nb, length, _ = xbc.shape
    q = SSM_CHUNK * math.gcd(length // SSM_CHUNK, SSD_CHUNKS_PER_STEP)
    nc = length // q
    fixed = lambda b, c: (0, 0)
    state = pltpu.VMEM((SSM_GROUPS, SSM_STATE, SSM_GROUP_WIDTH), F32)
    pad = 128 - 2 * SSM_HEADS
    dtb = jnp.pad(dt_bias.reshape(1, 2 * SSM_HEADS), ((0, 0), (0, pad)))
    alog = jnp.pad(a_log.reshape(1, 2 * SSM_HEADS), ((0, 0), (0, pad)))

    def head_expand(direction):
        e = np.zeros((128, SSM_INNER), np.float32)
        for h in range(SSM_HEADS):
            e[direction * SSM_HEADS + h, h * SSM_HEAD_DIM:(h + 1) * SSM_HEAD_DIM] = 1.0
        return jnp.asarray(e, BF16)

    def specs(chunk):
        return [
            pl.BlockSpec((1, q, SSM_INNER), lambda b, c: (b, chunk(c), 0)),
            pl.BlockSpec((1, q, SSM_BC_WIDTH), lambda b, c: (b, chunk(c), SSM_INNER // SSM_BC_WIDTH)),
            pl.BlockSpec((1, q, SSM_BC_WIDTH), lambda b, c: (b, chunk(c), SSM_INNER // SSM_BC_WIDTH + 1)),
            pl.BlockSpec((1, q, 128), lambda b, c: (b, chunk(c), 0)),
        ]

    fw = lambda c: c
    rv = lambda c: nc - 1 - c
    dskip = jnp.repeat(d_skip, SSM_HEAD_DIM).reshape(1, SSM_INNER)
    out = jax.ShapeDtypeStruct((nb, length, SSM_INNER), BF16)
    return pl.pallas_call(
        _ssd_body,
        out_shape=(out, out),
        grid=(nb, nc),
        in_specs=specs(fw) + specs(rv) + [
            pl.BlockSpec((1, 128), fixed),
            pl.BlockSpec((1, 128), fixed),
            pl.BlockSpec((128, SSM_INNER), fixed),
            pl.BlockSpec((128, SSM_INNER), fixed),
            pl.BlockSpec((1, SSM_INNER), fixed),
        ],
        out_specs=(pl.BlockSpec((1, q, SSM_INNER), lambda b, c: (b, fw(c), 0)),
                   pl.BlockSpec((1, q, SSM_INNER), lambda b, c: (b, rv(c), 0))),
        scratch_shapes=[state, state],
        compiler_params=_cparams("parallel", "arbitrary"),
        name="ssd_scan",
    )(xbc, xbc, xbc, dt, xbc, xbc, xbc, dt, dtb, alog, head_expand(0), head_expand(1), dskip)


def _ab_out_body(x_ref, yf_ref, sf_ref, sr_ref, z_ref, gn_ref, w_ref, o_ref):
    gw = SSM_GROUP_WIDTH
    acc = x_ref[...] + _dot(yf_ref[...], w_ref[0:D_MODEL, :])
    for g in range(SSM_GROUPS):
        cols = slice(g * gw, (g + 1) * gw)
        y = (sf_ref[:, cols].astype(F32) + sr_ref[:, cols].astype(F32)) * _silu(z_ref[:, cols].astype(F32))
        yn = _rms(y, gn_ref[:, cols]).astype(BF16)
        acc = acc + _dot(yn, w_ref[D_MODEL + g * gw:D_MODEL + (g + 1) * gw, :])
    o_ref[...] = acc


def _ab_out(x, y_four, y_fw, y_bw, z, gate_norm, w_out):
    t, d = x.shape
    tm = _row_tile(t, 512)
    row = lambda i: (i, 0)
    fixed = lambda i: (0, 0)
    return pl.pallas_call(
        _ab_out_body,
        out_shape=jax.ShapeDtypeStruct((t, d), F32),
        grid=(t // tm,),
        in_specs=[
            pl.BlockSpec((tm, d), row),
            pl.BlockSpec((tm, D_MODEL), row),
            pl.BlockSpec((tm, SSM_INNER), row),
            pl.BlockSpec((tm, SSM_INNER), row),
            pl.BlockSpec((tm, SSM_INNER), row),
            pl.BlockSpec((1, SSM_INNER), fixed),
            pl.BlockSpec(w_out.shape, fixed),
        ],
        out_specs=pl.BlockSpec((tm, d), row),
        compiler_params=_cparams("parallel"),
        name="ab_out",
    )(x, y_four, y_fw, y_bw, z, gate_norm.reshape(1, SSM_INNER), w_out)


def _rope_tables(length):
    inv = ROPE_THETA ** (-jnp.arange(0, ROT_DIM, 2, dtype=F32) / ROT_DIM)
    ang = jnp.arange(length, dtype=F32)[:, None] * inv[None, :]
    cos, sin = jnp.cos(ang), jnp.sin(ang)
    half = ROT_DIM // 2
    pad = DIFF_HEAD_DIM - ROT_DIM
    ones = jnp.ones((length, pad), F32)
    zeros = jnp.zeros((length, pad), F32)
    zh = jnp.zeros((length, half), F32)
    c_self = jnp.concatenate([cos, cos, ones], axis=1)
    c_up = jnp.concatenate([-sin, zh, zeros], axis=1)
    c_down = jnp.concatenate([zh, sin, zeros], axis=1)
    rep = 128 // DIFF_HEAD_DIM
    return tuple(jnp.tile(tb, (1, rep)) for tb in (c_self, c_up, c_down))


def _pooled(length, ext, w_ref, s_ref):
    hr = F32_SUBLANE_TILE
    tm = ext.shape[0] - 2 * hr
    gd = POOL_GROUP_DIM
    pos = pl.program_id(1) * tm + lax.broadcasted_iota(jnp.int32, (tm, 1), 0)
    win = ext + _shift_rows(ext, -1)
    outs = []
    for g, w in enumerate(POOL_WINDOWS):
        if g > 0:
            win = win[:, gd:]
            win = _shift_rows(win, -(w // 4)) + _shift_rows(win, w // 4)
        lo = jnp.maximum(pos - w // 2, 0)
        hi = jnp.minimum(pos + w // 2 - 1, length - 1)
        mean = win[hr:hr + tm, :gd] / (hi - lo + 1).astype(F32)
        centred = (mean - ext[hr:hr + tm, g * gd:(g + 1) * gd]).astype(BF16)
        outs.append(_dot(centred, w_ref[g]))
    return jnp.concatenate(outs, axis=1) * s_ref[...]


def _cd_in_body(length, xp_ref, x_ref, xq_ref, g_ref, w_ref, ones_ref, qg_ref, kg_ref, cs_ref, cu_ref, cd_ref,
                pw_ref, ps_ref, yp_ref, q_ref, k_ref, v_ref):
    hr = F32_SUBLANE_TILE
    i = pl.program_id(1)
    keep_prev = (i > 0).astype(F32)
    keep_next = (i < pl.num_programs(1) - 1).astype(F32)
    xn = _rms(x_ref[0], g_ref[...]).astype(BF16)
    halo = _rms(jnp.concatenate([xp_ref[0], xq_ref[0]], axis=0), g_ref[...]).astype(BF16)
    d = D_MODEL
    half = ROT_DIM // 2
    rep = d // 128
    c_self = jnp.tile(cs_ref[...], (1, rep))
    c_up = jnp.tile(cu_ref[...], (1, rep))
    c_down = jnp.tile(cd_ref[...], (1, rep))

    def qk_norm_rope(t, gain):
        sq = (t * t).astype(BF16)
        ms = jnp.concatenate([_dot(sq[:, c:c + 256], ones_ref[...]) for c in range(0, d, 256)], axis=1)
        t = t * lax.rsqrt(ms * (1.0 / DIFF_HEAD_DIM) + EPS) * gain
        return t * c_self + pltpu.roll(t, d - half, 1) * c_up + pltpu.roll(t, half, 1) * c_down

    edge = _dot(halo, w_ref[:, 0:d])
    ext = jnp.concatenate([edge[0:hr] * keep_prev, _dot(xn, w_ref[:, 0:d]), edge[hr:2 * hr] * keep_next], axis=0)
    yp_ref[0] = _pooled(length, ext, pw_ref, ps_ref).astype(BF16)
    q = qk_norm_rope(_dot(xn, w_ref[:, d:2 * d]), qg_ref[...])
    q_ref[0] = (q * (LOG2E * DIFF_HEAD_DIM ** -0.5)).astype(BF16)
    k_ref[0] = qk_norm_rope(_dot(xn, w_ref[:, 2 * d:3 * d]), kg_ref[...]).astype(BF16)
    v_ref[0] = _dot(xn, w_ref[:, 3 * d:4 * d]).astype(BF16)


def _cd_in(x, g, w_in, q_norm, k_norm, pool_w, pool_scale):
    nb, length, d = x.shape
    tm = _row_tile(length, 512)
    hr = F32_SUBLANE_TILE
    per = tm // hr
    last = length // hr - 1
    fixed = lambda b, i: (0, 0)
    tile = lambda b, i: (b, i, 0)
    pos = lambda b, i: (i, 0)
    ones_blk = jnp.asarray(np.kron(np.eye(256 // DIFF_HEAD_DIM), np.ones((DIFF_HEAD_DIM, DIFF_HEAD_DIM))), BF16)
    qg = jnp.tile(q_norm, d // DIFF_HEAD_DIM).reshape(1, d)
    kg = jnp.tile(k_norm, d // DIFF_HEAD_DIM).reshape(1, d)
    out = jax.ShapeDtypeStruct((nb, length, d), BF16)
    return pl.pallas_call(
        functools.partial(_cd_in_body, length),
        out_shape=(out, out, out, out),
        grid=(nb, length // tm),
        in_specs=[
            pl.BlockSpec((1, hr, d), lambda b, i: (b, jnp.maximum(i * per - 1, 0), 0)),
            pl.BlockSpec((1, tm, d), tile),
            pl.BlockSpec((1, hr, d), lambda b, i: (b, jnp.minimum((i + 1) * per, last), 0)),
            pl.BlockSpec((1, d), fixed),
            pl.BlockSpec(w_in.shape, fixed),
            pl.BlockSpec((256, 256), fixed),
            pl.BlockSpec((1, d), fixed),
            pl.BlockSpec((1, d), fixed),
            pl.BlockSpec((tm, 128), pos),
            pl.BlockSpec((tm, 128), pos),
            pl.BlockSpec((tm, 128), pos),
            pl.BlockSpec(pool_w.shape, lambda b, i: (0, 0, 0)),
            pl.BlockSpec((1, d), fixed),
        ],
        out_specs=tuple(pl.BlockSpec((1, tm, d), tile) for _ in range(4)),
        compiler_params=_cparams("parallel", "parallel"),
        name="cd_in",
    )(x, x, x, g.reshape(1, d), w_in, ones_blk, qg, kg, *_rope_tables(length), pool_w, pool_scale.reshape(1, d))


ATTN_Q_TILE = 2048
ATTN_KV_TILE = 2048
ATTN_COL_BLOCK = 1024
ATTN_KEY_BLOCK = 512
ATTN_SUM_ROWS = 16
LOG2E = math.log2(math.e)


def _diff_attn_body(lambda_init, q_ref, k_ref, v_ref, lam_ref, sub_ref, o_ref, qs_ref, m_ref, acc_ref, s_ref):
    kv = pl.program_id(3)
    tq = q_ref.shape[1]
    tk = k_ref.shape[1]
    vd = DIFF_V_DIM

    @pl.when(kv == 0)
    def _():
        q = q_ref[0]
        lane = lax.broadcasted_iota(jnp.int32, q.shape, 1)
        zero = jnp.zeros_like(q)
        qs_ref[0:tq] = jnp.where(lane < DIFF_HEAD_DIM, q, zero)
        qs_ref[tq:2 * tq] = jnp.where(lane >= DIFF_HEAD_DIM, q, zero)
        m_ref[...] = jnp.full_like(m_ref, -jnp.inf)
        acc_ref[...] = jnp.zeros_like(acc_ref)

    vt = jnp.concatenate([v_ref[0].T, jnp.ones((ATTN_SUM_ROWS, tk), BF16)], axis=0)
    cb = s_ref.shape[2]
    kb = math.gcd(tk, ATTN_KEY_BLOCK)
    blocks = [slice(c, c + cb) for c in range(0, 2 * tq, cb)]

    def scores(n):
        top = None
        for r in range(0, tk, kb):
            part = _dot_nt(k_ref[0, r:r + kb], qs_ref[blocks[n]])
            s_ref[n % 2, r:r + kb] = part
            part = jnp.max(part, axis=0, keepdims=True)
            top = part if top is None else jnp.maximum(top, part)
        return top

    top_next = scores(0)
    for n, cols in enumerate(blocks):
        top = top_next
        if n + 1 < len(blocks):
            top_next = scores(n + 1)
        m_prev = m_ref[:, cols]
        m_next = jnp.maximum(m_prev, top)
        alpha = jnp.exp2(m_prev - m_next)
        acc = acc_ref[:, cols] * alpha[0:1]
        for r in range(0, tk, kb):
            p = jnp.exp2(s_ref[n % 2, r:r + kb] - m_next[0:1]).astype(BF16)
            acc = acc + _dot(vt[:, r:r + kb], p)
        acc_ref[:, cols] = acc
        m_ref[:, cols] = m_next

    @pl.when(kv == pl.num_programs(3) - 1)
    def _():
        acc = acc_ref[...]
        o = acc[0:vd] / acc[vd:vd + 1]
        lv = lam_ref[...]
        lam = (jnp.exp(jnp.sum(lv[0:1] * lv[1:2], axis=-1, keepdims=True))
               - jnp.exp(jnp.sum(lv[2:3] * lv[3:4], axis=-1, keepdims=True)) + lambda_init)
        diff = (o[:, 0:tq] - lam * o[:, tq:2 * tq]).T
        o_ref[0] = (_rms(diff, sub_ref[...]) * (1.0 - lambda_init)).astype(o_ref.dtype)


def _diff_attn(q, k, v, lam_vecs, sub_norm, lambda_init):
    nb, length, d = q.shape
    tq = _row_tile(length, ATTN_Q_TILE)
    tk = _row_tile(length, ATTN_KV_TILE)
    hw = 2 * DIFF_HEAD_DIM
    assert hw == 128 and DIFF_V_DIM == 128
    return pl.pallas_call(
        functools.partial(_diff_attn_body, lambda_init),
        out_shape=jax.ShapeDtypeStruct((nb, length, d), BF16),
        grid=(nb, DIFF_HEADS, length // tq, length // tk),
        in_specs=[
            pl.BlockSpec((1, tq, hw), lambda b, h, i, j: (b, i, h)),
            pl.BlockSpec((1, tk, hw), lambda b, h, i, j: (b, j, h)),
            pl.BlockSpec((1, tk, DIFF_V_DIM), lambda b, h, i, j: (b, j, h)),
            pl.BlockSpec((4, DIFF_HEAD_DIM), lambda b, h, i, j: (0, 0)),
            pl.BlockSpec((1, DIFF_V_DIM), lambda b, h, i, j: (0, 0)),
        ],
        out_specs=pl.BlockSpec((1, tq, DIFF_V_DIM), lambda b, h, i, j: (b, i, h)),
        scratch_shapes=[
            pltpu.VMEM((2 * tq, hw), BF16),
            pltpu.VMEM((F32_SUBLANE_TILE, 2 * tq), F32),
            pltpu.VMEM((DIFF_V_DIM + ATTN_SUM_ROWS, 2 * tq), F32),
            pltpu.VMEM((2, tk, min(ATTN_COL_BLOCK, 2 * tq)), F32),
        ],
        compiler_params=_cparams("parallel", "parallel", "parallel", "arbitrary"),
        name="diff_attn",
    )(q, k, v, lam_vecs, sub_norm.reshape(1, DIFF_V_DIM))


def _cd_out_body(x_ref, yp_ref, o_ref_in, w_ref, o_ref):
    d = D_MODEL
    o_ref[...] = x_ref[...] + _dot(yp_ref[...], w_ref[0:d, :]) + _dot(o_ref_in[...], w_ref[d:2 * d, :])


def _cd_out(x, y_pool, o, w_out):
    t, d = x.shape
    tm = _row_tile(t, 512)
    row = lambda i: (i, 0)
    return pl.pallas_call(
        _cd_out_body,
        out_shape=jax.ShapeDtypeStruct((t, d), F32),
        grid=(t // tm,),
        in_specs=[pl.BlockSpec((tm, d), row), pl.BlockSpec((tm, d), row), pl.BlockSpec((tm, d), row),
                  pl.BlockSpec(w_out.shape, lambda i: (0, 0))],
        out_specs=pl.BlockSpec((tm, d), row),
        compiler_params=_cparams("parallel"),
        name="cd_out",
    )(x, y_pool, o, w_out)


def _mem_kv_body(m_ref, g_ref, w_ref, kg_ref, k_ref, v_ref):
    d = D_MODEL
    mn = _rms(m_ref[0], g_ref[...]).astype(BF16)
    k = _dot(mn, w_ref[:, 0:d])
    hd = CROSS_HEAD_DIM
    k_ref[0] = jnp.concatenate(
        [_rms(k[:, h * hd:(h + 1) * hd], kg_ref[...]) for h in range(CROSS_HEADS)], axis=1).astype(BF16)
    v_ref[0] = _dot(mn, w_ref[:, d:2 * d]).astype(BF16)


def _mem_kv(mem, g, w_kv, k_norm):
    nb, n_mem, d = mem.shape
    fixed = lambda b: (0, 0)
    out = jax.ShapeDtypeStruct((nb, n_mem, d), BF16)
    blk = pl.BlockSpec((1, n_mem, d), lambda b: (b, 0, 0))
    return pl.pallas_call(
        _mem_kv_body,
        out_shape=(out, out),
        grid=(nb,),
        in_specs=[blk, pl.BlockSpec((1, d), fixed), pl.BlockSpec(w_kv.shape, fixed),
                  pl.BlockSpec((1, CROSS_HEAD_DIM), fixed)],
        out_specs=(blk, blk),
        compiler_params=_cparams("parallel"),
        name="cross_mem_kv",
    )(mem, g.reshape(1, d), w_kv, k_norm.reshape(1, CROSS_HEAD_DIM))


def _cross_body(x_ref, g_ref, wq_ref, qg_ref, k_ref, v_ref, wo_ref, o_ref):
    hd = CROSS_HEAD_DIM
    x = x_ref[0]
    q = _dot(_rms(x, g_ref[...]).astype(BF16), wq_ref[...])
    heads = []
    for h in range(CROSS_HEADS):
        cols = slice(h * hd, (h + 1) * hd)
        qh = (_rms(q[:, cols], qg_ref[...]) * (hd ** -0.5)).astype(BF16)
        s = _dot_nt(qh, k_ref[0, :, cols])
        p = jnp.exp(s - jnp.max(s, axis=-1, keepdims=True))
        p = p / jnp.sum(p, axis=-1, keepdims=True)
        heads.append(_dot(p.astype(BF16), v_ref[0, :, cols]).astype(BF16))
    o_ref[0] = x + _dot(jnp.concatenate(heads, axis=1), wo_ref[...])


def _cross(x, g, w_q, q_norm, k, v, w_o):
    nb, length, d = x.shape
    n_mem = k.shape[1]
    tm = _row_tile(length, 512)
    fixed = lambda b, i: (0, 0)
    tile = lambda b, i: (b, i, 0)
    per_batch = lambda b, i: (b, 0, 0)
    return pl.pallas_call(
        _cross_body,
        out_shape=jax.ShapeDtypeStruct((nb, length, d), F32),
        grid=(nb, length // tm),
        in_specs=[
            pl.BlockSpec((1, tm, d), tile),
            pl.BlockSpec((1, d), fixed),
            pl.BlockSpec((d, d), fixed),
            pl.BlockSpec((1, CROSS_HEAD_DIM), fixed),
            pl.BlockSpec((1, n_mem, d), per_batch),
            pl.BlockSpec((1, n_mem, d), per_batch),
            pl.BlockSpec((d, d), fixed),
        ],
        out_specs=pl.BlockSpec((1, tm, d), tile),
        compiler_params=_cparams("parallel", "parallel"),
        name="cross_attn",
    )(x, g.reshape(1, d), w_q, q_norm.reshape(1, CROSS_HEAD_DIM), k, v, w_o)


def _lambda_init(layer_idx):
    return 0.8 - 0.6 * math.exp(-0.3 * layer_idx)


def _mixer_ab(x, p, i):
    nb, length, d = x.shape
    t = nb * length
    w_in = p['ab_w_in'][i]
    n_main = D_MODEL + SSM_INNER + SSM_CONV_CH
    w_dt = jnp.pad(w_in[:, n_main:], ((0, 0), (0, 128 - 2 * SSM_HEADS)))
    uf, z, xbc, dt = _ab_in(x.reshape(t, d), p['mix_norm_l'], w_in[:, :n_main].astype(BF16), w_dt.astype(BF16),
                            p['ab_conv_w'][i], p['ab_conv_b'][i], length)
    y_four = _fourier(uf, nb, length)
    y_fw, y_bw = _ssd(xbc.reshape(nb, length, SSM_CONV_CH), dt.reshape(nb, length, 128), p['ab_dt_bias'][i],
                      p['ab_a_log'][i], p['ab_d_skip'][i])
    out = _ab_out(x.reshape(t, d), y_four, y_fw.reshape(t, SSM_INNER), y_bw.reshape(t, SSM_INNER), z,
                  p['ab_gate_norm'][i], p['ab_w_out'][i].astype(BF16))
    return out.reshape(nb, length, d)


def _mixer_cd(x, p, i, layer_idx):
    nb, length, d = x.shape
    t = nb * length
    y_pool, q, k, v = _cd_in(x, p['mix_norm_l'], p['cd_w_in'][i].astype(BF16), p['cd_q_norm'][i], p['cd_k_norm'][i],
                             p['cd_pool_w'][i].astype(BF16), p['cd_pool_scale'][i])
    lam_vecs = jnp.stack([p['cd_lambda_q1'][i], p['cd_lambda_k1'][i], p['cd_lambda_q2'][i], p['cd_lambda_k2'][i]])
    o = _diff_attn(q, k, v, lam_vecs, p['cd_sub_norm'][i], _lambda_init(layer_idx))
    out = _cd_out(x.reshape(t, d), y_pool.reshape(t, d), o.reshape(t, d), p['cd_w_out'][i].astype(BF16))
    return out.reshape(nb, length, d)


def _trunk(x, mem, p, depth):
    nb, length, d = x.shape
    t = nb * length
    for l in range(depth):
        x = _ffn(x.reshape(t, d), p['ffn1_norm'][l], p['ffn1_w_gate'][l].astype(BF16),
                 p['ffn1_w_up'][l].astype(BF16), p['ffn1_w_down'][l].astype(BF16)).reshape(nb, length, d)
        pl_ = dict(p, mix_norm_l=p['mix_norm'][l])
        if l % 2 == 0:
            x = _mixer_ab(x, pl_, l // 2)
        else:
            x = _mixer_cd(x, pl_, l // 2, l)
        mk, mv = _mem_kv(mem, p['cross_mem_norm'][l], p['cross_w_kv'][l].astype(BF16), p['cross_k_norm'][l])
        x = _cross(x, p['cross_norm'][l], p['cross_w_q'][l].astype(BF16), p['cross_q_norm'][l], mk, mv,
                   p['cross_w_o'][l].astype(BF16))
        x = _ffn(x.reshape(t, d), p['ffn2_norm'][l], p['ffn2_w_gate'][l].astype(BF16),
                 p['ffn2_w_up'][l].astype(BF16), p['ffn2_w_down'][l].astype(BF16)).reshape(nb, length, d)
    return x


def kernel(x_prompt, x_sample, mem_prompt, mem_sample, ffn1_norm, ffn1_w_gate, ffn1_w_up, ffn1_w_down, mix_norm, ab_w_in, ab_conv_w, ab_conv_b, ab_dt_bias, ab_a_log, ab_d_skip, ab_gate_norm, ab_w_out, cd_w_in, cd_pool_w, cd_pool_scale, cd_q_norm, cd_k_norm, cd_lambda_q1, cd_lambda_k1, cd_lambda_q2, cd_lambda_k2, cd_sub_norm, cd_w_out, cross_norm, cross_mem_norm, cross_w_q, cross_w_kv, cross_q_norm, cross_k_norm, cross_w_o, ffn2_norm, ffn2_w_gate, ffn2_w_up, ffn2_w_down):
    p = {
        'ffn1_norm': ffn1_norm, 'ffn1_w_gate': ffn1_w_gate, 'ffn1_w_up': ffn1_w_up, 'ffn1_w_down': ffn1_w_down,
        'mix_norm': mix_norm,
        'ab_w_in': ab_w_in, 'ab_conv_w': ab_conv_w, 'ab_conv_b': ab_conv_b, 'ab_dt_bias': ab_dt_bias,
        'ab_a_log': ab_a_log, 'ab_d_skip': ab_d_skip, 'ab_gate_norm': ab_gate_norm, 'ab_w_out': ab_w_out,
        'cd_w_in': cd_w_in, 'cd_pool_w': cd_pool_w, 'cd_pool_scale': cd_pool_scale, 'cd_q_norm': cd_q_norm,
        'cd_k_norm': cd_k_norm, 'cd_lambda_q1': cd_lambda_q1, 'cd_lambda_k1': cd_lambda_k1,
        'cd_lambda_q2': cd_lambda_q2, 'cd_lambda_k2': cd_lambda_k2, 'cd_sub_norm': cd_sub_norm,
        'cd_w_out': cd_w_out,
        'cross_norm': cross_norm, 'cross_mem_norm': cross_mem_norm, 'cross_w_q': cross_w_q,
        'cross_w_kv': cross_w_kv, 'cross_q_norm': cross_q_norm, 'cross_k_norm': cross_k_norm,
        'cross_w_o': cross_w_o,
        'ffn2_norm': ffn2_norm, 'ffn2_w_gate': ffn2_w_gate, 'ffn2_w_up': ffn2_w_up, 'ffn2_w_down': ffn2_w_down,
    }
    depth = ffn1_norm.shape[0]
    return (_trunk(x_prompt, mem_prompt, p, depth), _trunk(x_sample, mem_sample, p, depth))
```

```python
import functools
import math

import numpy as np
import jax
import jax.numpy as jnp
from jax import lax
from jax.experimental import pallas as pl
from jax.experimental.pallas import tpu as pltpu

F32 = jnp.float32
BF16 = jnp.bfloat16
EPS = 1e-6

VMEM_LIMIT_BYTES = 56 * 1024 * 1024
F32_SUBLANE_TILE = 8
BF16_SUBLANE_TILE = 16

D_MODEL = 1024
FNET_HEADS = 4
FNET_HEAD_DIM = 256
FFT_INNER = 64
SSM_HEADS = 32
SSM_HEAD_DIM = 64
SSM_STATE = 128
SSM_GROUPS = 4
SSM_HEADS_PER_GROUP = SSM_HEADS // SSM_GROUPS
SSM_INNER = SSM_HEADS * SSM_HEAD_DIM
SSM_GROUP_WIDTH = SSM_INNER // SSM_GROUPS
SSM_BC_WIDTH = SSM_GROUPS * SSM_STATE
SSM_CONV_CH = SSM_INNER + 2 * SSM_BC_WIDTH
SSM_CONV = 5
SSM_CHUNK = 128
POOL_WINDOWS = (2, 4, 8, 16)
POOL_GROUP_DIM = 256
DIFF_HEADS = 8
DIFF_HEAD_DIM = 64
DIFF_V_DIM = 128
ROT_DIM = 16
ROPE_THETA = 500000.0
CROSS_HEADS = 4
CROSS_HEAD_DIM = 256


def _cparams(*semantics):
    return pltpu.CompilerParams(dimension_semantics=semantics, vmem_limit_bytes=VMEM_LIMIT_BYTES)


def _dot(a, b):
    return jnp.dot(a, b, preferred_element_type=F32)


def _dot_nt(a, b):
    return lax.dot_general(a, b, (((1,), (1,)), ((), ())), preferred_element_type=F32)


def _dot_tn(a, b):
    return lax.dot_general(a, b, (((0,), (0,)), ((), ())), preferred_element_type=F32)


def _rms(x, g):
    return x * lax.rsqrt(jnp.mean(x * x, axis=-1, keepdims=True) + EPS) * g


def _silu(x):
    return x * jax.nn.sigmoid(x)


def _row_tile(n, want):
    t = min(n, want)
    assert n % t == 0, (n, t)
    return t


FFN_ROW_TILE = 512


def _ffn_body(x_ref, g_ref, wg_ref, wu_ref, wd_ref, o_ref):
    x = x_ref[...]
    xn = _rms(x, g_ref[...]).astype(BF16)
    gate = _dot(xn, wg_ref[...])
    up = _dot(xn, wu_ref[...])
    h = (_silu(gate) * up).astype(BF16)
    o_ref[...] = x + 0.5 * _dot(h, wd_ref[...])


def _ffn(x, g, wg, wu, wd):
    t, d = x.shape
    f = wg.shape[1]
    tm = _row_tile(t, FFN_ROW_TILE)
    fixed = lambda i: (0, 0)
    return pl.pallas_call(
        _ffn_body,
        out_shape=jax.ShapeDtypeStruct((t, d), F32),
        grid=(t // tm,),
        in_specs=[
            pl.BlockSpec((tm, d), lambda i: (i, 0)),
            pl.BlockSpec((1, d), fixed),
            pl.BlockSpec((d, f), fixed),
            pl.BlockSpec((d, f), fixed),
            pl.BlockSpec((f, d), fixed),
        ],
        out_specs=pl.BlockSpec((tm, d), lambda i: (i, 0)),
        compiler_params=_cparams("parallel"),
        name="ffn",
    )(x, g.reshape(1, d), wg, wu, wd)


CONV_LANES = 512


def _ab_in_body(tiles_per_seq, xp_ref, x_ref, xq_ref, g_ref, w_ref, wdt_ref, cw_ref, cb_ref,
                uf_ref, z_ref, xbc_ref, dt_ref, xn_ref, ext_ref):
    tm = x_ref.shape[0]
    hr = F32_SUBLANE_TILE
    i = pl.program_id(0)
    keep_prev = (i % tiles_per_seq != 0).astype(F32)
    keep_next = (i % tiles_per_seq != tiles_per_seq - 1).astype(F32)
    xn_ref[0:tm] = _rms(x_ref[...], g_ref[...]).astype(BF16)
    xn_ref[tm:tm + 2 * hr] = _rms(jnp.concatenate([xp_ref[...], xq_ref[...]], axis=0), g_ref[...]).astype(BF16)
    half = SSM_CONV // 2
    col = uf_ref.shape[1] + z_ref.shape[1]
    for n, c in enumerate(range(0, SSM_CONV_CH, CONV_LANES)):
        cols = slice(c, c + CONV_LANES)
        pre = _dot(xn_ref[...], w_ref[:, col + c:col + c + CONV_LANES])
        ext = ext_ref.at[n % ext_ref.shape[0]]
        ext[0:hr] = pre[tm:tm + hr] * keep_prev
        ext[hr:hr + tm] = pre[0:tm]
        ext[hr + tm:2 * hr + tm] = pre[tm + hr:tm + 2 * hr] * keep_next
        acc = cb_ref[:, cols] + ext[hr - half:hr - half + tm] * cw_ref[0:1, cols]
        for j in range(1, SSM_CONV):
            acc = acc + ext[hr - half + j:hr - half + j + tm] * cw_ref[j:j + 1, cols]
        xbc_ref[:, cols] = _silu(acc).astype(xbc_ref.dtype)
    dt_ref[...] = _dot(xn_ref[0:tm], wdt_ref[...])
    col = 0
    for ref in (uf_ref, z_ref):
        width = ref.shape[1]
        for c in range(0, width, 1024):
            ref[:, c:c + 1024] = _dot(xn_ref[0:tm], w_ref[:, col + c:col + c + 1024]).astype(ref.dtype)
        col += width


def _ab_in(x, g, w_main, w_dt, conv_w, conv_b, length):
    t, d = x.shape
    tm = _row_tile(length, 512)
    hr = F32_SUBLANE_TILE
    per = tm // hr
    last = t // hr - 1
    n_main = w_main.shape[1]
    row = lambda i: (i, 0)
    fixed = lambda i: (0, 0)
    return pl.pallas_call(
        functools.partial(_ab_in_body, length // tm),
        out_shape=(
            jax.ShapeDtypeStruct((t, D_MODEL), BF16),
            jax.ShapeDtypeStruct((t, SSM_INNER), BF16),
            jax.ShapeDtypeStruct((t, SSM_CONV_CH), BF16),
            jax.ShapeDtypeStruct((t, 128), F32),
        ),
        grid=(t // tm,),
        in_specs=[
            pl.BlockSpec((hr, d), lambda i: (jnp.maximum(i * per - 1, 0), 0)),
            pl.BlockSpec((tm, d), row),
            pl.BlockSpec((hr, d), lambda i: (jnp.minimum((i + 1) * per, last), 0)),
            pl.BlockSpec((1, d), fixed),
            pl.BlockSpec((d, n_main), fixed),
            pl.BlockSpec((d, 128), fixed),
            pl.BlockSpec((SSM_CONV, SSM_CONV_CH), fixed),
            pl.BlockSpec((1, SSM_CONV_CH), fixed),
        ],
        out_specs=(
            pl.BlockSpec((tm, D_MODEL), row),
            pl.BlockSpec((tm, SSM_INNER), row),
            pl.BlockSpec((tm, SSM_CONV_CH), row),
            pl.BlockSpec((tm, 128), row),
        ),
        scratch_shapes=[pltpu.VMEM((tm + 2 * hr, d), BF16), pltpu.VMEM((2, tm + 2 * hr, CONV_LANES), F32)],
        compiler_params=_cparams("parallel"),
        name="ab_in",
    )(x, x, x, g.reshape(1, d), w_main, w_dt, conv_w, conv_b.reshape(1, SSM_CONV_CH))


def _dft_tables(length):
    l2 = FFT_INNER
    l1 = length // l2
    assert l1 * l2 == length
    k1 = np.arange(l1)
    ang1 = 2.0 * np.pi * ((k1[:, None] * k1[None, :]) % l1) / l1
    f1 = np.concatenate([np.cos(ang1), -np.sin(ang1)], axis=0)
    k2 = np.arange(l2)
    n2 = np.arange(l2)
    kk = k1[:, None, None] + l1 * k2[None, :, None]
    ang2 = 2.0 * np.pi * ((kk * n2[None, None, :]) % length) / length
    mr, mi = np.cos(ang2), -np.sin(ang2)
    m2 = np.concatenate([np.concatenate([mr, -mi], axis=2),
                         np.concatenate([mi, mr], axis=2)], axis=1)
    c = np.arange(FNET_HEAD_DIM)
    angc = 2.0 * np.pi * ((c[:, None] * c[None, :]) % FNET_HEAD_DIM) / FNET_HEAD_DIM
    fc = np.concatenate([np.cos(angc), np.sin(angc)], axis=0)
    return (jnp.asarray(f1, dtype=BF16), jnp.asarray(m2, dtype=BF16), jnp.asarray(fc, dtype=BF16))


def _fft1_body(f_ref, x_ref, o_ref):
    l1, tn, c = x_ref.shape[1:]
    x = x_ref[0].astype(F32).reshape(l1, tn * c).astype(BF16)
    t = _dot(f_ref[...], x)
    o_ref[0] = t.reshape(2, l1, tn, c).astype(o_ref.dtype)


FFT_K1_PER_STEP = 16


def _fft2_body(scale, m_ref, fc_ref, t_ref, o_ref):
    nk, l2, c = t_ref.shape[2:]
    fc = fc_ref[...]
    outs = []
    for j in range(nk):
        t = jnp.concatenate([t_ref[0, 0, j], t_ref[0, 1, j]], axis=0)
        y = _dot(m_ref[j], t)
        yr, yi = y[:l2].astype(BF16), y[l2:].astype(BF16)
        for h in range(FNET_HEADS):
            sl = slice(h * FNET_HEAD_DIM, (h + 1) * FNET_HEAD_DIM)
            outs.append(_dot(jnp.concatenate([yr[:, sl], yi[:, sl]], axis=1), fc))
    out = jnp.concatenate(outs, axis=1) * scale
    o_ref[0] = out.reshape(l2, nk, c).astype(o_ref.dtype)


def _fourier(uf, nb, length):
    c = D_MODEL
    l2 = FFT_INNER
    l1 = length // l2
    f1, m2, fc = _dft_tables(length)
    x4 = uf.reshape(nb, l1, l2, c)
    tn = math.gcd(l2, BF16_SUBLANE_TILE)
    t5 = pl.pallas_call(
        _fft1_body,
        out_shape=jax.ShapeDtypeStruct((nb, 2, l1, l2, c), BF16),
        grid=(nb, l2 // tn),
        in_specs=[pl.BlockSpec((2 * l1, l1), lambda b, j: (0, 0)),
                  pl.BlockSpec((1, l1, tn, c), lambda b, j: (b, 0, j, 0))],
        out_specs=pl.BlockSpec((1, 2, l1, tn, c), lambda b, j: (b, 0, 0, j, 0)),
        compiler_params=_cparams("parallel", "parallel"),
        name="fft_stage1",
    )(f1, x4)
    nk = math.gcd(l1, FFT_K1_PER_STEP)
    scale = 1.0 / math.sqrt(length * FNET_HEAD_DIM)
    y = pl.pallas_call(
        functools.partial(_fft2_body, scale),
        out_shape=jax.ShapeDtypeStruct((nb, l2, l1, c), BF16),
        grid=(nb, l1 // nk),
        in_specs=[pl.BlockSpec((nk, 2 * l2, 2 * l2), lambda b, k: (k, 0, 0)),
                  pl.BlockSpec((2 * FNET_HEAD_DIM, FNET_HEAD_DIM), lambda b, k: (0, 0)),
                  pl.BlockSpec((1, 2, nk, l2, c), lambda b, k: (b, 0, k, 0, 0))],
        out_specs=pl.BlockSpec((1, l2, nk, c), lambda b, k: (b, 0, k, 0)),
        compiler_params=_cparams("parallel", "parallel"),
        name="fft_stage2",
    )(m2, fc, t5)
    return y.reshape(nb * length, c)


def _shift_rows(x, k):
    n = x.shape[0]
    return x if k % n == 0 else pltpu.roll(x, (-k) % n, 0)


def _split_bf16(v, pieces):
    out = []
    for _ in range(pieces):
        p = v.astype(BF16)
        out.append(p)
        v = v - p.astype(F32)
    return out


def _ssd_chunk(reverse, rows, x_ref, b_ref, c_ref, dt_ref, dtb_ref, alog_ref, expand_ref, state_ref):
    q = SSM_CHUNK
    n = SSM_STATE
    assert q == 128 and n == 128
    ch0 = SSM_HEADS * (1 if reverse else 0)
    x_b = x_ref[0, rows]
    raw = dt_ref[0, rows] + dtb_ref[...]
    e = jnp.exp(-jnp.abs(raw))
    u = 1.0 + e
    um1 = u - 1.0
    dt = jnp.maximum(raw, 0.0) + jnp.where(um1 == 0.0, e, jnp.log(u) * (e / jnp.where(um1 == 0.0, 1.0, um1)))
    da = dt * (-LOG2E * jnp.exp(alog_ref[...]))
    row = lax.broadcasted_iota(jnp.int32, (q, q), 0)
    col = lax.broadcasted_iota(jnp.int32, (q, q), 1)
    mask = (col >= row) if reverse else (col <= row)
    tri = jnp.where(mask, 1.0, 0.0).astype(BF16)
    acum = sum(_dot(tri, p) for p in _split_bf16(da, 3))
    src_t = (acum - jnp.log2(dt)).T
    total = acum[0:1] if reverse else acum[q - 1:q]
    expand = expand_ref[...]
    step_w = _dot((dt * jnp.exp2(total - acum)).astype(BF16), expand)
    xw = (x_b.astype(F32) * step_w).astype(BF16)
    e_total = jnp.exp2(jnp.broadcast_to(total, (8, 128)))
    e_total = sum(_dot(p, expand) for p in _split_bf16(e_total, 3))[0:1]

    first_head = lax.broadcasted_iota(jnp.int32, (q, 2 * SSM_HEAD_DIM), 1) < SSM_HEAD_DIM
    ys = []
    for g in range(SSM_GROUPS):
        bg = b_ref[0, rows, g * n:(g + 1) * n]
        cg = c_ref[0, rows, g * n:(g + 1) * n]
        scores = _dot_nt(cg, bg)
        cg_f = cg.astype(F32)
        state = state_ref[g]
        state_b = state.astype(BF16)
        pairs = []
        for j in range(SSM_HEADS_PER_GROUP // 2):
            lhs = []
            for h in (2 * j, 2 * j + 1):
                ch = ch0 + g * SSM_HEADS_PER_GROUP + h
                a_l = jnp.broadcast_to(acum[:, ch:ch + 1], (q, q))
                decay_dt = jnp.exp2(jnp.where(mask, a_l - src_t[ch:ch + 1, :], -jnp.inf))
                s_h = (scores * decay_dt).astype(BF16)
                c_h = (cg_f * jnp.exp2(a_l)).astype(BF16)
                lhs.append(jnp.concatenate([s_h, c_h], axis=1))
            lanes = slice(g * SSM_GROUP_WIDTH + 128 * j, g * SSM_GROUP_WIDTH + 128 * (j + 1))
            rhs = jnp.concatenate([x_b[:, lanes], state_b[:, 128 * j:128 * (j + 1)]], axis=0)
            out = _dot(jnp.concatenate(lhs, axis=0), rhs)
            pairs.append(jnp.where(first_head, out[:q], out[q:]))
        ys.append(jnp.concatenate(pairs, axis=1))
        gcols = slice(g * SSM_GROUP_WIDTH, (g + 1) * SSM_GROUP_WIDTH)
        state_ref[g] = state * e_total[:, gcols] + _dot_tn(bg, xw[:, gcols])
    return jnp.concatenate(ys, axis=1)


SSD_CHUNKS_PER_STEP = 4


def _ssd_body(xf_ref, bf_ref, cf_ref, dtf_ref, xr_ref, br_ref, cr_ref, dtr_ref, dtb_ref, alog_ref,
              ef_ref, er_ref, dskip_ref, yf_ref, yr_ref, sf_ref, sr_ref):
    @pl.when(pl.program_id(1) == 0)
    def _():
        sf_ref[...] = jnp.zeros_like(sf_ref)
        sr_ref[...] = jnp.zeros_like(sr_ref)

    q = SSM_CHUNK
    per_step = xf_ref.shape[1] // q
    for j in range(per_step):
        rows = slice(j * q, (j + 1) * q)
        yf = _ssd_chunk(False, rows, xf_ref, bf_ref, cf_ref, dtf_ref, dtb_ref, alog_ref, ef_ref, sf_ref)
        yf_ref[0, rows] = yf.astype(yf_ref.dtype)
        rows = slice((per_step - 1 - j) * q, (per_step - j) * q)
        yr = _ssd_chunk(True, rows, xr_ref, br_ref, cr_ref, dtr_ref, dtb_ref, alog_ref, er_ref, sr_ref)
        yr_ref[0, rows] = (yr + dskip_ref[...] * xr_ref[0, rows].astype(F32)).astype(yr_ref.dtype)


def _ssd(xbc, dt, dt_bias, a_log, d_skip):
    nb, length, _ = xbc.shape
    q = SSM_CHUNK * math.gcd(length // SSM_CHUNK, SSD_CHUNKS_PER_STEP)
    nc = length // q
    fixed = lambda b, c: (0, 0)
    state = pltpu.VMEM((SSM_GROUPS, SSM_STATE, SSM_GROUP_WIDTH), F32)
    pad = 128 - 2 * SSM_HEADS
    dtb = jnp.pad(dt_bias.reshape(1, 2 * SSM_HEADS), ((0, 0), (0, pad)))
    alog = jnp.pad(a_log.reshape(1, 2 * SSM_HEADS), ((0, 0), (0, pad)))

    def head_expand(direction):
        e = np.zeros((128, SSM_INNER), np.float32)
        for h in range(SSM_HEADS):
            e[direction * SSM_HEADS + h, h * SSM_HEAD_DIM:(h + 1) * SSM_HEAD_DIM] = 1.0
        return jnp.asarray(e, BF16)

    def specs(chunk):
        return [
            pl.BlockSpec((1, q, SSM_INNER), lambda b, c: (b, chunk(c), 0)),
            pl.BlockSpec((1, q, SSM_BC_WIDTH), lambda b, c: (b, chunk(c), SSM_INNER // SSM_BC_WIDTH)),
            pl.BlockSpec((1, q, SSM_BC_WIDTH), lambda b, c: (b, chunk(c), SSM_INNER // SSM_BC_WIDTH + 1)),
            pl.BlockSpec((1, q, 128), lambda b, c: (b, chunk(c), 0)),
        ]

    fw = lambda c: c
    rv = lambda c: nc - 1 - c
    dskip = jnp.repeat(d_skip, SSM_HEAD_DIM).reshape(1, SSM_INNER)
    out = jax.ShapeDtypeStruct((nb, length, SSM_INNER), BF16)
    return pl.pallas_call(
        _ssd_body,
        out_shape=(out, out),
        grid=(nb, nc),
        in_specs=specs(fw) + specs(rv) + [
            pl.BlockSpec((1, 128), fixed),
            pl.BlockSpec((1, 128), fixed),
            pl.BlockSpec((128, SSM_INNER), fixed),
            pl.BlockSpec((128, SSM_INNER), fixed),
            pl.BlockSpec((1, SSM_INNER), fixed),
        ],
        out_specs=(pl.BlockSpec((1, q, SSM_INNER), lambda b, c: (b, fw(c), 0)),
                   pl.BlockSpec((1, q, SSM_INNER), lambda b, c: (b, rv(c), 0))),
        scratch_shapes=[state, state],
        compiler_params=_cparams("parallel", "arbitrary"),
        name="ssd_scan",
    )(xbc, xbc, xbc, dt, xbc, xbc, xbc, dt, dtb, alog, head_expand(0), head_expand(1), dskip)


def _ab_out_body(x_ref, yf_ref, sf_ref, sr_ref, z_ref, gn_ref, w_ref, o_ref):
    gw = SSM_GROUP_WIDTH
    acc = x_ref[...] + _dot(yf_ref[...], w_ref[0:D_MODEL, :])
    for g in range(SSM_GROUPS):
        cols = slice(g * gw, (g + 1) * gw)
        y = (sf_ref[:, cols].astype(F32) + sr_ref[:, cols].astype(F32)) * _silu(z_ref[:, cols].astype(F32))
        yn = _rms(y, gn_ref[:, cols]).astype(BF16)
        acc = acc + _dot(yn, w_ref[D_MODEL + g * gw:D_MODEL + (g + 1) * gw, :])
    o_ref[...] = acc


def _ab_out(x, y_four, y_fw, y_bw, z, gate_norm, w_out):
    t, d = x.shape
    tm = _row_tile(t, 512)
    row = lambda i: (i, 0)
    fixed = lambda i: (0, 0)
    return pl.pallas_call(
        _ab_out_body,
        out_shape=jax.ShapeDtypeStruct((t, d), F32),
        grid=(t // tm,),
        in_specs=[
            pl.BlockSpec((tm, d), row),
            pl.BlockSpec((tm, D_MODEL), row),
            pl.BlockSpec((tm, SSM_INNER), row),
            pl.BlockSpec((tm, SSM_INNER), row),
            pl.BlockSpec((tm, SSM_INNER), row),
            pl.BlockSpec((1, SSM_INNER), fixed),
            pl.BlockSpec(w_out.shape, fixed),
        ],
        out_specs=pl.BlockSpec((tm, d), row),
        compiler_params=_cparams("parallel"),
        name="ab_out",
    )(x, y_four, y_fw, y_bw, z, gate_norm.reshape(1, SSM_INNER), w_out)


def _rope_tables(length):
    inv = ROPE_THETA ** (-jnp.arange(0, ROT_DIM, 2, dtype=F32) / ROT_DIM)
    ang = jnp.arange(length, dtype=F32)[:, None] * inv[None, :]
    cos, sin = jnp.cos(ang), jnp.sin(ang)
    half = ROT_DIM // 2
    pad = DIFF_HEAD_DIM - ROT_DIM
    ones = jnp.ones((length, pad), F32)
    zeros = jnp.zeros((length, pad), F32)
    zh = jnp.zeros((length, half), F32)
    c_self = jnp.concatenate([cos, cos, ones], axis=1)
    c_up = jnp.concatenate([-sin, zh, zeros], axis=1)
    c_down = jnp.concatenate([zh, sin, zeros], axis=1)
    rep = 128 // DIFF_HEAD_DIM
    return tuple(jnp.tile(tb, (1, rep)) for tb in (c_self, c_up, c_down))


def _pooled(length, ext, w_ref, s_ref):
    hr = F32_SUBLANE_TILE
    tm = ext.shape[0] - 2 * hr
    gd = POOL_GROUP_DIM
    pos = pl.program_id(1) * tm + lax.broadcasted_iota(jnp.int32, (tm, 1), 0)
    win = ext + _shift_rows(ext, -1)
    outs = []
    for g, w in enumerate(POOL_WINDOWS):
        if g > 0:
            win = win[:, gd:]
            win = _shift_rows(win, -(w // 4)) + _shift_rows(win, w // 4)
        lo = jnp.maximum(pos - w // 2, 0)
        hi = jnp.minimum(pos + w // 2 - 1, length - 1)
        mean = win[hr:hr + tm, :gd] / (hi - lo + 1).astype(F32)
        centred = (mean - ext[hr:hr + tm, g * gd:(g + 1) * gd]).astype(BF16)
        outs.append(_dot(centred, w_ref[g]))
    return jnp.concatenate(outs, axis=1) * s_ref[...]


def _cd_in_body(length, xp_ref, x_ref, xq_ref, g_ref, w_ref, ones_ref, qg_ref, kg_ref, cs_ref, cu_ref, cd_ref,
                pw_ref, ps_ref, yp_ref, q_ref, k_ref, v_ref):
    hr = F32_SUBLANE_TILE
    i = pl.program_id(1)
    keep_prev = (i > 0).astype(F32)
    keep_next = (i < pl.num_programs(1) - 1).astype(F32)
    xn = _rms(x_ref[0], g_ref[...]).astype(BF16)
    halo = _rms(jnp.concatenate([xp_ref[0], xq_ref[0]], axis=0), g_ref[...]).astype(BF16)
    d = D_MODEL
    half = ROT_DIM // 2
    rep = d // 128
    c_self = jnp.tile(cs_ref[...], (1, rep))
    c_up = jnp.tile(cu_ref[...], (1, rep))
    c_down = jnp.tile(cd_ref[...], (1, rep))

    def qk_norm_rope(t, gain):
        sq = (t * t).astype(BF16)
        ms = jnp.concatenate([_dot(sq[:, c:c + 256], ones_ref[...]) for c in range(0, d, 256)], axis=1)
        t = t * lax.rsqrt(ms * (1.0 / DIFF_HEAD_DIM) + EPS) * gain
        return t * c_self + pltpu.roll(t, d - half, 1) * c_up + pltpu.roll(t, half, 1) * c_down

    edge = _dot(halo, w_ref[:, 0:d])
    ext = jnp.concatenate([edge[0:hr] * keep_prev, _dot(xn, w_ref[:, 0:d]), edge[hr:2 * hr] * keep_next], axis=0)
    yp_ref[0] = _pooled(length, ext, pw_ref, ps_ref).astype(BF16)
    q = qk_norm_rope(_dot(xn, w_ref[:, d:2 * d]), qg_ref[...])
    q_ref[0] = (q * (LOG2E * DIFF_HEAD_DIM ** -0.5)).astype(BF16)
    k_ref[0] = qk_norm_rope(_dot(xn, w_ref[:, 2 * d:3 * d]), kg_ref[...]).astype(BF16)
    v_ref[0] = _dot(xn, w_ref[:, 3 * d:4 * d]).astype(BF16)


def _cd_in(x, g, w_in, q_norm, k_norm, pool_w, pool_scale):
    nb, length, d = x.shape
    tm = _row_tile(length, 512)
    hr = F32_SUBLANE_TILE
    per = tm // hr
    last = length // hr - 1
    fixed = lambda b, i: (0, 0)
    tile = lambda b, i: (b, i, 0)
    pos = lambda b, i: (i, 0)
    ones_blk = jnp.asarray(np.kron(np.eye(256 // DIFF_HEAD_DIM), np.ones((DIFF_HEAD_DIM, DIFF_HEAD_DIM))), BF16)
    qg = jnp.tile(q_norm, d // DIFF_HEAD_DIM).reshape(1, d)
    kg = jnp.tile(k_norm, d // DIFF_HEAD_DIM).reshape(1, d)
    out = jax.ShapeDtypeStruct((nb, length, d), BF16)
    return pl.pallas_call(
        functools.partial(_cd_in_body, length),
        out_shape=(out, out, out, out),
        grid=(nb, length // tm),
        in_specs=[
            pl.BlockSpec((1, hr, d), lambda b, i: (b, jnp.maximum(i * per - 1, 0), 0)),
            pl.BlockSpec((1, tm, d), tile),
            pl.BlockSpec((1, hr, d), lambda b, i: (b, jnp.minimum((i + 1) * per, last), 0)),
            pl.BlockSpec((1, d), fixed),
            pl.BlockSpec(w_in.shape, fixed),
            pl.BlockSpec((256, 256), fixed),
            pl.BlockSpec((1, d), fixed),
            pl.BlockSpec((1, d), fixed),
            pl.BlockSpec((tm, 128), pos),
            pl.BlockSpec((tm, 128), pos),
            pl.BlockSpec((tm, 128), pos),
            pl.BlockSpec(pool_w.shape, lambda b, i: (0, 0, 0)),
            pl.BlockSpec((1, d), fixed),
        ],
        out_specs=tuple(pl.BlockSpec((1, tm, d), tile) for _ in range(4)),
        compiler_params=_cparams("parallel", "parallel"),
        name="cd_in",
    )(x, x, x, g.reshape(1, d), w_in, ones_blk, qg, kg, *_rope_tables(length), pool_w, pool_scale.reshape(1, d))


ATTN_Q_TILE = 2048
ATTN_KV_TILE = 4096
ATTN_COL_BLOCK = 1024
ATTN_KEY_BLOCK = 512
ATTN_SUM_ROWS = 16
LOG2E = math.log2(math.e)


def _diff_attn_body(lambda_init, q_ref, k_ref, v_ref, lam_ref, sub_ref, o_ref, qs_ref, m_ref, acc_ref, s_ref):
    kv = pl.program_id(3)
    tq = q_ref.shape[1]
    tk = k_ref.shape[1]
    vd = DIFF_V_DIM

    @pl.when(kv == 0)
    def _():
        q = q_ref[0]
        lane = lax.broadcasted_iota(jnp.int32, q.shape, 1)
        zero = jnp.zeros_like(q)
        qs_ref[0:tq] = jnp.where(lane < DIFF_HEAD_DIM, q, zero)
        qs_ref[tq:2 * tq] = jnp.where(lane >= DIFF_HEAD_DIM, q, zero)
        m_ref[...] = jnp.full_like(m_ref, -jnp.inf)
        acc_ref[...] = jnp.zeros_like(acc_ref)

    vt = jnp.concatenate([v_ref[0].T, jnp.ones((ATTN_SUM_ROWS, tk), BF16)], axis=0)
    cb = s_ref.shape[2]
    kb = math.gcd(tk, ATTN_KEY_BLOCK)
    blocks = [slice(c, c + cb) for c in range(0, 2 * tq, cb)]

    def scores(n):
        top = None
        for r in range(0, tk, kb):
            part = _dot_nt(k_ref[0, r:r + kb], qs_ref[blocks[n]])
            s_ref[n % 2, r:r + kb] = part
            part = jnp.max(part, axis=0, keepdims=True)
            top = part if top is None else jnp.maximum(top, part)
        return top

    top_next = scores(0)
    for n, cols in enumerate(blocks):
        top = top_next
        if n + 1 < len(blocks):
            top_next = scores(n + 1)
        m_prev = m_ref[:, cols]
        m_next = jnp.maximum(m_prev, top)
        alpha = jnp.exp2(m_prev - m_next)
        acc = acc_ref[:, cols] * alpha[0:1]
        for r in range(0, tk, kb):
            p = jnp.exp2(s_ref[n % 2, r:r + kb] - m_next[0:1]).astype(BF16)
            acc = acc + _dot(vt[:, r:r + kb], p)
        acc_ref[:, cols] = acc
        m_ref[:, cols] = m_next

    @pl.when(kv == pl.num_programs(3) - 1)
    def _():
        acc = acc_ref[...]
        o = acc[0:vd] / acc[vd:vd + 1]
        lv = lam_ref[...]
        lam = (jnp.exp(jnp.sum(lv[0:1] * lv[1:2], axis=-1, keepdims=True))
               - jnp.exp(jnp.sum(lv[2:3] * lv[3:4], axis=-1, keepdims=True)) + lambda_init)
        diff = (o[:, 0:tq] - lam * o[:, tq:2 * tq]).T
        o_ref[0] = (_rms(diff, sub_ref[...]) * (1.0 - lambda_init)).astype(o_ref.dtype)


def _diff_attn(q, k, v, lam_vecs, sub_norm, lambda_init):
    nb, length, d = q.shape
    tq = _row_tile(length, ATTN_Q_TILE)
    tk = _row_tile(length, ATTN_KV_TILE)
    hw = 2 * DIFF_HEAD_DIM
    assert hw == 128 and DIFF_V_DIM == 128
    return pl.pallas_call(
        functools.partial(_diff_attn_body, lambda_init),
        out_shape=jax.ShapeDtypeStruct((nb, length, d), BF16),
        grid=(nb, DIFF_HEADS, length // tq, length // tk),
        in_specs=[
            pl.BlockSpec((1, tq, hw), lambda b, h, i, j: (b, i, h)),
            pl.BlockSpec((1, tk, hw), lambda b, h, i, j: (b, j, h)),
            pl.BlockSpec((1, tk, DIFF_V_DIM), lambda b, h, i, j: (b, j, h)),
            pl.BlockSpec((4, DIFF_HEAD_DIM), lambda b, h, i, j: (0, 0)),
            pl.BlockSpec((1, DIFF_V_DIM), lambda b, h, i, j: (0, 0)),
        ],
        out_specs=pl.BlockSpec((1, tq, DIFF_V_DIM), lambda b, h, i, j: (b, i, h)),
        scratch_shapes=[
            pltpu.VMEM((2 * tq, hw), BF16),
            pltpu.VMEM((F32_SUBLANE_TILE, 2 * tq), F32),
            pltpu.VMEM((DIFF_V_DIM + ATTN_SUM_ROWS, 2 * tq), F32),
            pltpu.VMEM((2, tk, min(ATTN_COL_BLOCK, 2 * tq)), F32),
        ],
        compiler_params=_cparams("parallel", "parallel", "parallel", "arbitrary"),
        name="diff_attn",
    )(q, k, v, lam_vecs, sub_norm.reshape(1, DIFF_V_DIM))


def _cd_out_body(x_ref, yp_ref, o_ref_in, w_ref, o_ref):
    d = D_MODEL
    o_ref[...] = x_ref[...] + _dot(yp_ref[...], w_ref[0:d, :]) + _dot(o_ref_in[...], w_ref[d:2 * d, :])


def _cd_out(x, y_pool, o, w_out):
    t, d = x.shape
    tm = _row_tile(t, 512)
    row = lambda i: (i, 0)
    return pl.pallas_call(
        _cd_out_body,
        out_shape=jax.ShapeDtypeStruct((t, d), F32),
        grid=(t // tm,),
        in_specs=[pl.BlockSpec((tm, d), row), pl.BlockSpec((tm, d), row), pl.BlockSpec((tm, d), row),
                  pl.BlockSpec(w_out.shape, lambda i: (0, 0))],
        out_specs=pl.BlockSpec((tm, d), row),
        compiler_params=_cparams("parallel"),
        name="cd_out",
    )(x, y_pool, o, w_out)


def _mem_kv_body(m_ref, g_ref, w_ref, kg_ref, k_ref, v_ref):
    d = D_MODEL
    mn = _rms(m_ref[0], g_ref[...]).astype(BF16)
    k = _dot(mn, w_ref[:, 0:d])
    hd = CROSS_HEAD_DIM
    k_ref[0] = jnp.concatenate(
        [_rms(k[:, h * hd:(h + 1) * hd], kg_ref[...]) for h in range(CROSS_HEADS)], axis=1).astype(BF16)
    v_ref[0] = _dot(mn, w_ref[:, d:2 * d]).astype(BF16)


def _mem_kv(mem, g, w_kv, k_norm):
    nb, n_mem, d = mem.shape
    fixed = lambda b: (0, 0)
    out = jax.ShapeDtypeStruct((nb, n_mem, d), BF16)
    blk = pl.BlockSpec((1, n_mem, d), lambda b: (b, 0, 0))
    return pl.pallas_call(
        _mem_kv_body,
        out_shape=(out, out),
        grid=(nb,),
        in_specs=[blk, pl.BlockSpec((1, d), fixed), pl.BlockSpec(w_kv.shape, fixed),
                  pl.BlockSpec((1, CROSS_HEAD_DIM), fixed)],
        out_specs=(blk, blk),
        compiler_params=_cparams("parallel"),
        name="cross_mem_kv",
    )(mem, g.reshape(1, d), w_kv, k_norm.reshape(1, CROSS_HEAD_DIM))


def _cross_body(x_ref, g_ref, wq_ref, qg_ref, k_ref, v_ref, wo_ref, o_ref):
    hd = CROSS_HEAD_DIM
    x = x_ref[0]
    q = _dot(_rms(x, g_ref[...]).astype(BF16), wq_ref[...])
    heads = []
    for h in range(CROSS_HEADS):
        cols = slice(h * hd, (h + 1) * hd)
        qh = (_rms(q[:, cols], qg_ref[...]) * (hd ** -0.5)).astype(BF16)
        s = _dot_nt(qh, k_ref[0, :, cols])
        p = jnp.exp(s - jnp.max(s, axis=-1, keepdims=True))
        p = p / jnp.sum(p, axis=-1, keepdims=True)
        heads.append(_dot(p.astype(BF16), v_ref[0, :, cols]).astype(BF16))
    o_ref[0] = x + _dot(jnp.concatenate(heads, axis=1), wo_ref[...])


def _cross(x, g, w_q, q_norm, k, v, w_o):
    nb, length, d = x.shape
    n_mem = k.shape[1]
    tm = _row_tile(length, 512)
    fixed = lambda b, i: (0, 0)
    tile = lambda b, i: (b, i, 0)
    per_batch = lambda b, i: (b, 0, 0)
    return pl.pallas_call(
        _cross_body,
        out_shape=jax.ShapeDtypeStruct((nb, length, d), F32),
        grid=(nb, length // tm),
        in_specs=[
            pl.BlockSpec((1, tm, d), tile),
            pl.BlockSpec((1, d), fixed),
            pl.BlockSpec((d, d), fixed),
            pl.BlockSpec((1, CROSS_HEAD_DIM), fixed),
            pl.BlockSpec((1, n_mem, d), per_batch),
            pl.BlockSpec((1, n_mem, d), per_batch),
            pl.BlockSpec((d, d), fixed),
        ],
        out_specs=pl.BlockSpec((1, tm, d), tile),
        compiler_params=_cparams("parallel", "parallel"),
        name="cross_attn",
    )(x, g.reshape(1, d), w_q, q_norm.reshape(1, CROSS_HEAD_DIM), k, v, w_o)


def _lambda_init(layer_idx):
    return 0.8 - 0.6 * math.exp(-0.3 * layer_idx)


def _mixer_ab(x, p, i):
    nb, length, d = x.shape
    t = nb * length
    w_in = p['ab_w_in'][i]
    n_main = D_MODEL + SSM_INNER + SSM_CONV_CH
    w_dt = jnp.pad(w_in[:, n_main:], ((0, 0), (0, 128 - 2 * SSM_HEADS)))
    uf, z, xbc, dt = _ab_in(x.reshape(t, d), p['mix_norm_l'], w_in[:, :n_main].astype(BF16), w_dt.astype(BF16),
                            p['ab_conv_w'][i], p['ab_conv_b'][i], length)
    y_four = _fourier(uf, nb, length)
    y_fw, y_bw = _ssd(xbc.reshape(nb, length, SSM_CONV_CH), dt.reshape(nb, length, 128), p['ab_dt_bias'][i],
                      p['ab_a_log'][i], p['ab_d_skip'][i])
    out = _ab_out(x.reshape(t, d), y_four, y_fw.reshape(t, SSM_INNER), y_bw.reshape(t, SSM_INNER), z,
                  p['ab_gate_norm'][i], p['ab_w_out'][i].astype(BF16))
    return out.reshape(nb, length, d)


def _mixer_cd(x, p, i, layer_idx):
    nb, length, d = x.shape
    t = nb * length
    y_pool, q, k, v = _cd_in(x, p['mix_norm_l'], p['cd_w_in'][i].astype(BF16), p['cd_q_norm'][i], p['cd_k_norm'][i],
                             p['cd_pool_w'][i].astype(BF16), p['cd_pool_scale'][i])
    lam_vecs = jnp.stack([p['cd_lambda_q1'][i], p['cd_lambda_k1'][i], p['cd_lambda_q2'][i], p['cd_lambda_k2'][i]])
    o = _diff_attn(q, k, v, lam_vecs, p['cd_sub_norm'][i], _lambda_init(layer_idx))
    out = _cd_out(x.reshape(t, d), y_pool.reshape(t, d), o.reshape(t, d), p['cd_w_out'][i].astype(BF16))
    return out.reshape(nb, length, d)


def _trunk(x, mem, p, depth):
    nb, length, d = x.shape
    t = nb * length
    for l in range(depth):
        x = _ffn(x.reshape(t, d), p['ffn1_norm'][l], p['ffn1_w_gate'][l].astype(BF16),
                 p['ffn1_w_up'][l].astype(BF16), p['ffn1_w_down'][l].astype(BF16)).reshape(nb, length, d)
        pl_ = dict(p, mix_norm_l=p['mix_norm'][l])
        if l % 2 == 0:
            x = _mixer_ab(x, pl_, l // 2)
        else:
            x = _mixer_cd(x, pl_, l // 2, l)
        mk, mv = _mem_kv(mem, p['cross_mem_norm'][l], p['cross_w_kv'][l].astype(BF16), p['cross_k_norm'][l])
        x = _cross(x, p['cross_norm'][l], p['cross_w_q'][l].astype(BF16), p['cross_q_norm'][l], mk, mv,
                   p['cross_w_o'][l].astype(BF16))
        x = _ffn(x.reshape(t, d), p['ffn2_norm'][l], p['ffn2_w_gate'][l].astype(BF16),
                 p['ffn2_w_up'][l].astype(BF16), p['ffn2_w_down'][l].astype(BF16)).reshape(nb, length, d)
    return x


def kernel(x_prompt, x_sample, mem_prompt, mem_sample, ffn1_norm, ffn1_w_gate, ffn1_w_up, ffn1_w_down, mix_norm, ab_w_in, ab_conv_w, ab_conv_b, ab_dt_bias, ab_a_log, ab_d_skip, ab_gate_norm, ab_w_out, cd_w_in, cd_pool_w, cd_pool_scale, cd_q_norm, cd_k_norm, cd_lambda_q1, cd_lambda_k1, cd_lambda_q2, cd_lambda_k2, cd_sub_norm, cd_w_out, cross_norm, cross_mem_norm, cross_w_q, cross_w_kv, cross_q_norm, cross_k_norm, cross_w_o, ffn2_norm, ffn2_w_gate, ffn2_w_up, ffn2_w_down):
    p = {
        'ffn1_norm': ffn1_norm, 'ffn1_w_gate': ffn1_w_gate, 'ffn1_w_up': ffn1_w_up, 'ffn1_w_down': ffn1_w_down,
        'mix_norm': mix_norm,
        'ab_w_in': ab_w_in, 'ab_conv_w': ab_conv_w, 'ab_conv_b': ab_conv_b, 'ab_dt_bias': ab_dt_bias,
        'ab_a_log': ab_a_log, 'ab_d_skip': ab_d_skip, 'ab_gate_norm': ab_gate_norm, 'ab_w_out': ab_w_out,
        'cd_w_in': cd_w_in, 'cd_pool_w': cd_pool_w, 'cd_pool_scale': cd_pool_scale, 'cd_q_norm': cd_q_norm,
        'cd_k_norm': cd_k_norm, 'cd_lambda_q1': cd_lambda_q1, 'cd_lambda_k1': cd_lambda_k1,
        'cd_lambda_q2': cd_lambda_q2, 'cd_lambda_k2': cd_lambda_k2, 'cd_sub_norm': cd_sub_norm,
        'cd_w_out': cd_w_out,
        'cross_norm': cross_norm, 'cross_mem_norm': cross_mem_norm, 'cross_w_q': cross_w_q,
        'cross_w_kv': cross_w_kv, 'cross_q_norm': cross_q_norm, 'cross_k_norm': cross_k_norm,
        'cross_w_o': cross_w_o,
        'ffn2_norm': ffn2_norm, 'ffn2_w_gate': ffn2_w_gate, 'ffn2_w_up': ffn2_w_up, 'ffn2_w_down': ffn2_w_down,
    }
    depth = ffn1_norm.shape[0]
    return (_trunk(x_prompt, mem_prompt, p, depth), _trunk(x_sample, mem_sample, p, depth))
```

```python
import functools
import math

import numpy as np
import jax
import jax.numpy as jnp
from jax import lax
from jax.experimental import pallas as pl
from jax.experimental.pallas import tpu as pltpu

F32 = jnp.float32
BF16 = jnp.bfloat16
EPS = 1e-6

VMEM_LIMIT_BYTES = 56 * 1024 * 1024
LANES = 128
MXU_DIM = 256
F32_SUBLANE_TILE = 8
BF16_SUBLANE_TILE = 16

D_MODEL = 1024
FNET_HEADS = 4
FNET_HEAD_DIM = 256
FFT_INNER = 64
SSM_HEADS = 32
SSM_HEAD_DIM = 64
SSM_STATE = 128
SSM_GROUPS = 4
SSM_HEADS_PER_GROUP = SSM_HEADS // SSM_GROUPS
SSM_INNER = SSM_HEADS * SSM_HEAD_DIM
SSM_GROUP_WIDTH = SSM_INNER // SSM_GROUPS
SSM_BC_WIDTH = SSM_GROUPS * SSM_STATE
SSM_CONV_CH = SSM_INNER + 2 * SSM_BC_WIDTH
SSM_CONV = 5
SSM_CHUNK = 128
POOL_WINDOWS = (2, 4, 8, 16)
POOL_GROUP_DIM = 256
DIFF_HEADS = 8
DIFF_HEAD_DIM = 64
DIFF_V_DIM = 128
ROT_DIM = 16
ROPE_THETA = 500000.0
CROSS_HEADS = 4
CROSS_HEAD_DIM = 256


def _cparams(*semantics):
    return pltpu.CompilerParams(dimension_semantics=semantics, vmem_limit_bytes=VMEM_LIMIT_BYTES)


def _dot(a, b):
    return jnp.dot(a, b, preferred_element_type=F32)


def _dot_nt(a, b):
    return lax.dot_general(a, b, (((1,), (1,)), ((), ())), preferred_element_type=F32)


def _dot_tn(a, b):
    return lax.dot_general(a, b, (((0,), (0,)), ((), ())), preferred_element_type=F32)


def _rms(x, g):
    return x * lax.rsqrt(jnp.mean(x * x, axis=-1, keepdims=True) + EPS) * g


def _silu(x):
    return x * jax.nn.sigmoid(x)


def _row_tile(n, want):
    t = min(n, want)
    assert n % t == 0, (n, t)
    return t


FFN_ROW_TILE = 512


def _ffn_body(x_ref, g_ref, wg_ref, wu_ref, wd_ref, o_ref):
    x = x_ref[...]
    xn = _rms(x, g_ref[...]).astype(BF16)
    gate = _dot(xn, wg_ref[...])
    up = _dot(xn, wu_ref[...])
    h = (_silu(gate) * up).astype(BF16)
    o_ref[...] = x + 0.5 * _dot(h, wd_ref[...])


def _ffn(x, g, wg, wu, wd):
    t, d = x.shape
    f = wg.shape[1]
    tm = _row_tile(t, FFN_ROW_TILE)
    fixed = lambda i: (0, 0)
    return pl.pallas_call(
        _ffn_body,
        out_shape=jax.ShapeDtypeStruct((t, d), F32),
        grid=(t // tm,),
        in_specs=[
            pl.BlockSpec((tm, d), lambda i: (i, 0)),
            pl.BlockSpec((1, d), fixed),
            pl.BlockSpec((d, f), fixed),
            pl.BlockSpec((d, f), fixed),
            pl.BlockSpec((f, d), fixed),
        ],
        out_specs=pl.BlockSpec((tm, d), lambda i: (i, 0)),
        compiler_params=_cparams("parallel"),
        name="ffn",
    )(x, g.reshape(1, d), wg, wu, wd)


CONV_LANES = 512


def _ab_in_body(tiles_per_seq, xp_ref, x_ref, xq_ref, g_ref, w_ref, wdt_ref, cw_ref, cb_ref,
                uf_ref, z_ref, xbc_ref, dt_ref, xn_ref, ext_ref):
    tm = x_ref.shape[0]
    hr = F32_SUBLANE_TILE
    i = pl.program_id(0)
    keep_prev = (i % tiles_per_seq != 0).astype(F32)
    keep_next = (i % tiles_per_seq != tiles_per_seq - 1).astype(F32)
    xn_ref[0:tm] = _rms(x_ref[...], g_ref[...]).astype(BF16)
    xn_ref[tm:tm + 2 * hr] = _rms(jnp.concatenate([xp_ref[...], xq_ref[...]], axis=0), g_ref[...]).astype(BF16)
    half = SSM_CONV // 2
    col = uf_ref.shape[1] + z_ref.shape[1]
    for n, c in enumerate(range(0, SSM_CONV_CH, CONV_LANES)):
        cols = slice(c, c + CONV_LANES)
        pre = _dot(xn_ref[...], w_ref[:, col + c:col + c + CONV_LANES])
        ext = ext_ref.at[n % ext_ref.shape[0]]
        ext[0:hr] = pre[tm:tm + hr] * keep_prev
        ext[hr:hr + tm] = pre[0:tm]
        ext[hr + tm:2 * hr + tm] = pre[tm + hr:tm + 2 * hr] * keep_next
        acc = cb_ref[:, cols] + ext[hr - half:hr - half + tm] * cw_ref[0:1, cols]
        for j in range(1, SSM_CONV):
            acc = acc + ext[hr - half + j:hr - half + j + tm] * cw_ref[j:j + 1, cols]
        xbc_ref[:, cols] = _silu(acc).astype(xbc_ref.dtype)
    dt_ref[...] = _dot(xn_ref[0:tm], wdt_ref[...])
    col = 0
    for ref in (uf_ref, z_ref):
        width = ref.shape[1]
        for c in range(0, width, 1024):
            ref[:, c:c + 1024] = _dot(xn_ref[0:tm], w_ref[:, col + c:col + c + 1024]).astype(ref.dtype)
        col += width


def _ab_in(x, g, w_main, w_dt, conv_w, conv_b, length):
    t, d = x.shape
    tm = _row_tile(length, 512)
    hr = F32_SUBLANE_TILE
    per = tm // hr
    last = t // hr - 1
    n_main = w_main.shape[1]
    row = lambda i: (i, 0)
    fixed = lambda i: (0, 0)
    return pl.pallas_call(
        functools.partial(_ab_in_body, length // tm),
        out_shape=(
            jax.ShapeDtypeStruct((t, D_MODEL), BF16),
            jax.ShapeDtypeStruct((t, SSM_INNER), BF16),
            jax.ShapeDtypeStruct((t, SSM_CONV_CH), BF16),
            jax.ShapeDtypeStruct((t, LANES), F32),
        ),
        grid=(t // tm,),
        in_specs=[
            pl.BlockSpec((hr, d), lambda i: (jnp.maximum(i * per - 1, 0), 0)),
            pl.BlockSpec((tm, d), row),
            pl.BlockSpec((hr, d), lambda i: (jnp.minimum((i + 1) * per, last), 0)),
            pl.BlockSpec((1, d), fixed),
            pl.BlockSpec((d, n_main), fixed),
            pl.BlockSpec((d, LANES), fixed),
            pl.BlockSpec((SSM_CONV, SSM_CONV_CH), fixed),
            pl.BlockSpec((1, SSM_CONV_CH), fixed),
        ],
        out_specs=(
            pl.BlockSpec((tm, D_MODEL), row),
            pl.BlockSpec((tm, SSM_INNER), row),
            pl.BlockSpec((tm, SSM_CONV_CH), row),
            pl.BlockSpec((tm, LANES),row),
        ),
        scratch_shapes=[pltpu.VMEM((tm + 2 * hr, d), BF16), pltpu.VMEM((2, tm + 2 * hr, CONV_LANES), F32)],
        compiler_params=_cparams("parallel"),
        name="ab_in",
    )(x, x, x, g.reshape(1, d), w_main, w_dt, conv_w, conv_b.reshape(1, SSM_CONV_CH))


def _dft_tables(length):
    l2 = FFT_INNER
    l1 = length // l2
    assert l1 * l2 == length
    k1 = np.arange(l1)
    ang1 = 2.0 * np.pi * ((k1[:, None] * k1[None, :]) % l1) / l1
    f1 = np.concatenate([np.cos(ang1), -np.sin(ang1)], axis=0)
    k2 = np.arange(l2)
    n2 = np.arange(l2)
    kk = k1[:, None, None] + l1 * k2[None, :, None]
    ang2 = 2.0 * np.pi * ((kk * n2[None, None, :]) % length) / length
    mr, mi = np.cos(ang2), -np.sin(ang2)
    m2 = np.concatenate([np.concatenate([mr, -mi], axis=2),
                         np.concatenate([mi, mr], axis=2)], axis=1)
    c = np.arange(FNET_HEAD_DIM)
    angc = 2.0 * np.pi * ((c[:, None] * c[None, :]) % FNET_HEAD_DIM) / FNET_HEAD_DIM
    fc = np.concatenate([np.cos(angc), np.sin(angc)], axis=0)
    return (jnp.asarray(f1, dtype=BF16), jnp.asarray(m2, dtype=BF16), jnp.asarray(fc, dtype=BF16))


def _fft1_body(f_ref, x_ref, o_ref):
    l1, tn, c = x_ref.shape[1:]
    x = x_ref[0].astype(F32).reshape(l1, tn * c).astype(BF16)
    t = _dot(f_ref[...], x)
    o_ref[0] = t.reshape(2, l1, tn, c).astype(o_ref.dtype)


FFT_K1_PER_STEP = 16


def _fft2_body(scale, m_ref, fc_ref, t_ref, o_ref):
    nk, l2, c = t_ref.shape[2:]
    fc = fc_ref[...]
    outs = []
    for j in range(nk):
        t = jnp.concatenate([t_ref[0, 0, j], t_ref[0, 1, j]], axis=0)
        y = _dot(m_ref[j], t)
        yr, yi = y[:l2].astype(BF16), y[l2:].astype(BF16)
        for h in range(FNET_HEADS):
            sl = slice(h * FNET_HEAD_DIM, (h + 1) * FNET_HEAD_DIM)
            outs.append(_dot(jnp.concatenate([yr[:, sl], yi[:, sl]], axis=1), fc))
    out = jnp.concatenate(outs, axis=1) * scale
    o_ref[0] = out.reshape(l2, nk, c).astype(o_ref.dtype)


def _fourier(uf, nb, length):
    c = D_MODEL
    l2 = FFT_INNER
    l1 = length // l2
    f1, m2, fc = _dft_tables(length)
    x4 = uf.reshape(nb, l1, l2, c)
    tn = math.gcd(l2, BF16_SUBLANE_TILE)
    t5 = pl.pallas_call(
        _fft1_body,
        out_shape=jax.ShapeDtypeStruct((nb, 2, l1, l2, c), BF16),
        grid=(nb, l2 // tn),
        in_specs=[pl.BlockSpec((2 * l1, l1), lambda b, j: (0, 0)),
                  pl.BlockSpec((1, l1, tn, c), lambda b, j: (b, 0, j, 0))],
        out_specs=pl.BlockSpec((1, 2, l1, tn, c), lambda b, j: (b, 0, 0, j, 0)),
        compiler_params=_cparams("parallel", "parallel"),
        name="fft_stage1",
    )(f1, x4)
    nk = math.gcd(l1, FFT_K1_PER_STEP)
    scale = 1.0 / math.sqrt(length * FNET_HEAD_DIM)
    y = pl.pallas_call(
        functools.partial(_fft2_body, scale),
        out_shape=jax.ShapeDtypeStruct((nb, l2, l1, c), BF16),
        grid=(nb, l1 // nk),
        in_specs=[pl.BlockSpec((nk, 2 * l2, 2 * l2), lambda b, k: (k, 0, 0)),
                  pl.BlockSpec((2 * FNET_HEAD_DIM, FNET_HEAD_DIM), lambda b, k: (0, 0)),
                  pl.BlockSpec((1, 2, nk, l2, c), lambda b, k: (b, 0, k, 0, 0))],
        out_specs=pl.BlockSpec((1, l2, nk, c), lambda b, k: (b, 0, k, 0)),
        compiler_params=_cparams("parallel", "parallel"),
        name="fft_stage2",
    )(m2, fc, t5)
    return y.reshape(nb * length, c)


def _shift_rows(x, k):
    n = x.shape[0]
    return x if k % n == 0 else pltpu.roll(x, (-k) % n, 0)


def _split_bf16(v, pieces):
    out = []
    for _ in range(pieces):
        p = v.astype(BF16)
        out.append(p)
        v = v - p.astype(F32)
    return out


def _ssd_chunk(reverse, rows, x_ref, b_ref, c_ref, dt_ref, dtb_ref, alog_ref, expand_ref, state_ref):
    q = SSM_CHUNK
    n = SSM_STATE
    assert q == LANES and n == LANES
    ch0 = SSM_HEADS * (1 if reverse else 0)
    x_b = x_ref[0, rows]
    raw = dt_ref[0, rows] + dtb_ref[...]
    e = jnp.exp(-jnp.abs(raw))
    u = 1.0 + e
    um1 = u - 1.0
    dt = jnp.maximum(raw, 0.0) + jnp.where(um1 == 0.0, e, jnp.log(u) * (e / jnp.where(um1 == 0.0, 1.0, um1)))
    da = dt * (-LOG2E * jnp.exp(alog_ref[...]))
    row = lax.broadcasted_iota(jnp.int32, (q, q), 0)
    col = lax.broadcasted_iota(jnp.int32, (q, q), 1)
    mask = (col >= row) if reverse else (col <= row)
    tri = jnp.where(mask, 1.0, 0.0).astype(BF16)
    acum = sum(_dot(tri, p) for p in _split_bf16(da, 3))
    src_t = (acum - jnp.log2(dt)).T
    total = acum[0:1] if reverse else acum[q - 1:q]
    expand = expand_ref[...]
    step_w = _dot((dt * jnp.exp2(total - acum)).astype(BF16), expand)
    xw = (x_b.astype(F32) * step_w).astype(BF16)
    e_total = jnp.exp2(jnp.broadcast_to(total, (F32_SUBLANE_TILE, LANES)))
    e_total = sum(_dot(p, expand) for p in _split_bf16(e_total, 3))[0:1]

    first_head = lax.broadcasted_iota(jnp.int32, (q, 2 * SSM_HEAD_DIM), 1) < SSM_HEAD_DIM
    ys = []
    for g in range(SSM_GROUPS):
        bg = b_ref[0, rows, g * n:(g + 1) * n]
        cg = c_ref[0, rows, g * n:(g + 1) * n]
        scores = _dot_nt(cg, bg)
        cg_f = cg.astype(F32)
        state = state_ref[g]
        state_b = state.astype(BF16)
        pairs = []
        for j in range(SSM_HEADS_PER_GROUP // 2):
            lhs = []
            for h in (2 * j, 2 * j + 1):
                ch = ch0 + g * SSM_HEADS_PER_GROUP + h
                a_l = jnp.broadcast_to(acum[:, ch:ch + 1], (q, q))
                decay_dt = jnp.exp2(jnp.where(mask, a_l - src_t[ch:ch + 1, :], -jnp.inf))
                s_h = (scores * decay_dt).astype(BF16)
                c_h = (cg_f * jnp.exp2(a_l)).astype(BF16)
                lhs.append(jnp.concatenate([s_h, c_h], axis=1))
            pw = 2 * SSM_HEAD_DIM
            lanes = slice(g * SSM_GROUP_WIDTH + pw * j, g * SSM_GROUP_WIDTH + pw * (j + 1))
            rhs = jnp.concatenate([x_b[:, lanes], state_b[:, pw * j:pw * (j + 1)]], axis=0)
            out = _dot(jnp.concatenate(lhs, axis=0), rhs)
            pairs.append(jnp.where(first_head, out[:q], out[q:]))
        ys.append(jnp.concatenate(pairs, axis=1))
        gcols = slice(g * SSM_GROUP_WIDTH, (g + 1) * SSM_GROUP_WIDTH)
        state_ref[g] = state * e_total[:, gcols] + _dot_tn(bg, xw[:, gcols])
    return jnp.concatenate(ys, axis=1)


SSD_CHUNKS_PER_STEP = 4


def _ssd_body(xf_ref, bf_ref, cf_ref, dtf_ref, xr_ref, br_ref, cr_ref, dtr_ref, dtb_ref, alog_ref,
              ef_ref, er_ref, dskip_ref, yf_ref, yr_ref, sf_ref, sr_ref):
    @pl.when(pl.program_id(1) == 0)
    def _():
        sf_ref[...] = jnp.zeros_like(sf_ref)
        sr_ref[...] = jnp.zeros_like(sr_ref)

    q = SSM_CHUNK
    per_step = xf_ref.shape[1] // q
    for j in range(per_step):
        rows = slice(j * q, (j + 1) * q)
        yf = _ssd_chunk(False, rows, xf_ref, bf_ref, cf_ref, dtf_ref, dtb_ref, alog_ref, ef_ref, sf_ref)
        yf_ref[0, rows] = yf.astype(yf_ref.dtype)
        rows = slice((per_step - 1 - j) * q, (per_step - j) * q)
        yr = _ssd_chunk(True, rows, xr_ref, br_ref, cr_ref, dtr_ref, dtb_ref, alog_ref, er_ref, sr_ref)
        yr_ref[0, rows] = (yr + dskip_ref[...] * xr_ref[0, rows].astype(F32)).astype(yr_ref.dtype)


def _ssd(xbc, dt, dt_bias, a_log, d_skip):
    nb, length, _ = xbc.shape
    q = SSM_CHUNK * math.gcd(length // SSM_CHUNK, SSD_CHUNKS_PER_STEP)
    nc = length // q
    fixed = lambda b, c: (0, 0)
    state = pltpu.VMEM((SSM_GROUPS, SSM_STATE, SSM_GROUP_WIDTH), F32)
    pad = LANES - 2 * SSM_HEADS
    dtb = jnp.pad(dt_bias.reshape(1, 2 * SSM_HEADS), ((0, 0), (0, pad)))
    alog = jnp.pad(a_log.reshape(1, 2 * SSM_HEADS), ((0, 0), (0, pad)))

    def head_expand(direction):
        e = np.zeros((LANES, SSM_INNER), np.float32)
        for h in range(SSM_HEADS):
            e[direction * SSM_HEADS + h, h * SSM_HEAD_DIM:(h + 1) * SSM_HEAD_DIM] = 1.0
        return jnp.asarray(e, BF16)

    def specs(chunk):
        return [
            pl.BlockSpec((1, q, SSM_INNER), lambda b, c: (b, chunk(c), 0)),
            pl.BlockSpec((1, q, SSM_BC_WIDTH), lambda b, c: (b, chunk(c), SSM_INNER // SSM_BC_WIDTH)),
            pl.BlockSpec((1, q, SSM_BC_WIDTH), lambda b, c: (b, chunk(c), SSM_INNER // SSM_BC_WIDTH + 1)),
            pl.BlockSpec((1, q, LANES), lambda b, c: (b, chunk(c), 0)),
        ]

    fw = lambda c: c
    rv = lambda c: nc - 1 - c
    dskip = jnp.repeat(d_skip, SSM_HEAD_DIM).reshape(1, SSM_INNER)
    out = jax.ShapeDtypeStruct((nb, length, SSM_INNER), BF16)
    return pl.pallas_call(
        _ssd_body,
        out_shape=(out, out),
        grid=(nb, nc),
        in_specs=specs(fw) + specs(rv) + [
            pl.BlockSpec((1, LANES), fixed),
            pl.BlockSpec((1, LANES), fixed),
            pl.BlockSpec((LANES, SSM_INNER), fixed),
            pl.BlockSpec((LANES, SSM_INNER), fixed),
            pl.BlockSpec((1, SSM_INNER), fixed),
        ],
        out_specs=(pl.BlockSpec((1, q, SSM_INNER), lambda b, c: (b, fw(c), 0)),
                   pl.BlockSpec((1, q, SSM_INNER), lambda b, c: (b, rv(c), 0))),
        scratch_shapes=[state, state],
        compiler_params=_cparams("parallel", "arbitrary"),
        name="ssd_scan",
    )(xbc, xbc, xbc, dt, xbc, xbc, xbc, dt, dtb, alog, head_expand(0), head_expand(1), dskip)


def _ab_out_body(x_ref, yf_ref, sf_ref, sr_ref, z_ref, gn_ref, w_ref, o_ref):
    gw = SSM_GROUP_WIDTH
    acc = x_ref[...] + _dot(yf_ref[...], w_ref[0:D_MODEL, :])
    for g in range(SSM_GROUPS):
        cols = slice(g * gw, (g + 1) * gw)
        y = (sf_ref[:, cols].astype(F32) + sr_ref[:, cols].astype(F32)) * _silu(z_ref[:, cols].astype(F32))
        yn = _rms(y, gn_ref[:, cols]).astype(BF16)
        acc = acc + _dot(yn, w_ref[D_MODEL + g * gw:D_MODEL + (g + 1) * gw, :])
    o_ref[...] = acc


def _ab_out(x, y_four, y_fw, y_bw, z, gate_norm, w_out):
    t, d = x.shape
    tm = _row_tile(t, 512)
    row = lambda i: (i, 0)
    fixed = lambda i: (0, 0)
    return pl.pallas_call(
        _ab_out_body,
        out_shape=jax.ShapeDtypeStruct((t, d), F32),
        grid=(t // tm,),
        in_specs=[
            pl.BlockSpec((tm, d), row),
            pl.BlockSpec((tm, D_MODEL), row),
            pl.BlockSpec((tm, SSM_INNER), row),
            pl.BlockSpec((tm, SSM_INNER), row),
            pl.BlockSpec((tm, SSM_INNER), row),
            pl.BlockSpec((1, SSM_INNER), fixed),
            pl.BlockSpec(w_out.shape, fixed),
        ],
        out_specs=pl.BlockSpec((tm, d), row),
        compiler_params=_cparams("parallel"),
        name="ab_out",
    )(x, y_four, y_fw, y_bw, z, gate_norm.reshape(1, SSM_INNER), w_out)


def _rope_tables(length):
    inv = ROPE_THETA ** (-jnp.arange(0, ROT_DIM, 2, dtype=F32) / ROT_DIM)
    ang = jnp.arange(length, dtype=F32)[:, None] * inv[None, :]
    cos, sin = jnp.cos(ang), jnp.sin(ang)
    half = ROT_DIM // 2
    pad = DIFF_HEAD_DIM - ROT_DIM
    ones = jnp.ones((length, pad), F32)
    zeros = jnp.zeros((length, pad), F32)
    zh = jnp.zeros((length, half), F32)
    c_self = jnp.concatenate([cos, cos, ones], axis=1)
    c_up = jnp.concatenate([-sin, zh, zeros], axis=1)
    c_down = jnp.concatenate([zh, sin, zeros], axis=1)
    rep = LANES // DIFF_HEAD_DIM
    return tuple(jnp.tile(tb, (1, rep)) for tb in (c_self, c_up, c_down))


def _pooled(length, ext, w_ref, s_ref):
    hr = F32_SUBLANE_TILE
    tm = ext.shape[0] - 2 * hr
    gd = POOL_GROUP_DIM
    pos = pl.program_id(1) * tm + lax.broadcasted_iota(jnp.int32, (tm, 1), 0)
    win = ext + _shift_rows(ext, -1)
    outs = []
    for g, w in enumerate(POOL_WINDOWS):
        if g > 0:
            win = win[:, gd:]
            win = _shift_rows(win, -(w // 4)) + _shift_rows(win, w // 4)
        lo = jnp.maximum(pos - w // 2, 0)
        hi = jnp.minimum(pos + w // 2 - 1, length - 1)
        mean = win[hr:hr + tm, :gd] / (hi - lo + 1).astype(F32)
        centred = (mean - ext[hr:hr + tm, g * gd:(g + 1) * gd]).astype(BF16)
        outs.append(_dot(centred, w_ref[g]))
    return jnp.concatenate(outs, axis=1) * s_ref[...]


def _cd_in_body(length, xp_ref, x_ref, xq_ref, g_ref, w_ref, ones_ref, qg_ref, kg_ref, cs_ref, cu_ref, cd_ref,
                pw_ref, ps_ref, yp_ref, q_ref, k_ref, v_ref):
    hr = F32_SUBLANE_TILE
    i = pl.program_id(1)
    keep_prev = (i > 0).astype(F32)
    keep_next = (i < pl.num_programs(1) - 1).astype(F32)
    xn = _rms(x_ref[0], g_ref[...]).astype(BF16)
    halo = _rms(jnp.concatenate([xp_ref[0], xq_ref[0]], axis=0), g_ref[...]).astype(BF16)
    d = D_MODEL
    half = ROT_DIM // 2
    rep = d // LANES
    c_self = jnp.tile(cs_ref[...], (1, rep))
    c_up = jnp.tile(cu_ref[...], (1, rep))
    c_down = jnp.tile(cd_ref[...], (1, rep))

    def qk_norm_rope(t, gain):
        sq = (t * t).astype(BF16)
        ms = jnp.concatenate([_dot(sq[:, c:c + MXU_DIM], ones_ref[...]) for c in range(0, d, MXU_DIM)], axis=1)
        t = t * lax.rsqrt(ms * (1.0 / DIFF_HEAD_DIM) + EPS) * gain
        return t * c_self + pltpu.roll(t, d - half, 1) * c_up + pltpu.roll(t, half, 1) * c_down

    edge = _dot(halo, w_ref[:, 0:d])
    ext = jnp.concatenate([edge[0:hr] * keep_prev, _dot(xn, w_ref[:, 0:d]), edge[hr:2 * hr] * keep_next], axis=0)
    yp_ref[0] = _pooled(length, ext, pw_ref, ps_ref).astype(BF16)
    q = qk_norm_rope(_dot(xn, w_ref[:, d:2 * d]), qg_ref[...])
    q_ref[0] = (q * (LOG2E * DIFF_HEAD_DIM ** -0.5)).astype(BF16)
    k_ref[0] = qk_norm_rope(_dot(xn, w_ref[:, 2 * d:3 * d]), kg_ref[...]).astype(BF16)
    v_ref[0] = _dot(xn, w_ref[:, 3 * d:4 * d]).astype(BF16)


def _cd_in(x, g, w_in, q_norm, k_norm, pool_w, pool_scale):
    nb, length, d = x.shape
    tm = _row_tile(length, 512)
    hr = F32_SUBLANE_TILE
    per = tm // hr
    last = length // hr - 1
    fixed = lambda b, i: (0, 0)
    tile = lambda b, i: (b, i, 0)
    pos = lambda b, i: (i, 0)
    ones_blk = jnp.asarray(np.kron(np.eye(MXU_DIM // DIFF_HEAD_DIM), np.ones((DIFF_HEAD_DIM, DIFF_HEAD_DIM))), BF16)
    qg = jnp.tile(q_norm, d // DIFF_HEAD_DIM).reshape(1, d)
    kg = jnp.tile(k_norm, d // DIFF_HEAD_DIM).reshape(1, d)
    out = jax.ShapeDtypeStruct((nb, length, d), BF16)
    return pl.pallas_call(
        functools.partial(_cd_in_body, length),
        out_shape=(out, out, out, out),
        grid=(nb, length // tm),
        in_specs=[
            pl.BlockSpec((1, hr, d), lambda b, i: (b, jnp.maximum(i * per - 1, 0), 0)),
            pl.BlockSpec((1, tm, d), tile),
            pl.BlockSpec((1, hr, d), lambda b, i: (b, jnp.minimum((i + 1) * per, last), 0)),
            pl.BlockSpec((1, d), fixed),
            pl.BlockSpec(w_in.shape, fixed),
            pl.BlockSpec((MXU_DIM, MXU_DIM), fixed),
            pl.BlockSpec((1, d), fixed),
            pl.BlockSpec((1, d), fixed),
            pl.BlockSpec((tm, LANES),pos),
            pl.BlockSpec((tm, LANES),pos),
            pl.BlockSpec((tm, LANES),pos),
            pl.BlockSpec(pool_w.shape, lambda b, i: (0, 0, 0)),
            pl.BlockSpec((1, d), fixed),
        ],
        out_specs=tuple(pl.BlockSpec((1, tm, d), tile) for _ in range(4)),
        compiler_params=_cparams("parallel", "parallel"),
        name="cd_in",
    )(x, x, x, g.reshape(1, d), w_in, ones_blk, qg, kg, *_rope_tables(length), pool_w, pool_scale.reshape(1, d))


ATTN_Q_TILE = 2048
ATTN_KV_TILE = 4096
ATTN_COL_BLOCK = 1024
ATTN_KEY_BLOCK = 512
ATTN_SUM_ROWS = 16
LOG2E = math.log2(math.e)


def _diff_attn_body(lambda_init, q_ref, k_ref, v_ref, lam_ref, sub_ref, o_ref, qs_ref, m_ref, acc_ref, s_ref):
    kv = pl.program_id(3)
    tq = q_ref.shape[1]
    tk = k_ref.shape[1]
    vd = DIFF_V_DIM

    @pl.when(kv == 0)
    def _():
        q = q_ref[0]
        lane = lax.broadcasted_iota(jnp.int32, q.shape, 1)
        zero = jnp.zeros_like(q)
        qs_ref[0:tq] = jnp.where(lane < DIFF_HEAD_DIM, q, zero)
        qs_ref[tq:2 * tq] = jnp.where(lane >= DIFF_HEAD_DIM, q, zero)
        m_ref[...] = jnp.full_like(m_ref, -jnp.inf)
        acc_ref[...] = jnp.zeros_like(acc_ref)

    vt = jnp.concatenate([v_ref[0].T, jnp.ones((ATTN_SUM_ROWS, tk), BF16)], axis=0)
    cb = s_ref.shape[2]
    kb = math.gcd(tk, ATTN_KEY_BLOCK)
    blocks = [slice(c, c + cb) for c in range(0, 2 * tq, cb)]

    def scores(n):
        top = None
        for r in range(0, tk, kb):
            part = _dot_nt(k_ref[0, r:r + kb], qs_ref[blocks[n]])
            s_ref[n % 2, r:r + kb] = part
            part = jnp.max(part, axis=0, keepdims=True)
            top = part if top is None else jnp.maximum(top, part)
        return top

    top_next = scores(0)
    for n, cols in enumerate(blocks):
        top = top_next
        if n + 1 < len(blocks):
            top_next = scores(n + 1)
        m_prev = m_ref[:, cols]
        m_next = jnp.maximum(m_prev, top)
        alpha = jnp.exp2(m_prev - m_next)
        acc = acc_ref[:, cols] * alpha[0:1]
        for r in range(0, tk, kb):
            p = jnp.exp2(s_ref[n % 2, r:r + kb] - m_next[0:1]).astype(BF16)
            acc = acc + _dot(vt[:, r:r + kb], p)
        acc_ref[:, cols] = acc
        m_ref[:, cols] = m_next

    @pl.when(kv == pl.num_programs(3) - 1)
    def _():
        acc = acc_ref[...]
        o = acc[0:vd] / acc[vd:vd + 1]
        lv = lam_ref[...]
        lam = (jnp.exp(jnp.sum(lv[0:1] * lv[1:2], axis=-1, keepdims=True))
               - jnp.exp(jnp.sum(lv[2:3] * lv[3:4], axis=-1, keepdims=True)) + lambda_init)
        diff = (o[:, 0:tq] - lam * o[:, tq:2 * tq]).T
        o_ref[0] = (_rms(diff, sub_ref[...]) * (1.0 - lambda_init)).astype(o_ref.dtype)


def _diff_attn(q, k, v, lam_vecs, sub_norm, lambda_init):
    nb, length, d = q.shape
    tq = _row_tile(length, ATTN_Q_TILE)
    tk = _row_tile(length, ATTN_KV_TILE)
    hw = 2 * DIFF_HEAD_DIM
    assert hw == LANES and DIFF_V_DIM == LANES
    return pl.pallas_call(
        functools.partial(_diff_attn_body, lambda_init),
        out_shape=jax.ShapeDtypeStruct((nb, length, d), BF16),
        grid=(nb, DIFF_HEADS, length // tq, length // tk),
        in_specs=[
            pl.BlockSpec((1, tq, hw), lambda b, h, i, j: (b, i, h)),
            pl.BlockSpec((1, tk, hw), lambda b, h, i, j: (b, j, h)),
            pl.BlockSpec((1, tk, DIFF_V_DIM), lambda b, h, i, j: (b, j, h)),
            pl.BlockSpec((4, DIFF_HEAD_DIM), lambda b, h, i, j: (0, 0)),
            pl.BlockSpec((1, DIFF_V_DIM), lambda b, h, i, j: (0, 0)),
        ],
        out_specs=pl.BlockSpec((1, tq, DIFF_V_DIM), lambda b, h, i, j: (b, i, h)),
        scratch_shapes=[
            pltpu.VMEM((2 * tq, hw), BF16),
            pltpu.VMEM((F32_SUBLANE_TILE, 2 * tq), F32),
            pltpu.VMEM((DIFF_V_DIM + ATTN_SUM_ROWS, 2 * tq), F32),
            pltpu.VMEM((2, tk, min(ATTN_COL_BLOCK, 2 * tq)), F32),
        ],
        compiler_params=_cparams("parallel", "parallel", "parallel", "arbitrary"),
        name="diff_attn",
    )(q, k, v, lam_vecs, sub_norm.reshape(1, DIFF_V_DIM))


def _cd_out_body(x_ref, yp_ref, o_ref_in, w_ref, o_ref):
    d = D_MODEL
    o_ref[...] = x_ref[...] + _dot(yp_ref[...], w_ref[0:d, :]) + _dot(o_ref_in[...], w_ref[d:2 * d, :])


def _cd_out(x, y_pool, o, w_out):
    t, d = x.shape
    tm = _row_tile(t, 512)
    row = lambda i: (i, 0)
    return pl.pallas_call(
        _cd_out_body,
        out_shape=jax.ShapeDtypeStruct((t, d), F32),
        grid=(t // tm,),
        in_specs=[pl.BlockSpec((tm, d), row), pl.BlockSpec((tm, d), row), pl.BlockSpec((tm, d), row),
                  pl.BlockSpec(w_out.shape, lambda i: (0, 0))],
        out_specs=pl.BlockSpec((tm, d), row),
        compiler_params=_cparams("parallel"),
        name="cd_out",
    )(x, y_pool, o, w_out)


def _mem_kv_body(m_ref, g_ref, w_ref, kg_ref, k_ref, v_ref):
    d = D_MODEL
    mn = _rms(m_ref[0], g_ref[...]).astype(BF16)
    k = _dot(mn, w_ref[:, 0:d])
    hd = CROSS_HEAD_DIM
    k_ref[0] = jnp.concatenate(
        [_rms(k[:, h * hd:(h + 1) * hd], kg_ref[...]) for h in range(CROSS_HEADS)], axis=1).astype(BF16)
    v_ref[0] = _dot(mn, w_ref[:, d:2 * d]).astype(BF16)


def _mem_kv(mem, g, w_kv, k_norm):
    nb, n_mem, d = mem.shape
    fixed = lambda b: (0, 0)
    out = jax.ShapeDtypeStruct((nb, n_mem, d), BF16)
    blk = pl.BlockSpec((1, n_mem, d), lambda b: (b, 0, 0))
    return pl.pallas_call(
        _mem_kv_body,
        out_shape=(out, out),
        grid=(nb,),
        in_specs=[blk, pl.BlockSpec((1, d), fixed), pl.BlockSpec(w_kv.shape, fixed),
                  pl.BlockSpec((1, CROSS_HEAD_DIM), fixed)],
        out_specs=(blk, blk),
        compiler_params=_cparams("parallel"),
        name="cross_mem_kv",
    )(mem, g.reshape(1, d), w_kv, k_norm.reshape(1, CROSS_HEAD_DIM))


def _cross_body(x_ref, g_ref, wq_ref, qg_ref, k_ref, v_ref, wo_ref, o_ref):
    hd = CROSS_HEAD_DIM
    x = x_ref[0]
    q = _dot(_rms(x, g_ref[...]).astype(BF16), wq_ref[...])
    heads = []
    for h in range(CROSS_HEADS):
        cols = slice(h * hd, (h + 1) * hd)
        qh = (_rms(q[:, cols], qg_ref[...]) * (hd ** -0.5)).astype(BF16)
        s = _dot_nt(qh, k_ref[0, :, cols])
        p = jnp.exp(s - jnp.max(s, axis=-1, keepdims=True))
        p = p / jnp.sum(p, axis=-1, keepdims=True)
        heads.append(_dot(p.astype(BF16), v_ref[0, :, cols]).astype(BF16))
    o_ref[0] = x + _dot(jnp.concatenate(heads, axis=1), wo_ref[...])


def _cross(x, g, w_q, q_norm, k, v, w_o):
    nb, length, d = x.shape
    n_mem = k.shape[1]
    tm = _row_tile(length, 512)
    fixed = lambda b, i: (0, 0)
    tile = lambda b, i: (b, i, 0)
    per_batch = lambda b, i: (b, 0, 0)
    return pl.pallas_call(
        _cross_body,
        out_shape=jax.ShapeDtypeStruct((nb, length, d), F32),
        grid=(nb, length // tm),
        in_specs=[
            pl.BlockSpec((1, tm, d), tile),
            pl.BlockSpec((1, d), fixed),
            pl.BlockSpec((d, d), fixed),
            pl.BlockSpec((1, CROSS_HEAD_DIM), fixed),
            pl.BlockSpec((1, n_mem, d), per_batch),
            pl.BlockSpec((1, n_mem, d), per_batch),
            pl.BlockSpec((d, d), fixed),
        ],
        out_specs=pl.BlockSpec((1, tm, d), tile),
        compiler_params=_cparams("parallel", "parallel"),
        name="cross_attn",
    )(x, g.reshape(1, d), w_q, q_norm.reshape(1, CROSS_HEAD_DIM), k, v, w_o)


def _lambda_init(layer_idx):
    return 0.8 - 0.6 * math.exp(-0.3 * layer_idx)


def _mixer_ab(x, p, i):
    nb, length, d = x.shape
    t = nb * length
    w_in = p['ab_w_in'][i]
    n_main = D_MODEL + SSM_INNER + SSM_CONV_CH
    w_dt = jnp.pad(w_in[:, n_main:], ((0, 0), (0, LANES - 2 * SSM_HEADS)))
    uf, z, xbc, dt = _ab_in(x.reshape(t, d), p['mix_norm_l'], w_in[:, :n_main].astype(BF16), w_dt.astype(BF16),
                            p['ab_conv_w'][i], p['ab_conv_b'][i], length)
    y_four = _fourier(uf, nb, length)
    y_fw, y_bw = _ssd(xbc.reshape(nb, length, SSM_CONV_CH), dt.reshape(nb, length, LANES), p['ab_dt_bias'][i],
                      p['ab_a_log'][i], p['ab_d_skip'][i])
    out = _ab_out(x.reshape(t, d), y_four, y_fw.reshape(t, SSM_INNER), y_bw.reshape(t, SSM_INNER), z,
                  p['ab_gate_norm'][i], p['ab_w_out'][i].astype(BF16))
    return out.reshape(nb, length, d)


def _mixer_cd(x, p, i, layer_idx):
    nb, length, d = x.shape
    t = nb * length
    y_pool, q, k, v = _cd_in(x, p['mix_norm_l'], p['cd_w_in'][i].astype(BF16), p['cd_q_norm'][i], p['cd_k_norm'][i],
                             p['cd_pool_w'][i].astype(BF16), p['cd_pool_scale'][i])
    lam_vecs = jnp.stack([p['cd_lambda_q1'][i], p['cd_lambda_k1'][i], p['cd_lambda_q2'][i], p['cd_lambda_k2'][i]])
    o = _diff_attn(q, k, v, lam_vecs, p['cd_sub_norm'][i], _lambda_init(layer_idx))
    out = _cd_out(x.reshape(t, d), y_pool.reshape(t, d), o.reshape(t, d), p['cd_w_out'][i].astype(BF16))
    return out.reshape(nb, length, d)


def _trunk(x, mem, p, depth):
    nb, length, d = x.shape
    t = nb * length
    for l in range(depth):
        x = _ffn(x.reshape(t, d), p['ffn1_norm'][l], p['ffn1_w_gate'][l].astype(BF16),
                 p['ffn1_w_up'][l].astype(BF16), p['ffn1_w_down'][l].astype(BF16)).reshape(nb, length, d)
        pl_ = dict(p, mix_norm_l=p['mix_norm'][l])
        if l % 2 == 0:
            x = _mixer_ab(x, pl_, l // 2)
        else:
            x = _mixer_cd(x, pl_, l // 2, l)
        mk, mv = _mem_kv(mem, p['cross_mem_norm'][l], p['cross_w_kv'][l].astype(BF16), p['cross_k_norm'][l])
        x = _cross(x, p['cross_norm'][l], p['cross_w_q'][l].astype(BF16), p['cross_q_norm'][l], mk, mv,
                   p['cross_w_o'][l].astype(BF16))
        x = _ffn(x.reshape(t, d), p['ffn2_norm'][l], p['ffn2_w_gate'][l].astype(BF16),
                 p['ffn2_w_up'][l].astype(BF16), p['ffn2_w_down'][l].astype(BF16)).reshape(nb, length, d)
    return x


def kernel(x_prompt, x_sample, mem_prompt, mem_sample, ffn1_norm, ffn1_w_gate, ffn1_w_up, ffn1_w_down, mix_norm, ab_w_in, ab_conv_w, ab_conv_b, ab_dt_bias, ab_a_log, ab_d_skip, ab_gate_norm, ab_w_out, cd_w_in, cd_pool_w, cd_pool_scale, cd_q_norm, cd_k_norm, cd_lambda_q1, cd_lambda_k1, cd_lambda_q2, cd_lambda_k2, cd_sub_norm, cd_w_out, cross_norm, cross_mem_norm, cross_w_q, cross_w_kv, cross_q_norm, cross_k_norm, cross_w_o, ffn2_norm, ffn2_w_gate, ffn2_w_up, ffn2_w_down):
    p = {
        'ffn1_norm': ffn1_norm, 'ffn1_w_gate': ffn1_w_gate, 'ffn1_w_up': ffn1_w_up, 'ffn1_w_down': ffn1_w_down,
        'mix_norm': mix_norm,
        'ab_w_in': ab_w_in, 'ab_conv_w': ab_conv_w, 'ab_conv_b': ab_conv_b, 'ab_dt_bias': ab_dt_bias,
        'ab_a_log': ab_a_log, 'ab_d_skip': ab_d_skip, 'ab_gate_norm': ab_gate_norm, 'ab_w_out': ab_w_out,
        'cd_w_in': cd_w_in, 'cd_pool_w': cd_pool_w, 'cd_pool_scale': cd_pool_scale, 'cd_q_norm': cd_q_norm,
        'cd_k_norm': cd_k_norm, 'cd_lambda_q1': cd_lambda_q1, 'cd_lambda_k1': cd_lambda_k1,
        'cd_lambda_q2': cd_lambda_q2, 'cd_lambda_k2': cd_lambda_k2, 'cd_sub_norm': cd_sub_norm,
        'cd_w_out': cd_w_out,
        'cross_norm': cross_norm, 'cross_mem_norm': cross_mem_norm, 'cross_w_q': cross_w_q,
        'cross_w_kv': cross_w_kv, 'cross_q_norm': cross_q_norm, 'cross_k_norm': cross_k_norm,
        'cross_w_o': cross_w_o,
        'ffn2_norm': ffn2_norm, 'ffn2_w_gate': ffn2_w_gate, 'ffn2_w_up': ffn2_w_up, 'ffn2_w_down': ffn2_w_down,
    }
    depth = ffn1_norm.shape[0]
    return (_trunk(x_prompt, mem_prompt, p, depth), _trunk(x_sample, mem_sample, p, depth))
```

```python
import functools
import math

import numpy as np
import jax
import jax.numpy as jnp
from jax import lax
from jax.experimental import pallas as pl
from jax.experimental.pallas import tpu as pltpu

F32 = jnp.float32
BF16 = jnp.bfloat16
EPS = 1e-6

VMEM_LIMIT_BYTES = 56 * 1024 * 1024
LANES = 128
MXU_DIM = 256
F32_SUBLANE_TILE = 8
BF16_SUBLANE_TILE = 16

D_MODEL = 1024
FNET_HEADS = 4
FNET_HEAD_DIM = 256
FFT_INNER = 64
SSM_HEADS = 32
SSM_HEAD_DIM = 64
SSM_STATE = 128
SSM_GROUPS = 4
SSM_HEADS_PER_GROUP = SSM_HEADS // SSM_GROUPS
SSM_INNER = SSM_HEADS * SSM_HEAD_DIM
SSM_GROUP_WIDTH = SSM_INNER // SSM_GROUPS
SSM_BC_WIDTH = SSM_GROUPS * SSM_STATE
SSM_CONV_CH = SSM_INNER + 2 * SSM_BC_WIDTH
SSM_CONV = 5
SSM_CHUNK = 128
POOL_WINDOWS = (2, 4, 8, 16)
POOL_GROUP_DIM = 256
DIFF_HEADS = 8
DIFF_HEAD_DIM = 64
DIFF_V_DIM = 128
ROT_DIM = 16
ROPE_THETA = 500000.0
CROSS_HEADS = 4
CROSS_HEAD_DIM = 256


def _cparams(*semantics):
    return pltpu.CompilerParams(dimension_semantics=semantics, vmem_limit_bytes=VMEM_LIMIT_BYTES)


def _dot(a, b):
    return jnp.dot(a, b, preferred_element_type=F32)


def _dot_nt(a, b):
    return lax.dot_general(a, b, (((1,), (1,)), ((), ())), preferred_element_type=F32)


def _dot_tn(a, b):
    return lax.dot_general(a, b, (((0,), (0,)), ((), ())), preferred_element_type=F32)


def _rms(x, g):
    return x * lax.rsqrt(jnp.mean(x * x, axis=-1, keepdims=True) + EPS) * g


def _silu(x):
    return x * jax.nn.sigmoid(x)


def _row_tile(n, want):
    t = min(n, want)
    assert n % t == 0, (n, t)
    return t


FFN_ROW_TILE = 512


def _ffn_body(x_ref, g_ref, wg_ref, wu_ref, wd_ref, o_ref):
    x = x_ref[...]
    xn = _rms(x, g_ref[...]).astype(BF16)
    gate = _dot(xn, wg_ref[...])
    up = _dot(xn, wu_ref[...])
    h = (_silu(gate) * up).astype(BF16)
    o_ref[...] = x + 0.5 * _dot(h, wd_ref[...])


def _ffn(x, g, wg, wu, wd):
    t, d = x.shape
    f = wg.shape[1]
    tm = _row_tile(t, FFN_ROW_TILE)
    fixed = lambda i: (0, 0)
    return pl.pallas_call(
        _ffn_body,
        out_shape=jax.ShapeDtypeStruct((t, d), F32),
        grid=(t // tm,),
        in_specs=[
            pl.BlockSpec((tm, d), lambda i: (i, 0)),
            pl.BlockSpec((1, d), fixed),
            pl.BlockSpec((d, f), fixed),
            pl.BlockSpec((d, f), fixed),
            pl.BlockSpec((f, d), fixed),
        ],
        out_specs=pl.BlockSpec((tm, d), lambda i: (i, 0)),
        compiler_params=_cparams("parallel"),
        name="ffn",
    )(x, g.reshape(1, d), wg, wu, wd)


CONV_LANES = 512


def _ab_in_body(tiles_per_seq, xp_ref, x_ref, xq_ref, g_ref, w_ref, wdt_ref, cw_ref, cb_ref,
                uf_ref, z_ref, xbc_ref, dt_ref, xn_ref, ext_ref):
    tm = x_ref.shape[0]
    hr = F32_SUBLANE_TILE
    i = pl.program_id(0)
    keep_prev = (i % tiles_per_seq != 0).astype(F32)
    keep_next = (i % tiles_per_seq != tiles_per_seq - 1).astype(F32)
    xn_ref[0:tm] = _rms(x_ref[...], g_ref[...]).astype(BF16)
    xn_ref[tm:tm + 2 * hr] = _rms(jnp.concatenate([xp_ref[...], xq_ref[...]], axis=0), g_ref[...]).astype(BF16)
    half = SSM_CONV // 2
    col = uf_ref.shape[1] + z_ref.shape[1]
    for n, c in enumerate(range(0, SSM_CONV_CH, CONV_LANES)):
        cols = slice(c, c + CONV_LANES)
        pre = _dot(xn_ref[...], w_ref[:, col + c:col + c + CONV_LANES])
        ext = ext_ref.at[n % ext_ref.shape[0]]
        ext[0:hr] = pre[tm:tm + hr] * keep_prev
        ext[hr:hr + tm] = pre[0:tm]
        ext[hr + tm:2 * hr + tm] = pre[tm + hr:tm + 2 * hr] * keep_next
        acc = cb_ref[:, cols] + ext[hr - half:hr - half + tm] * cw_ref[0:1, cols]
        for j in range(1, SSM_CONV):
            acc = acc + ext[hr - half + j:hr - half + j + tm] * cw_ref[j:j + 1, cols]
        xbc_ref[:, cols] = _silu(acc).astype(xbc_ref.dtype)
    dt_ref[...] = _dot(xn_ref[0:tm], wdt_ref[...])
    col = 0
    for ref in (uf_ref, z_ref):
        width = ref.shape[1]
        for c in range(0, width, 1024):
            ref[:, c:c + 1024] = _dot(xn_ref[0:tm], w_ref[:, col + c:col + c + 1024]).astype(ref.dtype)
        col += width


def _ab_in(x, g, w_main, w_dt, conv_w, conv_b, length):
    t, d = x.shape
    tm = _row_tile(length, 512)
    hr = F32_SUBLANE_TILE
    per = tm // hr
    last = t // hr - 1
    n_main = w_main.shape[1]
    row = lambda i: (i, 0)
    fixed = lambda i: (0, 0)
    return pl.pallas_call(
        functools.partial(_ab_in_body, length // tm),
        out_shape=(
            jax.ShapeDtypeStruct((t, D_MODEL), BF16),
            jax.ShapeDtypeStruct((t, SSM_INNER), BF16),
            jax.ShapeDtypeStruct((t, SSM_CONV_CH), BF16),
            jax.ShapeDtypeStruct((t, LANES), F32),
        ),
        grid=(t // tm,),
        in_specs=[
            pl.BlockSpec((hr, d), lambda i: (jnp.maximum(i * per - 1, 0), 0)),
            pl.BlockSpec((tm, d), row),
            pl.BlockSpec((hr, d), lambda i: (jnp.minimum((i + 1) * per, last), 0)),
            pl.BlockSpec((1, d), fixed),
            pl.BlockSpec((d, n_main), fixed),
            pl.BlockSpec((d, LANES), fixed),
            pl.BlockSpec((SSM_CONV, SSM_CONV_CH), fixed),
            pl.BlockSpec((1, SSM_CONV_CH), fixed),
        ],
        out_specs=(
            pl.BlockSpec((tm, D_MODEL), row),
            pl.BlockSpec((tm, SSM_INNER), row),
            pl.BlockSpec((tm, SSM_CONV_CH), row),
            pl.BlockSpec((tm, LANES),row),
        ),
        scratch_shapes=[pltpu.VMEM((tm + 2 * hr, d), BF16), pltpu.VMEM((2, tm + 2 * hr, CONV_LANES), F32)],
        compiler_params=_cparams("parallel"),
        name="ab_in",
    )(x, x, x, g.reshape(1, d), w_main, w_dt, conv_w, conv_b.reshape(1, SSM_CONV_CH))


def _dft_tables(length):
    l2 = FFT_INNER
    l1 = length // l2
    assert l1 * l2 == length
    k1 = np.arange(l1)
    ang1 = 2.0 * np.pi * ((k1[:, None] * k1[None, :]) % l1) / l1
    f1 = np.concatenate([np.cos(ang1), -np.sin(ang1)], axis=0)
    k2 = np.arange(l2)
    n2 = np.arange(l2)
    kk = k1[:, None, None] + l1 * k2[None, :, None]
    ang2 = 2.0 * np.pi * ((kk * n2[None, None, :]) % length) / length
    mr, mi = np.cos(ang2), -np.sin(ang2)
    m2 = np.concatenate([np.concatenate([mr, -mi], axis=2),
                         np.concatenate([mi, mr], axis=2)], axis=1)
    c = np.arange(FNET_HEAD_DIM)
    angc = 2.0 * np.pi * ((c[:, None] * c[None, :]) % FNET_HEAD_DIM) / FNET_HEAD_DIM
    fc = np.concatenate([np.cos(angc), np.sin(angc)], axis=0)
    return (jnp.asarray(f1, dtype=BF16), jnp.asarray(m2, dtype=BF16), jnp.asarray(fc, dtype=BF16))


def _fft1_body(f_ref, x_ref, o_ref):
    l1, tn, c = x_ref.shape[1:]
    x = x_ref[0].astype(F32).reshape(l1, tn * c).astype(BF16)
    t = _dot(f_ref[...], x)
    o_ref[0] = t.reshape(2, l1, tn, c).astype(o_ref.dtype)


FFT_K1_PER_STEP = 16


def _fft2_body(scale, m_ref, fc_ref, t_ref, o_ref):
    nk, l2, c = t_ref.shape[2:]
    fc = fc_ref[...]
    outs = []
    for j in range(nk):
        t = jnp.concatenate([t_ref[0, 0, j], t_ref[0, 1, j]], axis=0)
        y = _dot(m_ref[j], t)
        yr, yi = y[:l2].astype(BF16), y[l2:].astype(BF16)
        for h in range(FNET_HEADS):
            sl = slice(h * FNET_HEAD_DIM, (h + 1) * FNET_HEAD_DIM)
            outs.append(_dot(jnp.concatenate([yr[:, sl], yi[:, sl]], axis=1), fc))
    out = jnp.concatenate(outs, axis=1) * scale
    o_ref[0] = out.reshape(l2, nk, c).astype(o_ref.dtype)


def _fourier(uf, nb, length):
    c = D_MODEL
    l2 = FFT_INNER
    l1 = length // l2
    f1, m2, fc = _dft_tables(length)
    x4 = uf.reshape(nb, l1, l2, c)
    tn = math.gcd(l2, BF16_SUBLANE_TILE)
    t5 = pl.pallas_call(
        _fft1_body,
        out_shape=jax.ShapeDtypeStruct((nb, 2, l1, l2, c), BF16),
        grid=(nb, l2 // tn),
        in_specs=[pl.BlockSpec((2 * l1, l1), lambda b, j: (0, 0)),
                  pl.BlockSpec((1, l1, tn, c), lambda b, j: (b, 0, j, 0))],
        out_specs=pl.BlockSpec((1, 2, l1, tn, c), lambda b, j: (b, 0, 0, j, 0)),
        compiler_params=_cparams("parallel", "parallel"),
        name="fft_stage1",
    )(f1, x4)
    nk = math.gcd(l1, FFT_K1_PER_STEP)
    scale = 1.0 / math.sqrt(length * FNET_HEAD_DIM)
    y = pl.pallas_call(
        functools.partial(_fft2_body, scale),
        out_shape=jax.ShapeDtypeStruct((nb, l2, l1, c), BF16),
        grid=(nb, l1 // nk),
        in_specs=[pl.BlockSpec((nk, 2 * l2, 2 * l2), lambda b, k: (k, 0, 0)),
                  pl.BlockSpec((2 * FNET_HEAD_DIM, FNET_HEAD_DIM), lambda b, k: (0, 0)),
                  pl.BlockSpec((1, 2, nk, l2, c), lambda b, k: (b, 0, k, 0, 0))],
        out_specs=pl.BlockSpec((1, l2, nk, c), lambda b, k: (b, 0, k, 0)),
        compiler_params=_cparams("parallel", "parallel"),
        name="fft_stage2",
    )(m2, fc, t5)
    return y.reshape(nb * length, c)


def _shift_rows(x, k):
    n = x.shape[0]
    return x if k % n == 0 else pltpu.roll(x, (-k) % n, 0)


def _split_bf16(v, pieces):
    out = []
    for _ in range(pieces):
        p = v.astype(BF16)
        out.append(p)
        v = v - p.astype(F32)
    return out


def _ssd_chunk(reverse, rows, x_ref, b_ref, c_ref, dt_ref, dtb_ref, alog_ref, expand_ref, state_ref):
    q = SSM_CHUNK
    n = SSM_STATE
    assert q == LANES and n == LANES
    ch0 = SSM_HEADS * (1 if reverse else 0)
    x_b = x_ref[0, rows]
    raw = dt_ref[0, rows] + dtb_ref[...]
    e = jnp.exp(-jnp.abs(raw))
    u = 1.0 + e
    um1 = u - 1.0
    dt = jnp.maximum(raw, 0.0) + jnp.where(um1 == 0.0, e, jnp.log(u) * (e / jnp.where(um1 == 0.0, 1.0, um1)))
    da = dt * (-LOG2E * jnp.exp(alog_ref[...]))
    row = lax.broadcasted_iota(jnp.int32, (q, q), 0)
    col = lax.broadcasted_iota(jnp.int32, (q, q), 1)
    mask = (col >= row) if reverse else (col <= row)
    tri = jnp.where(mask, 1.0, 0.0).astype(BF16)
    acum = sum(_dot(tri, p) for p in _split_bf16(da, 3))
    src_t = (acum - jnp.log2(dt)).T
    total = acum[0:1] if reverse else acum[q - 1:q]
    expand = expand_ref[...]
    step_w = _dot((dt * jnp.exp2(total - acum)).astype(BF16), expand)
    xw = (x_b.astype(F32) * step_w).astype(BF16)
    e_total = jnp.exp2(jnp.broadcast_to(total, (F32_SUBLANE_TILE, LANES)))
    e_total = sum(_dot(p, expand) for p in _split_bf16(e_total, 3))[0:1]

    first_head = lax.broadcasted_iota(jnp.int32, (q, 2 * SSM_HEAD_DIM), 1) < SSM_HEAD_DIM
    ys = []
    for g in range(SSM_GROUPS):
        bg = b_ref[0, rows, g * n:(g + 1) * n]
        cg = c_ref[0, rows, g * n:(g + 1) * n]
        scores = _dot_nt(cg, bg)
        cg_f = cg.astype(F32)
        state = state_ref[g]
        state_b = state.astype(BF16)
        pairs = []
        for j in range(SSM_HEADS_PER_GROUP // 2):
            lhs = []
            for h in (2 * j, 2 * j + 1):
                ch = ch0 + g * SSM_HEADS_PER_GROUP + h
                a_l = jnp.broadcast_to(acum[:, ch:ch + 1], (q, q))
                decay_dt = jnp.exp2(jnp.where(mask, a_l - src_t[ch:ch + 1, :], -jnp.inf))
                s_h = (scores * decay_dt).astype(BF16)
                c_h = (cg_f * jnp.exp2(a_l)).astype(BF16)
                lhs.append(jnp.concatenate([s_h, c_h], axis=1))
            pw = 2 * SSM_HEAD_DIM
            lanes = slice(g * SSM_GROUP_WIDTH + pw * j, g * SSM_GROUP_WIDTH + pw * (j + 1))
            rhs = jnp.concatenate([x_b[:, lanes], state_b[:, pw * j:pw * (j + 1)]], axis=0)
            out = _dot(jnp.concatenate(lhs, axis=0), rhs)
            pairs.append(jnp.where(first_head, out[:q], out[q:]))
        ys.append(jnp.concatenate(pairs, axis=1))
        gcols = slice(g * SSM_GROUP_WIDTH, (g + 1) * SSM_GROUP_WIDTH)
        state_ref[g] = state * e_total[:, gcols] + _dot_tn(bg, xw[:, gcols])
    return jnp.concatenate(ys, axis=1)


SSD_CHUNKS_PER_STEP = 4


def _ssd_body(xf_ref, bf_ref, cf_ref, dtf_ref, xr_ref, br_ref, cr_ref, dtr_ref, dtb_ref, alog_ref,
              ef_ref, er_ref, dskip_ref, yf_ref, yr_ref, sf_ref, sr_ref):
    @pl.when(pl.program_id(1) == 0)
    def _():
        sf_ref[...] = jnp.zeros_like(sf_ref)
        sr_ref[...] = jnp.zeros_like(sr_ref)

    q = SSM_CHUNK
    per_step = xf_ref.shape[1] // q
    for j in range(per_step):
        rows = slice(j * q, (j + 1) * q)
        yf = _ssd_chunk(False, rows, xf_ref, bf_ref, cf_ref, dtf_ref, dtb_ref, alog_ref, ef_ref, sf_ref)
        yf_ref[0, rows] = yf.astype(yf_ref.dtype)
        rows = slice((per_step - 1 - j) * q, (per_step - j) * q)
        yr = _ssd_chunk(True, rows, xr_ref, br_ref, cr_ref, dtr_ref, dtb_ref, alog_ref, er_ref, sr_ref)
        yr_ref[0, rows] = (yr + dskip_ref[...] * xr_ref[0, rows].astype(F32)).astype(yr_ref.dtype)


def _ssd(xbc, dt, dt_bias, a_log, d_skip):
    nb, length, _ = xbc.shape
    q = SSM_CHUNK * math.gcd(length // SSM_CHUNK, SSD_CHUNKS_PER_STEP)
    nc = length // q
    fixed = lambda b, c: (0, 0)
    state = pltpu.VMEM((SSM_GROUPS, SSM_STATE, SSM_GROUP_WIDTH), F32)
    pad = LANES - 2 * SSM_HEADS
    dtb = jnp.pad(dt_bias.reshape(1, 2 * SSM_HEADS), ((0, 0), (0, pad)))
    alog = jnp.pad(a_log.reshape(1, 2 * SSM_HEADS), ((0, 0), (0, pad)))

    def head_expand(direction):
        e = np.zeros((LANES, SSM_INNER), np.float32)
        for h in range(SSM_HEADS):
            e[direction * SSM_HEADS + h, h * SSM_HEAD_DIM:(h + 1) * SSM_HEAD_DIM] = 1.0
        return jnp.asarray(e, BF16)

    def specs(chunk):
        return [
            pl.BlockSpec((1, q, SSM_INNER), lambda b, c: (b, chunk(c), 0)),
            pl.BlockSpec((1, q, SSM_BC_WIDTH), lambda b, c: (b, chunk(c), SSM_INNER // SSM_BC_WIDTH)),
            pl.BlockSpec((1, q, SSM_BC_WIDTH), lambda b, c: (b, chunk(c), SSM_INNER // SSM_BC_WIDTH + 1)),
            pl.BlockSpec((1, q, LANES), lambda b, c: (b, chunk(c), 0)),
        ]

    fw = lambda c: c
    rv = lambda c: nc - 1 - c
    dskip = jnp.repeat(d_skip, SSM_HEAD_DIM).reshape(1, SSM_INNER)
    out = jax.ShapeDtypeStruct((nb, length, SSM_INNER), BF16)
    return pl.pallas_call(
        _ssd_body,
        out_shape=(out, out),
        grid=(nb, nc),
        in_specs=specs(fw) + specs(rv) + [
            pl.BlockSpec((1, LANES), fixed),
            pl.BlockSpec((1, LANES), fixed),
            pl.BlockSpec((LANES, SSM_INNER), fixed),
            pl.BlockSpec((LANES, SSM_INNER), fixed),
            pl.BlockSpec((1, SSM_INNER), fixed),
        ],
        out_specs=(pl.BlockSpec((1, q, SSM_INNER), lambda b, c: (b, fw(c), 0)),
                   pl.BlockSpec((1, q, SSM_INNER), lambda b, c: (b, rv(c), 0))),
        scratch_shapes=[state, state],
        compiler_params=_cparams("parallel", "arbitrary"),
        name="ssd_scan",
    )(xbc, xbc, xbc, dt, xbc, xbc, xbc, dt, dtb, alog, head_expand(0), head_expand(1), dskip)


def _ab_out_body(x_ref, yf_ref, sf_ref, sr_ref, z_ref, gn_ref, w_ref, o_ref):
    gw = SSM_GROUP_WIDTH
    acc = x_ref[...] + _dot(yf_ref[...], w_ref[0:D_MODEL, :])
    for g in range(SSM_GROUPS):
        cols = slice(g * gw, (g + 1) * gw)
        y = (sf_ref[:, cols].astype(F32) + sr_ref[:, cols].astype(F32)) * _silu(z_ref[:, cols].astype(F32))
        yn = _rms(y, gn_ref[:, cols]).astype(BF16)
        acc = acc + _dot(yn, w_ref[D_MODEL + g * gw:D_MODEL + (g + 1) * gw, :])
    o_ref[...] = acc


def _ab_out(x, y_four, y_fw, y_bw, z, gate_norm, w_out):
    t, d = x.shape
    tm = _row_tile(t, 512)
    row = lambda i: (i, 0)
    fixed = lambda i: (0, 0)
    return pl.pallas_call(
        _ab_out_body,
        out_shape=jax.ShapeDtypeStruct((t, d), F32),
        grid=(t // tm,),
        in_specs=[
            pl.BlockSpec((tm, d), row),
            pl.BlockSpec((tm, D_MODEL), row),
            pl.BlockSpec((tm, SSM_INNER), row),
            pl.BlockSpec((tm, SSM_INNER), row),
            pl.BlockSpec((tm, SSM_INNER), row),
            pl.BlockSpec((1, SSM_INNER), fixed),
            pl.BlockSpec(w_out.shape, fixed),
        ],
        out_specs=pl.BlockSpec((tm, d), row),
        compiler_params=_cparams("parallel"),
        name="ab_out",
    )(x, y_four, y_fw, y_bw, z, gate_norm.reshape(1, SSM_INNER), w_out)


def _rope_tables(length):
    inv = ROPE_THETA ** (-jnp.arange(0, ROT_DIM, 2, dtype=F32) / ROT_DIM)
    ang = jnp.arange(length, dtype=F32)[:, None] * inv[None, :]
    cos, sin = jnp.cos(ang), jnp.sin(ang)
    half = ROT_DIM // 2
    pad = DIFF_HEAD_DIM - ROT_DIM
    ones = jnp.ones((length, pad), F32)
    zeros = jnp.zeros((length, pad), F32)
    zh = jnp.zeros((length, half), F32)
    c_self = jnp.concatenate([cos, cos, ones], axis=1)
    c_up = jnp.concatenate([-sin, zh, zeros], axis=1)
    c_down = jnp.concatenate([zh, sin, zeros], axis=1)
    rep = LANES // DIFF_HEAD_DIM
    return tuple(jnp.tile(tb, (1, rep)) for tb in (c_self, c_up, c_down))


def _pooled(length, ext, w_ref, s_ref):
    hr = F32_SUBLANE_TILE
    tm = ext.shape[0] - 2 * hr
    gd = POOL_GROUP_DIM
    pos = pl.program_id(1) * tm + lax.broadcasted_iota(jnp.int32, (tm, 1), 0)
    win = ext + _shift_rows(ext, -1)
    outs = []
    for g, w in enumerate(POOL_WINDOWS):
        if g > 0:
            win = win[:, gd:]
            win = _shift_rows(win, -(w // 4)) + _shift_rows(win, w // 4)
        lo = jnp.maximum(pos - w // 2, 0)
        hi = jnp.minimum(pos + w // 2 - 1, length - 1)
        mean = win[hr:hr + tm, :gd] / (hi - lo + 1).astype(F32)
        centred = (mean - ext[hr:hr + tm, g * gd:(g + 1) * gd]).astype(BF16)
        outs.append(_dot(centred, w_ref[g]))
    return jnp.concatenate(outs, axis=1) * s_ref[...]


def _cd_in_body(length, xp_ref, x_ref, xq_ref, g_ref, w_ref, ones_ref, qg_ref, kg_ref, cs_ref, cu_ref, cd_ref,
                pw_ref, ps_ref, yp_ref, q_ref, k_ref, v_ref):
    hr = F32_SUBLANE_TILE
    i = pl.program_id(1)
    keep_prev = (i > 0).astype(F32)
    keep_next = (i < pl.num_programs(1) - 1).astype(F32)
    xn = _rms(x_ref[0], g_ref[...]).astype(BF16)
    halo = _rms(jnp.concatenate([xp_ref[0], xq_ref[0]], axis=0), g_ref[...]).astype(BF16)
    d = D_MODEL
    half = ROT_DIM // 2
    rep = d // LANES
    c_self = jnp.tile(cs_ref[...], (1, rep))
    c_up = jnp.tile(cu_ref[...], (1, rep))
    c_down = jnp.tile(cd_ref[...], (1, rep))

    def qk_norm_rope(t, gain):
        sq = (t * t).astype(BF16)
        ms = jnp.concatenate([_dot(sq[:, c:c + MXU_DIM], ones_ref[...]) for c in range(0, d, MXU_DIM)], axis=1)
        t = t * lax.rsqrt(ms * (1.0 / DIFF_HEAD_DIM) + EPS) * gain
        return t * c_self + pltpu.roll(t, d - half, 1) * c_up + pltpu.roll(t, half, 1) * c_down

    edge = _dot(halo, w_ref[:, 0:d])
    ext = jnp.concatenate([edge[0:hr] * keep_prev, _dot(xn, w_ref[:, 0:d]), edge[hr:2 * hr] * keep_next], axis=0)
    yp_ref[0] = _pooled(length, ext, pw_ref, ps_ref).astype(BF16)
    q = qk_norm_rope(_dot(xn, w_ref[:, d:2 * d]), qg_ref[...])
    q_ref[0] = (q * (LOG2E * DIFF_HEAD_DIM ** -0.5)).astype(BF16)
    k_ref[0] = qk_norm_rope(_dot(xn, w_ref[:, 2 * d:3 * d]), kg_ref[...]).astype(BF16)
    v_ref[0] = _dot(xn, w_ref[:, 3 * d:4 * d]).astype(BF16)


def _cd_in(x, g, w_in, q_norm, k_norm, pool_w, pool_scale):
    nb, length, d = x.shape
    tm = _row_tile(length, 512)
    hr = F32_SUBLANE_TILE
    per = tm // hr
    last = length // hr - 1
    fixed = lambda b, i: (0, 0)
    tile = lambda b, i: (b, i, 0)
    pos = lambda b, i: (i, 0)
    ones_blk = jnp.asarray(np.kron(np.eye(MXU_DIM // DIFF_HEAD_DIM), np.ones((DIFF_HEAD_DIM, DIFF_HEAD_DIM))), BF16)
    qg = jnp.tile(q_norm, d // DIFF_HEAD_DIM).reshape(1, d)
    kg = jnp.tile(k_norm, d // DIFF_HEAD_DIM).reshape(1, d)
    out = jax.ShapeDtypeStruct((nb, length, d), BF16)
    return pl.pallas_call(
        functools.partial(_cd_in_body, length),
        out_shape=(out, out, out, out),
        grid=(nb, length // tm),
        in_specs=[
            pl.BlockSpec((1, hr, d), lambda b, i: (b, jnp.maximum(i * per - 1, 0), 0)),
            pl.BlockSpec((1, tm, d), tile),
            pl.BlockSpec((1, hr, d), lambda b, i: (b, jnp.minimum((i + 1) * per, last), 0)),
            pl.BlockSpec((1, d), fixed),
            pl.BlockSpec(w_in.shape, fixed),
            pl.BlockSpec((MXU_DIM, MXU_DIM), fixed),
            pl.BlockSpec((1, d), fixed),
            pl.BlockSpec((1, d), fixed),
            pl.BlockSpec((tm, LANES),pos),
            pl.BlockSpec((tm, LANES),pos),
            pl.BlockSpec((tm, LANES),pos),
            pl.BlockSpec(pool_w.shape, lambda b, i: (0, 0, 0)),
            pl.BlockSpec((1, d), fixed),
        ],
        out_specs=tuple(pl.BlockSpec((1, tm, d), tile) for _ in range(4)),
        compiler_params=_cparams("parallel", "parallel"),
        name="cd_in",
    )(x, x, x, g.reshape(1, d), w_in, ones_blk, qg, kg, *_rope_tables(length), pool_w, pool_scale.reshape(1, d))


ATTN_Q_TILE = 4096
ATTN_KV_TILE = 4096
ATTN_COL_BLOCK = 1024
ATTN_KEY_BLOCK = 512
ATTN_SUM_ROWS = 16
LOG2E = math.log2(math.e)


def _diff_attn_body(lambda_init, q_ref, k_ref, v_ref, lam_ref, sub_ref, o_ref, qs_ref, m_ref, acc_ref, s_ref):
    kv = pl.program_id(3)
    tq = q_ref.shape[1]
    tk = k_ref.shape[1]
    vd = DIFF_V_DIM

    @pl.when(kv == 0)
    def _():
        q = q_ref[0]
        lane = lax.broadcasted_iota(jnp.int32, q.shape, 1)
        zero = jnp.zeros_like(q)
        qs_ref[0:tq] = jnp.where(lane < DIFF_HEAD_DIM, q, zero)
        qs_ref[tq:2 * tq] = jnp.where(lane >= DIFF_HEAD_DIM, q, zero)
        m_ref[...] = jnp.full_like(m_ref, -jnp.inf)
        acc_ref[...] = jnp.zeros_like(acc_ref)

    vt = jnp.concatenate([v_ref[0].T, jnp.ones((ATTN_SUM_ROWS, tk), BF16)], axis=0)
    cb = s_ref.shape[2]
    kb = math.gcd(tk, ATTN_KEY_BLOCK)
    blocks = [slice(c, c + cb) for c in range(0, 2 * tq, cb)]

    def scores(n):
        top = None
        for r in range(0, tk, kb):
            part = _dot_nt(k_ref[0, r:r + kb], qs_ref[blocks[n]])
            s_ref[n % 2, r:r + kb] = part
            part = jnp.max(part, axis=0, keepdims=True)
            top = part if top is None else jnp.maximum(top, part)
        return top

    top_next = scores(0)
    for n, cols in enumerate(blocks):
        top = top_next
        if n + 1 < len(blocks):
            top_next = scores(n + 1)
        m_prev = m_ref[:, cols]
        m_next = jnp.maximum(m_prev, top)
        alpha = jnp.exp2(m_prev - m_next)
        acc = acc_ref[:, cols] * alpha[0:1]
        for r in range(0, tk, kb):
            p = jnp.exp2(s_ref[n % 2, r:r + kb] - m_next[0:1]).astype(BF16)
            acc = acc + _dot(vt[:, r:r + kb], p)
        acc_ref[:, cols] = acc
        m_ref[:, cols] = m_next

    @pl.when(kv == pl.num_programs(3) - 1)
    def _():
        acc = acc_ref[...]
        o = acc[0:vd] / acc[vd:vd + 1]
        lv = lam_ref[...]
        lam = (jnp.exp(jnp.sum(lv[0:1] * lv[1:2], axis=-1, keepdims=True))
               - jnp.exp(jnp.sum(lv[2:3] * lv[3:4], axis=-1, keepdims=True)) + lambda_init)
        diff = (o[:, 0:tq] - lam * o[:, tq:2 * tq]).T
        o_ref[0] = (_rms(diff, sub_ref[...]) * (1.0 - lambda_init)).astype(o_ref.dtype)


def _diff_attn(q, k, v, lam_vecs, sub_norm, lambda_init):
    nb, length, d = q.shape
    tq = _row_tile(length, ATTN_Q_TILE)
    tk = _row_tile(length, ATTN_KV_TILE)
    hw = 2 * DIFF_HEAD_DIM
    assert hw == LANES and DIFF_V_DIM == LANES
    return pl.pallas_call(
        functools.partial(_diff_attn_body, lambda_init),
        out_shape=jax.ShapeDtypeStruct((nb, length, d), BF16),
        grid=(nb, DIFF_HEADS, length // tq, length // tk),
        in_specs=[
            pl.BlockSpec((1, tq, hw), lambda b, h, i, j: (b, i, h)),
            pl.BlockSpec((1, tk, hw), lambda b, h, i, j: (b, j, h)),
            pl.BlockSpec((1, tk, DIFF_V_DIM), lambda b, h, i, j: (b, j, h)),
            pl.BlockSpec((4, DIFF_HEAD_DIM), lambda b, h, i, j: (0, 0)),
            pl.BlockSpec((1, DIFF_V_DIM), lambda b, h, i, j: (0, 0)),
        ],
        out_specs=pl.BlockSpec((1, tq, DIFF_V_DIM), lambda b, h, i, j: (b, i, h)),
        scratch_shapes=[
            pltpu.VMEM((2 * tq, hw), BF16),
            pltpu.VMEM((F32_SUBLANE_TILE, 2 * tq), F32),
            pltpu.VMEM((DIFF_V_DIM + ATTN_SUM_ROWS, 2 * tq), F32),
            pltpu.VMEM((2, tk, min(ATTN_COL_BLOCK, 2 * tq)), F32),
        ],
        compiler_params=_cparams("parallel", "parallel", "parallel", "arbitrary"),
        name="diff_attn",
    )(q, k, v, lam_vecs, sub_norm.reshape(1, DIFF_V_DIM))


def _cd_out_body(x_ref, yp_ref, o_ref_in, w_ref, o_ref):
    d = D_MODEL
    o_ref[...] = x_ref[...] + _dot(yp_ref[...], w_ref[0:d, :]) + _dot(o_ref_in[...], w_ref[d:2 * d, :])


def _cd_out(x, y_pool, o, w_out):
    t, d = x.shape
    tm = _row_tile(t, 512)
    row = lambda i: (i, 0)
    return pl.pallas_call(
        _cd_out_body,
        out_shape=jax.ShapeDtypeStruct((t, d), F32),
        grid=(t // tm,),
        in_specs=[pl.BlockSpec((tm, d), row), pl.BlockSpec((tm, d), row), pl.BlockSpec((tm, d), row),
                  pl.BlockSpec(w_out.shape, lambda i: (0, 0))],
        out_specs=pl.BlockSpec((tm, d), row),
        compiler_params=_cparams("parallel"),
        name="cd_out",
    )(x, y_pool, o, w_out)


def _mem_kv_body(m_ref, g_ref, w_ref, kg_ref, k_ref, v_ref):
    d = D_MODEL
    mn = _rms(m_ref[0], g_ref[...]).astype(BF16)
    k = _dot(mn, w_ref[:, 0:d])
    hd = CROSS_HEAD_DIM
    k_ref[0] = jnp.concatenate(
        [_rms(k[:, h * hd:(h + 1) * hd], kg_ref[...]) for h in range(CROSS_HEADS)], axis=1).astype(BF16)
    v_ref[0] = _dot(mn, w_ref[:, d:2 * d]).astype(BF16)


def _mem_kv(mem, g, w_kv, k_norm):
    nb, n_mem, d = mem.shape
    fixed = lambda b: (0, 0)
    out = jax.ShapeDtypeStruct((nb, n_mem, d), BF16)
    blk = pl.BlockSpec((1, n_mem, d), lambda b: (b, 0, 0))
    return pl.pallas_call(
        _mem_kv_body,
        out_shape=(out, out),
        grid=(nb,),
        in_specs=[blk, pl.BlockSpec((1, d), fixed), pl.BlockSpec(w_kv.shape, fixed),
                  pl.BlockSpec((1, CROSS_HEAD_DIM), fixed)],
        out_specs=(blk, blk),
        compiler_params=_cparams("parallel"),
        name="cross_mem_kv",
    )(mem, g.reshape(1, d), w_kv, k_norm.reshape(1, CROSS_HEAD_DIM))


def _cross_body(x_ref, g_ref, wq_ref, qg_ref, k_ref, v_ref, wo_ref, o_ref):
    hd = CROSS_HEAD_DIM
    x = x_ref[0]
    q = _dot(_rms(x, g_ref[...]).astype(BF16), wq_ref[...])
    heads = []
    for h in range(CROSS_HEADS):
        cols = slice(h * hd, (h + 1) * hd)
        qh = (_rms(q[:, cols], qg_ref[...]) * (hd ** -0.5)).astype(BF16)
        s = _dot_nt(qh, k_ref[0, :, cols])
        p = jnp.exp(s - jnp.max(s, axis=-1, keepdims=True))
        p = p / jnp.sum(p, axis=-1, keepdims=True)
        heads.append(_dot(p.astype(BF16), v_ref[0, :, cols]).astype(BF16))
    o_ref[0] = x + _dot(jnp.concatenate(heads, axis=1), wo_ref[...])


def _cross(x, g, w_q, q_norm, k, v, w_o):
    nb, length, d = x.shape
    n_mem = k.shape[1]
    tm = _row_tile(length, 512)
    fixed = lambda b, i: (0, 0)
    tile = lambda b, i: (b, i, 0)
    per_batch = lambda b, i: (b, 0, 0)
    return pl.pallas_call(
        _cross_body,
        out_shape=jax.ShapeDtypeStruct((nb, length, d), F32),
        grid=(nb, length // tm),
        in_specs=[
            pl.BlockSpec((1, tm, d), tile),
            pl.BlockSpec((1, d), fixed),
            pl.BlockSpec((d, d), fixed),
            pl.BlockSpec((1, CROSS_HEAD_DIM), fixed),
            pl.BlockSpec((1, n_mem, d), per_batch),
            pl.BlockSpec((1, n_mem, d), per_batch),
            pl.BlockSpec((d, d), fixed),
        ],
        out_specs=pl.BlockSpec((1, tm, d), tile),
        compiler_params=_cparams("parallel", "parallel"),
        name="cross_attn",
    )(x, g.reshape(1, d), w_q, q_norm.reshape(1, CROSS_HEAD_DIM), k, v, w_o)


def _lambda_init(layer_idx):
    return 0.8 - 0.6 * math.exp(-0.3 * layer_idx)


def _mixer_ab(x, p, i):
    nb, length, d = x.shape
    t = nb * length
    w_in = p['ab_w_in'][i]
    n_main = D_MODEL + SSM_INNER + SSM_CONV_CH
    w_dt = jnp.pad(w_in[:, n_main:], ((0, 0), (0, LANES - 2 * SSM_HEADS)))
    uf, z, xbc, dt = _ab_in(x.reshape(t, d), p['mix_norm_l'], w_in[:, :n_main].astype(BF16), w_dt.astype(BF16),
                            p['ab_conv_w'][i], p['ab_conv_b'][i], length)
    y_four = _fourier(uf, nb, length)
    y_fw, y_bw = _ssd(xbc.reshape(nb, length, SSM_CONV_CH), dt.reshape(nb, length, LANES), p['ab_dt_bias'][i],
                      p['ab_a_log'][i], p['ab_d_skip'][i])
    out = _ab_out(x.reshape(t, d), y_four, y_fw.reshape(t, SSM_INNER), y_bw.reshape(t, SSM_INNER), z,
                  p['ab_gate_norm'][i], p['ab_w_out'][i].astype(BF16))
    return out.reshape(nb, length, d)


def _mixer_cd(x, p, i, layer_idx):
    nb, length, d = x.shape
    t = nb * length
    y_pool, q, k, v = _cd_in(x, p['mix_norm_l'], p['cd_w_in'][i].astype(BF16), p['cd_q_norm'][i], p['cd_k_norm'][i],
                             p['cd_pool_w'][i].astype(BF16), p['cd_pool_scale'][i])
    lam_vecs = jnp.stack([p['cd_lambda_q1'][i], p['cd_lambda_k1'][i], p['cd_lambda_q2'][i], p['cd_lambda_k2'][i]])
    o = _diff_attn(q, k, v, lam_vecs, p['cd_sub_norm'][i], _lambda_init(layer_idx))
    out = _cd_out(x.reshape(t, d), y_pool.reshape(t, d), o.reshape(t, d), p['cd_w_out'][i].astype(BF16))
    return out.reshape(nb, length, d)


def _trunk(x, mem, p, depth):
    nb, length, d = x.shape
    t = nb * length
    for l in range(depth):
        x = _ffn(x.reshape(t, d), p['ffn1_norm'][l], p['ffn1_w_gate'][l].astype(BF16),
                 p['ffn1_w_up'][l].astype(BF16), p['ffn1_w_down'][l].astype(BF16)).reshape(nb, length, d)
        pl_ = dict(p, mix_norm_l=p['mix_norm'][l])
        if l % 2 == 0:
            x = _mixer_ab(x, pl_, l // 2)
        else:
            x = _mixer_cd(x, pl_, l // 2, l)
        mk, mv = _mem_kv(mem, p['cross_mem_norm'][l], p['cross_w_kv'][l].astype(BF16), p['cross_k_norm'][l])
        x = _cross(x, p['cross_norm'][l], p['cross_w_q'][l].astype(BF16), p['cross_q_norm'][l], mk, mv,
                   p['cross_w_o'][l].astype(BF16))
        x = _ffn(x.reshape(t, d), p['ffn2_norm'][l], p['ffn2_w_gate'][l].astype(BF16),
                 p['ffn2_w_up'][l].astype(BF16), p['ffn2_w_down'][l].astype(BF16)).reshape(nb, length, d)
    return x


def kernel(x_prompt, x_sample, mem_prompt, mem_sample, ffn1_norm, ffn1_w_gate, ffn1_w_up, ffn1_w_down, mix_norm, ab_w_in, ab_conv_w, ab_conv_b, ab_dt_bias, ab_a_log, ab_d_skip, ab_gate_norm, ab_w_out, cd_w_in, cd_pool_w, cd_pool_scale, cd_q_norm, cd_k_norm, cd_lambda_q1, cd_lambda_k1, cd_lambda_q2, cd_lambda_k2, cd_sub_norm, cd_w_out, cross_norm, cross_mem_norm, cross_w_q, cross_w_kv, cross_q_norm, cross_k_norm, cross_w_o, ffn2_norm, ffn2_w_gate, ffn2_w_up, ffn2_w_down):
    p = {
        'ffn1_norm': ffn1_norm, 'ffn1_w_gate': ffn1_w_gate, 'ffn1_w_up': ffn1_w_up, 'ffn1_w_down': ffn1_w_down,
        'mix_norm': mix_norm,
        'ab_w_in': ab_w_in, 'ab_conv_w': ab_conv_w, 'ab_conv_b': ab_conv_b, 'ab_dt_bias': ab_dt_bias,
        'ab_a_log': ab_a_log, 'ab_d_skip': ab_d_skip, 'ab_gate_norm': ab_gate_norm, 'ab_w_out': ab_w_out,
        'cd_w_in': cd_w_in, 'cd_pool_w': cd_pool_w, 'cd_pool_scale': cd_pool_scale, 'cd_q_norm': cd_q_norm,
        'cd_k_norm': cd_k_norm, 'cd_lambda_q1': cd_lambda_q1, 'cd_lambda_k1': cd_lambda_k1,
        'cd_lambda_q2': cd_lambda_q2, 'cd_lambda_k2': cd_lambda_k2, 'cd_sub_norm': cd_sub_norm,
        'cd_w_out': cd_w_out,
        'cross_norm': cross_norm, 'cross_mem_norm': cross_mem_norm, 'cross_w_q': cross_w_q,
        'cross_w_kv': cross_w_kv, 'cross_q_norm': cross_q_norm, 'cross_k_norm': cross_k_norm,
        'cross_w_o': cross_w_o,
        'ffn2_norm': ffn2_norm, 'ffn2_w_gate': ffn2_w_gate, 'ffn2_w_up': ffn2_w_up, 'ffn2_w_down': ffn2_w_down,
    }
    depth = ffn1_norm.shape[0]
    return (_trunk(x_prompt, mem_prompt, p, depth), _trunk(x_sample, mem_sample, p, depth))
```

```python
import functools
import math

import numpy as np
import jax
import jax.numpy as jnp
from jax import lax
from jax.experimental import pallas as pl
from jax.experimental.pallas import tpu as pltpu

F32 = jnp.float32
BF16 = jnp.bfloat16
EPS = 1e-6

VMEM_LIMIT_BYTES = 56 * 1024 * 1024
LANES = 128
MXU_DIM = 256
F32_SUBLANE_TILE = 8
BF16_SUBLANE_TILE = 16

D_MODEL = 1024
FNET_HEADS = 4
FNET_HEAD_DIM = 256
FFT_INNER = 64
SSM_HEADS = 32
SSM_HEAD_DIM = 64
SSM_STATE = 128
SSM_GROUPS = 4
SSM_HEADS_PER_GROUP = SSM_HEADS // SSM_GROUPS
SSM_INNER = SSM_HEADS * SSM_HEAD_DIM
SSM_GROUP_WIDTH = SSM_INNER // SSM_GROUPS
SSM_BC_WIDTH = SSM_GROUPS * SSM_STATE
SSM_CONV_CH = SSM_INNER + 2 * SSM_BC_WIDTH
SSM_CONV = 5
SSM_CHUNK = 128
POOL_WINDOWS = (2, 4, 8, 16)
POOL_GROUP_DIM = 256
DIFF_HEADS = 8
DIFF_HEAD_DIM = 64
DIFF_V_DIM = 128
ROT_DIM = 16
ROPE_THETA = 500000.0
CROSS_HEADS = 4
CROSS_HEAD_DIM = 256


def _cparams(*semantics):
    return pltpu.CompilerParams(dimension_semantics=semantics, vmem_limit_bytes=VMEM_LIMIT_BYTES)


def _dot(a, b):
    return jnp.dot(a, b, preferred_element_type=F32)


def _dot_nt(a, b):
    return lax.dot_general(a, b, (((1,), (1,)), ((), ())), preferred_element_type=F32)


def _dot_tn(a, b):
    return lax.dot_general(a, b, (((0,), (0,)), ((), ())), preferred_element_type=F32)


def _rms(x, g):
    return x * lax.rsqrt(jnp.mean(x * x, axis=-1, keepdims=True) + EPS) * g


def _silu(x):
    return x * jax.nn.sigmoid(x)


def _row_tile(n, want):
    t = min(n, want)
    assert n % t == 0, (n, t)
    return t


FFN_ROW_TILE = 512


def _ffn_body(x_ref, g_ref, wg_ref, wu_ref, wd_ref, o_ref):
    x = x_ref[...]
    xn = _rms(x, g_ref[...]).astype(BF16)
    gate = _dot(xn, wg_ref[...])
    up = _dot(xn, wu_ref[...])
    h = (_silu(gate) * up).astype(BF16)
    o_ref[...] = x + 0.5 * _dot(h, wd_ref[...])


def _ffn(x, g, wg, wu, wd):
    t, d = x.shape
    f = wg.shape[1]
    tm = _row_tile(t, FFN_ROW_TILE)
    fixed = lambda i: (0, 0)
    return pl.pallas_call(
        _ffn_body,
        out_shape=jax.ShapeDtypeStruct((t, d), F32),
        grid=(t // tm,),
        in_specs=[
            pl.BlockSpec((tm, d), lambda i: (i, 0)),
            pl.BlockSpec((1, d), fixed),
            pl.BlockSpec((d, f), fixed),
            pl.BlockSpec((d, f), fixed),
            pl.BlockSpec((f, d), fixed),
        ],
        out_specs=pl.BlockSpec((tm, d), lambda i: (i, 0)),
        compiler_params=_cparams("parallel"),
        name="ffn",
    )(x, g.reshape(1, d), wg, wu, wd)


CONV_LANES = 512


def _ab_in_body(tiles_per_seq, xp_ref, x_ref, xq_ref, g_ref, w_ref, wdt_ref, cw_ref, cb_ref,
                uf_ref, z_ref, xbc_ref, dt_ref, xn_ref, ext_ref):
    tm = x_ref.shape[0]
    hr = F32_SUBLANE_TILE
    i = pl.program_id(0)
    keep_prev = (i % tiles_per_seq != 0).astype(F32)
    keep_next = (i % tiles_per_seq != tiles_per_seq - 1).astype(F32)
    xn_ref[0:tm] = _rms(x_ref[...], g_ref[...]).astype(BF16)
    xn_ref[tm:tm + 2 * hr] = _rms(jnp.concatenate([xp_ref[...], xq_ref[...]], axis=0), g_ref[...]).astype(BF16)
    half = SSM_CONV // 2
    col = uf_ref.shape[1] + z_ref.shape[1]
    for n, c in enumerate(range(0, SSM_CONV_CH, CONV_LANES)):
        cols = slice(c, c + CONV_LANES)
        pre = _dot(xn_ref[...], w_ref[:, col + c:col + c + CONV_LANES])
        ext = ext_ref.at[n % ext_ref.shape[0]]
        ext[0:hr] = pre[tm:tm + hr] * keep_prev
        ext[hr:hr + tm] = pre[0:tm]
        ext[hr + tm:2 * hr + tm] = pre[tm + hr:tm + 2 * hr] * keep_next
        acc = cb_ref[:, cols] + ext[hr - half:hr - half + tm] * cw_ref[0:1, cols]
        for j in range(1, SSM_CONV):
            acc = acc + ext[hr - half + j:hr - half + j + tm] * cw_ref[j:j + 1, cols]
        xbc_ref[:, cols] = _silu(acc).astype(xbc_ref.dtype)
    dt_ref[...] = _dot(xn_ref[0:tm], wdt_ref[...])
    col = 0
    for ref in (uf_ref, z_ref):
        width = ref.shape[1]
        for c in range(0, width, 1024):
            ref[:, c:c + 1024] = _dot(xn_ref[0:tm], w_ref[:, col + c:col + c + 1024]).astype(ref.dtype)
        col += width


def _ab_in(x, g, w_main, w_dt, conv_w, conv_b, length):
    t, d = x.shape
    tm = _row_tile(length, 512)
    hr = F32_SUBLANE_TILE
    per = tm // hr
    last = t // hr - 1
    n_main = w_main.shape[1]
    row = lambda i: (i, 0)
    fixed = lambda i: (0, 0)
    return pl.pallas_call(
        functools.partial(_ab_in_body, length // tm),
        out_shape=(
            jax.ShapeDtypeStruct((t, D_MODEL), BF16),
            jax.ShapeDtypeStruct((t, SSM_INNER), BF16),
            jax.ShapeDtypeStruct((t, SSM_CONV_CH), BF16),
            jax.ShapeDtypeStruct((t, LANES), F32),
        ),
        grid=(t // tm,),
        in_specs=[
            pl.BlockSpec((hr, d), lambda i: (jnp.maximum(i * per - 1, 0), 0)),
            pl.BlockSpec((tm, d), row),
            pl.BlockSpec((hr, d), lambda i: (jnp.minimum((i + 1) * per, last), 0)),
            pl.BlockSpec((1, d), fixed),
            pl.BlockSpec((d, n_main), fixed),
            pl.BlockSpec((d, LANES), fixed),
            pl.BlockSpec((SSM_CONV, SSM_CONV_CH), fixed),
            pl.BlockSpec((1, SSM_CONV_CH), fixed),
        ],
        out_specs=(
            pl.BlockSpec((tm, D_MODEL), row),
            pl.BlockSpec((tm, SSM_INNER), row),
            pl.BlockSpec((tm, SSM_CONV_CH), row),
            pl.BlockSpec((tm, LANES),row),
        ),
        scratch_shapes=[pltpu.VMEM((tm + 2 * hr, d), BF16), pltpu.VMEM((2, tm + 2 * hr, CONV_LANES), F32)],
        compiler_params=_cparams("parallel"),
        name="ab_in",
    )(x, x, x, g.reshape(1, d), w_main, w_dt, conv_w, conv_b.reshape(1, SSM_CONV_CH))


def _dft_tables(length):
    l2 = FFT_INNER
    l1 = length // l2
    assert l1 * l2 == length
    k1 = np.arange(l1)
    ang1 = 2.0 * np.pi * ((k1[:, None] * k1[None, :]) % l1) / l1
    f1 = np.concatenate([np.cos(ang1), -np.sin(ang1)], axis=0)
    k2 = np.arange(l2)
    n2 = np.arange(l2)
    kk = k1[:, None, None] + l1 * k2[None, :, None]
    ang2 = 2.0 * np.pi * ((kk * n2[None, None, :]) % length) / length
    mr, mi = np.cos(ang2), -np.sin(ang2)
    m2 = np.concatenate([np.concatenate([mr, -mi], axis=2),
                         np.concatenate([mi, mr], axis=2)], axis=1)
    c = np.arange(FNET_HEAD_DIM)
    angc = 2.0 * np.pi * ((c[:, None] * c[None, :]) % FNET_HEAD_DIM) / FNET_HEAD_DIM
    fc = np.concatenate([np.cos(angc), np.sin(angc)], axis=0)
    return (jnp.asarray(f1, dtype=BF16), jnp.asarray(m2, dtype=BF16), jnp.asarray(fc, dtype=BF16))


def _fft1_body(f_ref, x_ref, o_ref):
    l1, tn, c = x_ref.shape[1:]
    x = x_ref[0].astype(F32).reshape(l1, tn * c).astype(BF16)
    t = _dot(f_ref[...], x)
    o_ref[0] = t.reshape(2, l1, tn, c).astype(o_ref.dtype)


FFT_K1_PER_STEP = 16


def _fft2_body(scale, m_ref, fc_ref, t_ref, o_ref):
    nk, l2, c = t_ref.shape[2:]
    fc = fc_ref[...]
    outs = []
    for j in range(nk):
        t = jnp.concatenate([t_ref[0, 0, j], t_ref[0, 1, j]], axis=0)
        y = _dot(m_ref[j], t)
        yr, yi = y[:l2].astype(BF16), y[l2:].astype(BF16)
        for h in range(FNET_HEADS):
            sl = slice(h * FNET_HEAD_DIM, (h + 1) * FNET_HEAD_DIM)
            outs.append(_dot(jnp.concatenate([yr[:, sl], yi[:, sl]], axis=1), fc))
    out = jnp.concatenate(outs, axis=1) * scale
    o_ref[0] = out.reshape(l2, nk, c).astype(o_ref.dtype)


def _fourier(uf, nb, length):
    c = D_MODEL
    l2 = FFT_INNER
    l1 = length // l2
    f1, m2, fc = _dft_tables(length)
    x4 = uf.reshape(nb, l1, l2, c)
    tn = math.gcd(l2, BF16_SUBLANE_TILE)
    t5 = pl.pallas_call(
        _fft1_body,
        out_shape=jax.ShapeDtypeStruct((nb, 2, l1, l2, c), BF16),
        grid=(nb, l2 // tn),
        in_specs=[pl.BlockSpec((2 * l1, l1), lambda b, j: (0, 0)),
                  pl.BlockSpec((1, l1, tn, c), lambda b, j: (b, 0, j, 0))],
        out_specs=pl.BlockSpec((1, 2, l1, tn, c), lambda b, j: (b, 0, 0, j, 0)),
        compiler_params=_cparams("parallel", "parallel"),
        name="fft_stage1",
    )(f1, x4)
    nk = math.gcd(l1, FFT_K1_PER_STEP)
    scale = 1.0 / math.sqrt(length * FNET_HEAD_DIM)
    y = pl.pallas_call(
        functools.partial(_fft2_body, scale),
        out_shape=jax.ShapeDtypeStruct((nb, l2, l1, c), BF16),
        grid=(nb, l1 // nk),
        in_specs=[pl.BlockSpec((nk, 2 * l2, 2 * l2), lambda b, k: (k, 0, 0)),
                  pl.BlockSpec((2 * FNET_HEAD_DIM, FNET_HEAD_DIM), lambda b, k: (0, 0)),
                  pl.BlockSpec((1, 2, nk, l2, c), lambda b, k: (b, 0, k, 0, 0))],
        out_specs=pl.BlockSpec((1, l2, nk, c), lambda b, k: (b, 0, k, 0)),
        compiler_params=_cparams("parallel", "parallel"),
        name="fft_stage2",
    )(m2, fc, t5)
    return y.reshape(nb * length, c)


def _shift_rows(x, k):
    n = x.shape[0]
    return x if k % n == 0 else pltpu.roll(x, (-k) % n, 0)


def _split_bf16(v, pieces):
    out = []
    for _ in range(pieces):
        p = v.astype(BF16)
        out.append(p)
        v = v - p.astype(F32)
    return out


def _ssd_chunk(reverse, rows, x_ref, b_ref, c_ref, dt_ref, dtb_ref, alog_ref, expand_ref, state_ref):
    q = SSM_CHUNK
    n = SSM_STATE
    assert q == LANES and n == LANES
    ch0 = SSM_HEADS * (1 if reverse else 0)
    x_b = x_ref[0, rows]
    raw = dt_ref[0, rows] + dtb_ref[...]
    e = jnp.exp(-jnp.abs(raw))
    u = 1.0 + e
    um1 = u - 1.0
    dt = jnp.maximum(raw, 0.0) + jnp.where(um1 == 0.0, e, jnp.log(u) * (e / jnp.where(um1 == 0.0, 1.0, um1)))
    da = dt * (-LOG2E * jnp.exp(alog_ref[...]))
    row = lax.broadcasted_iota(jnp.int32, (q, q), 0)
    col = lax.broadcasted_iota(jnp.int32, (q, q), 1)
    mask = (col >= row) if reverse else (col <= row)
    tri = jnp.where(mask, 1.0, 0.0).astype(BF16)
    acum = sum(_dot(tri, p) for p in _split_bf16(da, 3))
    src_t = (acum - jnp.log2(dt)).T
    total = acum[0:1] if reverse else acum[q - 1:q]
    expand = expand_ref[...]
    step_w = _dot((dt * jnp.exp2(total - acum)).astype(BF16), expand)
    xw = (x_b.astype(F32) * step_w).astype(BF16)
    e_total = jnp.exp2(jnp.broadcast_to(total, (F32_SUBLANE_TILE, LANES)))
    e_total = sum(_dot(p, expand) for p in _split_bf16(e_total, 3))[0:1]

    first_head = lax.broadcasted_iota(jnp.int32, (q, 2 * SSM_HEAD_DIM), 1) < SSM_HEAD_DIM
    ys = []
    for g in range(SSM_GROUPS):
        bg = b_ref[0, rows, g * n:(g + 1) * n]
        cg = c_ref[0, rows, g * n:(g + 1) * n]
        scores = _dot_nt(cg, bg)
        cg_f = cg.astype(F32)
        state = state_ref[g]
        state_b = state.astype(BF16)
        pairs = []
        for j in range(SSM_HEADS_PER_GROUP // 2):
            lhs = []
            for h in (2 * j, 2 * j + 1):
                ch = ch0 + g * SSM_HEADS_PER_GROUP + h
                a_l = jnp.broadcast_to(acum[:, ch:ch + 1], (q, q))
                decay_dt = jnp.exp2(jnp.where(mask, a_l - src_t[ch:ch + 1, :], -jnp.inf))
                s_h = (scores * decay_dt).astype(BF16)
                c_h = (cg_f * jnp.exp2(a_l)).astype(BF16)
                lhs.append(jnp.concatenate([s_h, c_h], axis=1))
            pw = 2 * SSM_HEAD_DIM
            lanes = slice(g * SSM_GROUP_WIDTH + pw * j, g * SSM_GROUP_WIDTH + pw * (j + 1))
            rhs = jnp.concatenate([x_b[:, lanes], state_b[:, pw * j:pw * (j + 1)]], axis=0)
            out = _dot(jnp.concatenate(lhs, axis=0), rhs)
            pairs.append(jnp.where(first_head, out[:q], out[q:]))
        ys.append(jnp.concatenate(pairs, axis=1))
        gcols = slice(g * SSM_GROUP_WIDTH, (g + 1) * SSM_GROUP_WIDTH)
        state_ref[g] = state * e_total[:, gcols] + _dot_tn(bg, xw[:, gcols])
    return jnp.concatenate(ys, axis=1)


SSD_CHUNKS_PER_STEP = 4


def _ssd_body(xf_ref, bf_ref, cf_ref, dtf_ref, xr_ref, br_ref, cr_ref, dtr_ref, dtb_ref, alog_ref,
              ef_ref, er_ref, dskip_ref, yf_ref, yr_ref, sf_ref, sr_ref):
    @pl.when(pl.program_id(1) == 0)
    def _():
        sf_ref[...] = jnp.zeros_like(sf_ref)
        sr_ref[...] = jnp.zeros_like(sr_ref)

    q = SSM_CHUNK
    per_step = xf_ref.shape[1] // q
    for j in range(per_step):
        rows = slice(j * q, (j + 1) * q)
        yf = _ssd_chunk(False, rows, xf_ref, bf_ref, cf_ref, dtf_ref, dtb_ref, alog_ref, ef_ref, sf_ref)
        yf_ref[0, rows] = yf.astype(yf_ref.dtype)
        rows = slice((per_step - 1 - j) * q, (per_step - j) * q)
        yr = _ssd_chunk(True, rows, xr_ref, br_ref, cr_ref, dtr_ref, dtb_ref, alog_ref, er_ref, sr_ref)
        yr_ref[0, rows] = (yr + dskip_ref[...] * xr_ref[0, rows].astype(F32)).astype(yr_ref.dtype)


def _ssd(xbc, dt, dt_bias, a_log, d_skip):
    nb, length, _ = xbc.shape
    q = SSM_CHUNK * math.gcd(length // SSM_CHUNK, SSD_CHUNKS_PER_STEP)
    nc = length // q
    fixed = lambda b, c: (0, 0)
    state = pltpu.VMEM((SSM_GROUPS, SSM_STATE, SSM_GROUP_WIDTH), F32)
    pad = LANES - 2 * SSM_HEADS
    dtb = jnp.pad(dt_bias.reshape(1, 2 * SSM_HEADS), ((0, 0), (0, pad)))
    alog = jnp.pad(a_log.reshape(1, 2 * SSM_HEADS), ((0, 0), (0, pad)))

    def head_expand(direction):
        e = np.zeros((LANES, SSM_INNER), np.float32)
        for h in range(SSM_HEADS):
            e[direction * SSM_HEADS + h, h * SSM_HEAD_DIM:(h + 1) * SSM_HEAD_DIM] = 1.0
        return jnp.asarray(e, BF16)

    def specs(chunk):
        return [
            pl.BlockSpec((1, q, SSM_INNER), lambda b, c: (b, chunk(c), 0)),
            pl.BlockSpec((1, q, SSM_BC_WIDTH), lambda b, c: (b, chunk(c), SSM_INNER // SSM_BC_WIDTH)),
            pl.BlockSpec((1, q, SSM_BC_WIDTH), lambda b, c: (b, chunk(c), SSM_INNER // SSM_BC_WIDTH + 1)),
            pl.BlockSpec((1, q, LANES), lambda b, c: (b, chunk(c), 0)),
        ]

    fw = lambda c: c
    rv = lambda c: nc - 1 - c
    dskip = jnp.repeat(d_skip, SSM_HEAD_DIM).reshape(1, SSM_INNER)
    out = jax.ShapeDtypeStruct((nb, length, SSM_INNER), BF16)
    return pl.pallas_call(
        _ssd_body,
        out_shape=(out, out),
        grid=(nb, nc),
        in_specs=specs(fw) + specs(rv) + [
            pl.BlockSpec((1, LANES), fixed),
            pl.BlockSpec((1, LANES), fixed),
            pl.BlockSpec((LANES, SSM_INNER), fixed),
            pl.BlockSpec((LANES, SSM_INNER), fixed),
            pl.BlockSpec((1, SSM_INNER), fixed),
        ],
        out_specs=(pl.BlockSpec((1, q, SSM_INNER), lambda b, c: (b, fw(c), 0)),
                   pl.BlockSpec((1, q, SSM_INNER), lambda b, c: (b, rv(c), 0))),
        scratch_shapes=[state, state],
        compiler_params=_cparams("parallel", "arbitrary"),
        name="ssd_scan",
    )(xbc, xbc, xbc, dt, xbc, xbc, xbc, dt, dtb, alog, head_expand(0), head_expand(1), dskip)


def _ab_out_body(x_ref, yf_ref, sf_ref, sr_ref, z_ref, gn_ref, w_ref, o_ref):
    gw = SSM_GROUP_WIDTH
    acc = x_ref[...] + _dot(yf_ref[...], w_ref[0:D_MODEL, :])
    for g in range(SSM_GROUPS):
        cols = slice(g * gw, (g + 1) * gw)
        y = (sf_ref[:, cols].astype(F32) + sr_ref[:, cols].astype(F32)) * _silu(z_ref[:, cols].astype(F32))
        yn = _rms(y, gn_ref[:, cols]).astype(BF16)
        acc = acc + _dot(yn, w_ref[D_MODEL + g * gw:D_MODEL + (g + 1) * gw, :])
    o_ref[...] = acc


def _ab_out(x, y_four, y_fw, y_bw, z, gate_norm, w_out):
    t, d = x.shape
    tm = _row_tile(t, 512)
    row = lambda i: (i, 0)
    fixed = lambda i: (0, 0)
    return pl.pallas_call(
        _ab_out_body,
        out_shape=jax.ShapeDtypeStruct((t, d), F32),
        grid=(t // tm,),
        in_specs=[
            pl.BlockSpec((tm, d), row),
            pl.BlockSpec((tm, D_MODEL), row),
            pl.BlockSpec((tm, SSM_INNER), row),
            pl.BlockSpec((tm, SSM_INNER), row),
            pl.BlockSpec((tm, SSM_INNER), row),
            pl.BlockSpec((1, SSM_INNER), fixed),
            pl.BlockSpec(w_out.shape, fixed),
        ],
        out_specs=pl.BlockSpec((tm, d), row),
        compiler_params=_cparams("parallel"),
        name="ab_out",
    )(x, y_four, y_fw, y_bw, z, gate_norm.reshape(1, SSM_INNER), w_out)


def _rope_tables(length):
    inv = ROPE_THETA ** (-jnp.arange(0, ROT_DIM, 2, dtype=F32) / ROT_DIM)
    ang = jnp.arange(length, dtype=F32)[:, None] * inv[None, :]
    cos, sin = jnp.cos(ang), jnp.sin(ang)
    half = ROT_DIM // 2
    pad = DIFF_HEAD_DIM - ROT_DIM
    ones = jnp.ones((length, pad), F32)
    zeros = jnp.zeros((length, pad), F32)
    zh = jnp.zeros((length, half), F32)
    c_self = jnp.concatenate([cos, cos, ones], axis=1)
    c_up = jnp.concatenate([-sin, zh, zeros], axis=1)
    c_down = jnp.concatenate([zh, sin, zeros], axis=1)
    rep = LANES // DIFF_HEAD_DIM
    return tuple(jnp.tile(tb, (1, rep)) for tb in (c_self, c_up, c_down))


def _pooled(length, ext, w_ref, s_ref):
    hr = F32_SUBLANE_TILE
    tm = ext.shape[0] - 2 * hr
    gd = POOL_GROUP_DIM
    pos = pl.program_id(1) * tm + lax.broadcasted_iota(jnp.int32, (tm, 1), 0)
    win = ext + _shift_rows(ext, -1)
    outs = []
    for g, w in enumerate(POOL_WINDOWS):
        if g > 0:
            win = win[:, gd:]
            win = _shift_rows(win, -(w // 4)) + _shift_rows(win, w // 4)
        lo = jnp.maximum(pos - w // 2, 0)
        hi = jnp.minimum(pos + w // 2 - 1, length - 1)
        mean = win[hr:hr + tm, :gd] / (hi - lo + 1).astype(F32)
        centred = (mean - ext[hr:hr + tm, g * gd:(g + 1) * gd]).astype(BF16)
        outs.append(_dot(centred, w_ref[g]))
    return jnp.concatenate(outs, axis=1) * s_ref[...]


def _cd_in_body(length, xp_ref, x_ref, xq_ref, g_ref, w_ref, ones_ref, qg_ref, kg_ref, cs_ref, cu_ref, cd_ref,
                pw_ref, ps_ref, yp_ref, q_ref, k_ref, v_ref):
    hr = F32_SUBLANE_TILE
    i = pl.program_id(1)
    keep_prev = (i > 0).astype(F32)
    keep_next = (i < pl.num_programs(1) - 1).astype(F32)
    xn = _rms(x_ref[0], g_ref[...]).astype(BF16)
    halo = _rms(jnp.concatenate([xp_ref[0], xq_ref[0]], axis=0), g_ref[...]).astype(BF16)
    d = D_MODEL
    half = ROT_DIM // 2
    rep = d // LANES
    c_self = jnp.tile(cs_ref[...], (1, rep))
    c_up = jnp.tile(cu_ref[...], (1, rep))
    c_down = jnp.tile(cd_ref[...], (1, rep))

    def qk_norm_rope(t, gain):
        sq = (t * t).astype(BF16)
        ms = jnp.concatenate([_dot(sq[:, c:c + MXU_DIM], ones_ref[...]) for c in range(0, d, MXU_DIM)], axis=1)
        t = t * lax.rsqrt(ms * (1.0 / DIFF_HEAD_DIM) + EPS) * gain
        return t * c_self + pltpu.roll(t, d - half, 1) * c_up + pltpu.roll(t, half, 1) * c_down

    edge = _dot(halo, w_ref[:, 0:d])
    ext = jnp.concatenate([edge[0:hr] * keep_prev, _dot(xn, w_ref[:, 0:d]), edge[hr:2 * hr] * keep_next], axis=0)
    yp_ref[0] = _pooled(length, ext, pw_ref, ps_ref).astype(BF16)
    q = qk_norm_rope(_dot(xn, w_ref[:, d:2 * d]), qg_ref[...])
    q_ref[0] = (q * (LOG2E * DIFF_HEAD_DIM ** -0.5)).astype(BF16)
    k_ref[0] = qk_norm_rope(_dot(xn, w_ref[:, 2 * d:3 * d]), kg_ref[...]).astype(BF16)
    v_ref[0] = _dot(xn, w_ref[:, 3 * d:4 * d]).astype(BF16)


def _cd_in(x, g, w_in, q_norm, k_norm, pool_w, pool_scale):
    nb, length, d = x.shape
    tm = _row_tile(length, 512)
    hr = F32_SUBLANE_TILE
    per = tm // hr
    last = length // hr - 1
    fixed = lambda b, i: (0, 0)
    tile = lambda b, i: (b, i, 0)
    pos = lambda b, i: (i, 0)
    ones_blk = jnp.asarray(np.kron(np.eye(MXU_DIM // DIFF_HEAD_DIM), np.ones((DIFF_HEAD_DIM, DIFF_HEAD_DIM))), BF16)
    qg = jnp.tile(q_norm, d // DIFF_HEAD_DIM).reshape(1, d)
    kg = jnp.tile(k_norm, d // DIFF_HEAD_DIM).reshape(1, d)
    out = jax.ShapeDtypeStruct((nb, length, d), BF16)
    return pl.pallas_call(
        functools.partial(_cd_in_body, length),
        out_shape=(out, out, out, out),
        grid=(nb, length // tm),
        in_specs=[
            pl.BlockSpec((1, hr, d), lambda b, i: (b, jnp.maximum(i * per - 1, 0), 0)),
            pl.BlockSpec((1, tm, d), tile),
            pl.BlockSpec((1, hr, d), lambda b, i: (b, jnp.minimum((i + 1) * per, last), 0)),
            pl.BlockSpec((1, d), fixed),
            pl.BlockSpec(w_in.shape, fixed),
            pl.BlockSpec((MXU_DIM, MXU_DIM), fixed),
            pl.BlockSpec((1, d), fixed),
            pl.BlockSpec((1, d), fixed),
            pl.BlockSpec((tm, LANES),pos),
            pl.BlockSpec((tm, LANES),pos),
            pl.BlockSpec((tm, LANES),pos),
            pl.BlockSpec(pool_w.shape, lambda b, i: (0, 0, 0)),
            pl.BlockSpec((1, d), fixed),
        ],
        out_specs=tuple(pl.BlockSpec((1, tm, d), tile) for _ in range(4)),
        compiler_params=_cparams("parallel", "parallel"),
        name="cd_in",
    )(x, x, x, g.reshape(1, d), w_in, ones_blk, qg, kg, *_rope_tables(length), pool_w, pool_scale.reshape(1, d))


ATTN_Q_TILE = 2048
ATTN_KV_TILE = 4096
ATTN_COL_BLOCK = 1024
ATTN_KEY_BLOCK = 2048
ATTN_SUM_ROWS = 16
LOG2E = math.log2(math.e)


def _diff_attn_body(lambda_init, q_ref, k_ref, v_ref, lam_ref, sub_ref, o_ref, qs_ref, m_ref, acc_ref, s_ref):
    kv = pl.program_id(3)
    tq = q_ref.shape[1]
    tk = k_ref.shape[1]
    vd = DIFF_V_DIM

    @pl.when(kv == 0)
    def _():
        q = q_ref[0]
        lane = lax.broadcasted_iota(jnp.int32, q.shape, 1)
        zero = jnp.zeros_like(q)
        qs_ref[0:tq] = jnp.where(lane < DIFF_HEAD_DIM, q, zero)
        qs_ref[tq:2 * tq] = jnp.where(lane >= DIFF_HEAD_DIM, q, zero)
        m_ref[...] = jnp.full_like(m_ref, -jnp.inf)
        acc_ref[...] = jnp.zeros_like(acc_ref)

    vt = jnp.concatenate([v_ref[0].T, jnp.ones((ATTN_SUM_ROWS, tk), BF16)], axis=0)
    cb = s_ref.shape[2]
    kb = math.gcd(tk, ATTN_KEY_BLOCK)
    blocks = [slice(c, c + cb) for c in range(0, 2 * tq, cb)]

    def scores(n):
        top = None
        for r in range(0, tk, kb):
            part = _dot_nt(k_ref[0, r:r + kb], qs_ref[blocks[n]])
            s_ref[n % 2, r:r + kb] = part
            part = jnp.max(part, axis=0, keepdims=True)
            top = part if top is None else jnp.maximum(top, part)
        return top

    top_next = scores(0)
    for n, cols in enumerate(blocks):
        top = top_next
        if n + 1 < len(blocks):
            top_next = scores(n + 1)
        m_prev = m_ref[:, cols]
        m_next = jnp.maximum(m_prev, top)
        alpha = jnp.exp2(m_prev - m_next)
        acc = acc_ref[:, cols] * alpha[0:1]
        for r in range(0, tk, kb):
            p = jnp.exp2(s_ref[n % 2, r:r + kb] - m_next[0:1]).astype(BF16)
            acc = acc + _dot(vt[:, r:r + kb], p)
        acc_ref[:, cols] = acc
        m_ref[:, cols] = m_next

    @pl.when(kv == pl.num_programs(3) - 1)
    def _():
        acc = acc_ref[...]
        o = acc[0:vd] / acc[vd:vd + 1]
        lv = lam_ref[...]
        lam = (jnp.exp(jnp.sum(lv[0:1] * lv[1:2], axis=-1, keepdims=True))
               - jnp.exp(jnp.sum(lv[2:3] * lv[3:4], axis=-1, keepdims=True)) + lambda_init)
        diff = (o[:, 0:tq] - lam * o[:, tq:2 * tq]).T
        o_ref[0] = (_rms(diff, sub_ref[...]) * (1.0 - lambda_init)).astype(o_ref.dtype)


def _diff_attn(q, k, v, lam_vecs, sub_norm, lambda_init):
    nb, length, d = q.shape
    tq = _row_tile(length, ATTN_Q_TILE)
    tk = _row_tile(length, ATTN_KV_TILE)
    hw = 2 * DIFF_HEAD_DIM
    assert hw == LANES and DIFF_V_DIM == LANES
    return pl.pallas_call(
        functools.partial(_diff_attn_body, lambda_init),
        out_shape=jax.ShapeDtypeStruct((nb, length, d), BF16),
        grid=(nb, DIFF_HEADS, length // tq, length // tk),
        in_specs=[
            pl.BlockSpec((1, tq, hw), lambda b, h, i, j: (b, i, h)),
            pl.BlockSpec((1, tk, hw), lambda b, h, i, j: (b, j, h)),
            pl.BlockSpec((1, tk, DIFF_V_DIM), lambda b, h, i, j: (b, j, h)),
            pl.BlockSpec((4, DIFF_HEAD_DIM), lambda b, h, i, j: (0, 0)),
            pl.BlockSpec((1, DIFF_V_DIM), lambda b, h, i, j: (0, 0)),
        ],
        out_specs=pl.BlockSpec((1, tq, DIFF_V_DIM), lambda b, h, i, j: (b, i, h)),
        scratch_shapes=[
            pltpu.VMEM((2 * tq, hw), BF16),
            pltpu.VMEM((F32_SUBLANE_TILE, 2 * tq), F32),
            pltpu.VMEM((DIFF_V_DIM + ATTN_SUM_ROWS, 2 * tq), F32),
            pltpu.VMEM((2, tk, min(ATTN_COL_BLOCK, 2 * tq)), F32),
        ],
        compiler_params=_cparams("parallel", "parallel", "parallel", "arbitrary"),
        name="diff_attn",
    )(q, k, v, lam_vecs, sub_norm.reshape(1, DIFF_V_DIM))


def _cd_out_body(x_ref, yp_ref, o_ref_in, w_ref, o_ref):
    d = D_MODEL
    o_ref[...] = x_ref[...] + _dot(yp_ref[...], w_ref[0:d, :]) + _dot(o_ref_in[...], w_ref[d:2 * d, :])


def _cd_out(x, y_pool, o, w_out):
    t, d = x.shape
    tm = _row_tile(t, 512)
    row = lambda i: (i, 0)
    return pl.pallas_call(
        _cd_out_body,
        out_shape=jax.ShapeDtypeStruct((t, d), F32),
        grid=(t // tm,),
        in_specs=[pl.BlockSpec((tm, d), row), pl.BlockSpec((tm, d), row), pl.BlockSpec((tm, d), row),
                  pl.BlockSpec(w_out.shape, lambda i: (0, 0))],
        out_specs=pl.BlockSpec((tm, d), row),
        compiler_params=_cparams("parallel"),
        name="cd_out",
    )(x, y_pool, o, w_out)


def _mem_kv_body(m_ref, g_ref, w_ref, kg_ref, k_ref, v_ref):
    d = D_MODEL
    mn = _rms(m_ref[0], g_ref[...]).astype(BF16)
    k = _dot(mn, w_ref[:, 0:d])
    hd = CROSS_HEAD_DIM
    k_ref[0] = jnp.concatenate(
        [_rms(k[:, h * hd:(h + 1) * hd], kg_ref[...]) for h in range(CROSS_HEADS)], axis=1).astype(BF16)
    v_ref[0] = _dot(mn, w_ref[:, d:2 * d]).astype(BF16)


def _mem_kv(mem, g, w_kv, k_norm):
    nb, n_mem, d = mem.shape
    fixed = lambda b: (0, 0)
    out = jax.ShapeDtypeStruct((nb, n_mem, d), BF16)
    blk = pl.BlockSpec((1, n_mem, d), lambda b: (b, 0, 0))
    return pl.pallas_call(
        _mem_kv_body,
        out_shape=(out, out),
        grid=(nb,),
        in_specs=[blk, pl.BlockSpec((1, d), fixed), pl.BlockSpec(w_kv.shape, fixed),
                  pl.BlockSpec((1, CROSS_HEAD_DIM), fixed)],
        out_specs=(blk, blk),
        compiler_params=_cparams("parallel"),
        name="cross_mem_kv",
    )(mem, g.reshape(1, d), w_kv, k_norm.reshape(1, CROSS_HEAD_DIM))


def _cross_body(x_ref, g_ref, wq_ref, qg_ref, k_ref, v_ref, wo_ref, o_ref):
    hd = CROSS_HEAD_DIM
    x = x_ref[0]
    q = _dot(_rms(x, g_ref[...]).astype(BF16), wq_ref[...])
    heads = []
    for h in range(CROSS_HEADS):
        cols = slice(h * hd, (h + 1) * hd)
        qh = (_rms(q[:, cols], qg_ref[...]) * (hd ** -0.5)).astype(BF16)
        s = _dot_nt(qh, k_ref[0, :, cols])
        p = jnp.exp(s - jnp.max(s, axis=-1, keepdims=True))
        p = p / jnp.sum(p, axis=-1, keepdims=True)
        heads.append(_dot(p.astype(BF16), v_ref[0, :, cols]).astype(BF16))
    o_ref[0] = x + _dot(jnp.concatenate(heads, axis=1), wo_ref[...])


def _cross(x, g, w_q, q_norm, k, v, w_o):
    nb, length, d = x.shape
    n_mem = k.shape[1]
    tm = _row_tile(length, 512)
    fixed = lambda b, i: (0, 0)
    tile = lambda b, i: (b, i, 0)
    per_batch = lambda b, i: (b, 0, 0)
    return pl.pallas_call(
        _cross_body,
        out_shape=jax.ShapeDtypeStruct((nb, length, d), F32),
        grid=(nb, length // tm),
        in_specs=[
            pl.BlockSpec((1, tm, d), tile),
            pl.BlockSpec((1, d), fixed),
            pl.BlockSpec((d, d), fixed),
            pl.BlockSpec((1, CROSS_HEAD_DIM), fixed),
            pl.BlockSpec((1, n_mem, d), per_batch),
            pl.BlockSpec((1, n_mem, d), per_batch),
            pl.BlockSpec((d, d), fixed),
        ],
        out_specs=pl.BlockSpec((1, tm, d), tile),
        compiler_params=_cparams("parallel", "parallel"),
        name="cross_attn",
    )(x, g.reshape(1, d), w_q, q_norm.reshape(1, CROSS_HEAD_DIM), k, v, w_o)


def _lambda_init(layer_idx):
    return 0.8 - 0.6 * math.exp(-0.3 * layer_idx)


def _mixer_ab(x, p, i):
    nb, length, d = x.shape
    t = nb * length
    w_in = p['ab_w_in'][i]
    n_main = D_MODEL + SSM_INNER + SSM_CONV_CH
    w_dt = jnp.pad(w_in[:, n_main:], ((0, 0), (0, LANES - 2 * SSM_HEADS)))
    uf, z, xbc, dt = _ab_in(x.reshape(t, d), p['mix_norm_l'], w_in[:, :n_main].astype(BF16), w_dt.astype(BF16),
                            p['ab_conv_w'][i], p['ab_conv_b'][i], length)
    y_four = _fourier(uf, nb, length)
    y_fw, y_bw = _ssd(xbc.reshape(nb, length, SSM_CONV_CH), dt.reshape(nb, length, LANES), p['ab_dt_bias'][i],
                      p['ab_a_log'][i], p['ab_d_skip'][i])
    out = _ab_out(x.reshape(t, d), y_four, y_fw.reshape(t, SSM_INNER), y_bw.reshape(t, SSM_INNER), z,
                  p['ab_gate_norm'][i], p['ab_w_out'][i].astype(BF16))
    return out.reshape(nb, length, d)


def _mixer_cd(x, p, i, layer_idx):
    nb, length, d = x.shape
    t = nb * length
    y_pool, q, k, v = _cd_in(x, p['mix_norm_l'], p['cd_w_in'][i].astype(BF16), p['cd_q_norm'][i], p['cd_k_norm'][i],
                             p['cd_pool_w'][i].astype(BF16), p['cd_pool_scale'][i])
    lam_vecs = jnp.stack([p['cd_lambda_q1'][i], p['cd_lambda_k1'][i], p['cd_lambda_q2'][i], p['cd_lambda_k2'][i]])
    o = _diff_attn(q, k, v, lam_vecs, p['cd_sub_norm'][i], _lambda_init(layer_idx))
    out = _cd_out(x.reshape(t, d), y_pool.reshape(t, d), o.reshape(t, d), p['cd_w_out'][i].astype(BF16))
    return out.reshape(nb, length, d)


def _trunk(x, mem, p, depth):
    nb, length, d = x.shape
    t = nb * length
    for l in range(depth):
        x = _ffn(x.reshape(t, d), p['ffn1_norm'][l], p['ffn1_w_gate'][l].astype(BF16),
                 p['ffn1_w_up'][l].astype(BF16), p['ffn1_w_down'][l].astype(BF16)).reshape(nb, length, d)
        pl_ = dict(p, mix_norm_l=p['mix_norm'][l])
        if l % 2 == 0:
            x = _mixer_ab(x, pl_, l // 2)
        else:
            x = _mixer_cd(x, pl_, l // 2, l)
        mk, mv = _mem_kv(mem, p['cross_mem_norm'][l], p['cross_w_kv'][l].astype(BF16), p['cross_k_norm'][l])
        x = _cross(x, p['cross_norm'][l], p['cross_w_q'][l].astype(BF16), p['cross_q_norm'][l], mk, mv,
                   p['cross_w_o'][l].astype(BF16))
        x = _ffn(x.reshape(t, d), p['ffn2_norm'][l], p['ffn2_w_gate'][l].astype(BF16),
                 p['ffn2_w_up'][l].astype(BF16), p['ffn2_w_down'][l].astype(BF16)).reshape(nb, length, d)
    return x


def kernel(x_prompt, x_sample, mem_prompt, mem_sample, ffn1_norm, ffn1_w_gate, ffn1_w_up, ffn1_w_down, mix_norm, ab_w_in, ab_conv_w, ab_conv_b, ab_dt_bias, ab_a_log, ab_d_skip, ab_gate_norm, ab_w_out, cd_w_in, cd_pool_w, cd_pool_scale, cd_q_norm, cd_k_norm, cd_lambda_q1, cd_lambda_k1, cd_lambda_q2, cd_lambda_k2, cd_sub_norm, cd_w_out, cross_norm, cross_mem_norm, cross_w_q, cross_w_kv, cross_q_norm, cross_k_norm, cross_w_o, ffn2_norm, ffn2_w_gate, ffn2_w_up, ffn2_w_down):
    p = {
        'ffn1_norm': ffn1_norm, 'ffn1_w_gate': ffn1_w_gate, 'ffn1_w_up': ffn1_w_up, 'ffn1_w_down': ffn1_w_down,
        'mix_norm': mix_norm,
        'ab_w_in': ab_w_in, 'ab_conv_w': ab_conv_w, 'ab_conv_b': ab_conv_b, 'ab_dt_bias': ab_dt_bias,
        'ab_a_log': ab_a_log, 'ab_d_skip': ab_d_skip, 'ab_gate_norm': ab_gate_norm, 'ab_w_out': ab_w_out,
        'cd_w_in': cd_w_in, 'cd_pool_w': cd_pool_w, 'cd_pool_scale': cd_pool_scale, 'cd_q_norm': cd_q_norm,
        'cd_k_norm': cd_k_norm, 'cd_lambda_q1': cd_lambda_q1, 'cd_lambda_k1': cd_lambda_k1,
        'cd_lambda_q2': cd_lambda_q2, 'cd_lambda_k2': cd_lambda_k2, 'cd_sub_norm': cd_sub_norm,
        'cd_w_out': cd_w_out,
        'cross_norm': cross_norm, 'cross_mem_norm': cross_mem_norm, 'cross_w_q': cross_w_q,
        'cross_w_kv': cross_w_kv, 'cross_q_norm': cross_q_norm, 'cross_k_norm': cross_k_norm,
        'cross_w_o': cross_w_o,
        'ffn2_norm': ffn2_norm, 'ffn2_w_gate': ffn2_w_gate, 'ffn2_w_up': ffn2_w_up, 'ffn2_w_down': ffn2_w_down,
    }
    depth = ffn1_norm.shape[0]
    return (_trunk(x_prompt, mem_prompt, p, depth), _trunk(x_sample, mem_sample, p, depth))
```

```python
import functools
import math

import numpy as np
import jax
import jax.numpy as jnp
from jax import lax
from jax.experimental import pallas as pl
from jax.experimental.pallas import tpu as pltpu

F32 = jnp.float32
BF16 = jnp.bfloat16
EPS = 1e-6

VMEM_LIMIT_BYTES = 56 * 1024 * 1024
LANES = 128
MXU_DIM = 256
F32_SUBLANE_TILE = 8
BF16_SUBLANE_TILE = 16

D_MODEL = 1024
FNET_HEADS = 4
FNET_HEAD_DIM = 256
FFT_INNER = 64
SSM_HEADS = 32
SSM_HEAD_DIM = 64
SSM_STATE = 128
SSM_GROUPS = 4
SSM_HEADS_PER_GROUP = SSM_HEADS // SSM_GROUPS
SSM_INNER = SSM_HEADS * SSM_HEAD_DIM
SSM_GROUP_WIDTH = SSM_INNER // SSM_GROUPS
SSM_BC_WIDTH = SSM_GROUPS * SSM_STATE
SSM_CONV_CH = SSM_INNER + 2 * SSM_BC_WIDTH
SSM_CONV = 5
SSM_CHUNK = 128
POOL_WINDOWS = (2, 4, 8, 16)
POOL_GROUP_DIM = 256
DIFF_HEADS = 8
DIFF_HEAD_DIM = 64
DIFF_V_DIM = 128
ROT_DIM = 16
ROPE_THETA = 500000.0
CROSS_HEADS = 4
CROSS_HEAD_DIM = 256


def _cparams(*semantics):
    return pltpu.CompilerParams(dimension_semantics=semantics, vmem_limit_bytes=VMEM_LIMIT_BYTES)


def _dot(a, b):
    return jnp.dot(a, b, preferred_element_type=F32)


def _dot_nt(a, b):
    return lax.dot_general(a, b, (((1,), (1,)), ((), ())), preferred_element_type=F32)


def _dot_tn(a, b):
    return lax.dot_general(a, b, (((0,), (0,)), ((), ())), preferred_element_type=F32)


def _rms(x, g):
    return x * lax.rsqrt(jnp.mean(x * x, axis=-1, keepdims=True) + EPS) * g


def _silu(x):
    return x * jax.nn.sigmoid(x)


def _row_tile(n, want):
    t = min(n, want)
    assert n % t == 0, (n, t)
    return t


FFN_ROW_TILE = 512


def _ffn_body(x_ref, g_ref, wg_ref, wu_ref, wd_ref, o_ref):
    x = x_ref[...]
    xn = _rms(x, g_ref[...]).astype(BF16)
    gate = _dot(xn, wg_ref[...])
    up = _dot(xn, wu_ref[...])
    h = (_silu(gate) * up).astype(BF16)
    o_ref[...] = x + 0.5 * _dot(h, wd_ref[...])


def _ffn(x, g, wg, wu, wd):
    t, d = x.shape
    f = wg.shape[1]
    tm = _row_tile(t, FFN_ROW_TILE)
    fixed = lambda i: (0, 0)
    return pl.pallas_call(
        _ffn_body,
        out_shape=jax.ShapeDtypeStruct((t, d), F32),
        grid=(t // tm,),
        in_specs=[
            pl.BlockSpec((tm, d), lambda i: (i, 0)),
            pl.BlockSpec((1, d), fixed),
            pl.BlockSpec((d, f), fixed),
            pl.BlockSpec((d, f), fixed),
            pl.BlockSpec((f, d), fixed),
        ],
        out_specs=pl.BlockSpec((tm, d), lambda i: (i, 0)),
        compiler_params=_cparams("parallel"),
        name="ffn",
    )(x, g.reshape(1, d), wg, wu, wd)


CONV_LANES = 512


def _ab_in_body(tiles_per_seq, xp_ref, x_ref, xq_ref, g_ref, w_ref, wdt_ref, cw_ref, cb_ref,
                uf_ref, z_ref, xbc_ref, dt_ref, xn_ref, ext_ref):
    tm = x_ref.shape[0]
    hr = F32_SUBLANE_TILE
    i = pl.program_id(0)
    keep_prev = (i % tiles_per_seq != 0).astype(F32)
    keep_next = (i % tiles_per_seq != tiles_per_seq - 1).astype(F32)
    xn_ref[0:tm] = _rms(x_ref[...], g_ref[...]).astype(BF16)
    xn_ref[tm:tm + 2 * hr] = _rms(jnp.concatenate([xp_ref[...], xq_ref[...]], axis=0), g_ref[...]).astype(BF16)
    half = SSM_CONV // 2
    col = uf_ref.shape[1] + z_ref.shape[1]
    for n, c in enumerate(range(0, SSM_CONV_CH, CONV_LANES)):
        cols = slice(c, c + CONV_LANES)
        pre = _dot(xn_ref[...], w_ref[:, col + c:col + c + CONV_LANES])
        ext = ext_ref.at[n % ext_ref.shape[0]]
        ext[0:hr] = pre[tm:tm + hr] * keep_prev
        ext[hr:hr + tm] = pre[0:tm]
        ext[hr + tm:2 * hr + tm] = pre[tm + hr:tm + 2 * hr] * keep_next
        acc = cb_ref[:, cols] + ext[hr - half:hr - half + tm] * cw_ref[0:1, cols]
        for j in range(1, SSM_CONV):
            acc = acc + ext[hr - half + j:hr - half + j + tm] * cw_ref[j:j + 1, cols]
        xbc_ref[:, cols] = _silu(acc).astype(xbc_ref.dtype)
    dt_ref[...] = _dot(xn_ref[0:tm], wdt_ref[...])
    col = 0
    for ref in (uf_ref, z_ref):
        width = ref.shape[1]
        for c in range(0, width, 1024):
            ref[:, c:c + 1024] = _dot(xn_ref[0:tm], w_ref[:, col + c:col + c + 1024]).astype(ref.dtype)
        col += width


def _ab_in(x, g, w_main, w_dt, conv_w, conv_b, length):
    t, d = x.shape
    tm = _row_tile(length, 512)
    hr = F32_SUBLANE_TILE
    per = tm // hr
    last = t // hr - 1
    n_main = w_main.shape[1]
    row = lambda i: (i, 0)
    fixed = lambda i: (0, 0)
    return pl.pallas_call(
        functools.partial(_ab_in_body, length // tm),
        out_shape=(
            jax.ShapeDtypeStruct((t, D_MODEL), BF16),
            jax.ShapeDtypeStruct((t, SSM_INNER), BF16),
            jax.ShapeDtypeStruct((t, SSM_CONV_CH), BF16),
            jax.ShapeDtypeStruct((t, LANES), F32),
        ),
        grid=(t // tm,),
        in_specs=[
            pl.BlockSpec((hr, d), lambda i: (jnp.maximum(i * per - 1, 0), 0)),
            pl.BlockSpec((tm, d), row),
            pl.BlockSpec((hr, d), lambda i: (jnp.minimum((i + 1) * per, last), 0)),
            pl.BlockSpec((1, d), fixed),
            pl.BlockSpec((d, n_main), fixed),
            pl.BlockSpec((d, LANES), fixed),
            pl.BlockSpec((SSM_CONV, SSM_CONV_CH), fixed),
            pl.BlockSpec((1, SSM_CONV_CH), fixed),
        ],
        out_specs=(
            pl.BlockSpec((tm, D_MODEL), row),
            pl.BlockSpec((tm, SSM_INNER), row),
            pl.BlockSpec((tm, SSM_CONV_CH), row),
            pl.BlockSpec((tm, LANES),row),
        ),
        scratch_shapes=[pltpu.VMEM((tm + 2 * hr, d), BF16), pltpu.VMEM((2, tm + 2 * hr, CONV_LANES), F32)],
        compiler_params=_cparams("parallel"),
        name="ab_in",
    )(x, x, x, g.reshape(1, d), w_main, w_dt, conv_w, conv_b.reshape(1, SSM_CONV_CH))


def _dft_tables(length):
    l2 = FFT_INNER
    l1 = length // l2
    assert l1 * l2 == length
    k1 = np.arange(l1)
    ang1 = 2.0 * np.pi * ((k1[:, None] * k1[None, :]) % l1) / l1
    f1 = np.concatenate([np.cos(ang1), -np.sin(ang1)], axis=0)
    k2 = np.arange(l2)
    n2 = np.arange(l2)
    kk = k1[:, None, None] + l1 * k2[None, :, None]
    ang2 = 2.0 * np.pi * ((kk * n2[None, None, :]) % length) / length
    mr, mi = np.cos(ang2), -np.sin(ang2)
    m2 = np.concatenate([np.concatenate([mr, -mi], axis=2),
                         np.concatenate([mi, mr], axis=2)], axis=1)
    c = np.arange(FNET_HEAD_DIM)
    angc = 2.0 * np.pi * ((c[:, None] * c[None, :]) % FNET_HEAD_DIM) / FNET_HEAD_DIM
    fc = np.concatenate([np.cos(angc), np.sin(angc)], axis=0)
    return (jnp.asarray(f1, dtype=BF16), jnp.asarray(m2, dtype=BF16), jnp.asarray(fc, dtype=BF16))


def _fft1_body(f_ref, x_ref, o_ref):
    l1, tn, c = x_ref.shape[1:]
    x = x_ref[0].astype(F32).reshape(l1, tn * c).astype(BF16)
    t = _dot(f_ref[...], x)
    o_ref[0] = t.reshape(2, l1, tn, c).astype(o_ref.dtype)


FFT_K1_PER_STEP = 16


def _fft2_body(scale, m_ref, fc_ref, t_ref, o_ref):
    nk, l2, c = t_ref.shape[2:]
    fc = fc_ref[...]
    outs = []
    for j in range(nk):
        t = jnp.concatenate([t_ref[0, 0, j], t_ref[0, 1, j]], axis=0)
        y = _dot(m_ref[j], t)
        yr, yi = y[:l2].astype(BF16), y[l2:].astype(BF16)
        for h in range(FNET_HEADS):
            sl = slice(h * FNET_HEAD_DIM, (h + 1) * FNET_HEAD_DIM)
            outs.append(_dot(jnp.concatenate([yr[:, sl], yi[:, sl]], axis=1), fc))
    out = jnp.concatenate(outs, axis=1) * scale
    o_ref[0] = out.reshape(l2, nk, c).astype(o_ref.dtype)


def _fourier(uf, nb, length):
    c = D_MODEL
    l2 = FFT_INNER
    l1 = length // l2
    f1, m2, fc = _dft_tables(length)
    x4 = uf.reshape(nb, l1, l2, c)
    tn = math.gcd(l2, BF16_SUBLANE_TILE)
    t5 = pl.pallas_call(
        _fft1_body,
        out_shape=jax.ShapeDtypeStruct((nb, 2, l1, l2, c), BF16),
        grid=(nb, l2 // tn),
        in_specs=[pl.BlockSpec((2 * l1, l1), lambda b, j: (0, 0)),
                  pl.BlockSpec((1, l1, tn, c), lambda b, j: (b, 0, j, 0))],
        out_specs=pl.BlockSpec((1, 2, l1, tn, c), lambda b, j: (b, 0, 0, j, 0)),
        compiler_params=_cparams("parallel", "parallel"),
        name="fft_stage1",
    )(f1, x4)
    nk = math.gcd(l1, FFT_K1_PER_STEP)
    scale = 1.0 / math.sqrt(length * FNET_HEAD_DIM)
    y = pl.pallas_call(
        functools.partial(_fft2_body, scale),
        out_shape=jax.ShapeDtypeStruct((nb, l2, l1, c), BF16),
        grid=(nb, l1 // nk),
        in_specs=[pl.BlockSpec((nk, 2 * l2, 2 * l2), lambda b, k: (k, 0, 0)),
                  pl.BlockSpec((2 * FNET_HEAD_DIM, FNET_HEAD_DIM), lambda b, k: (0, 0)),
                  pl.BlockSpec((1, 2, nk, l2, c), lambda b, k: (b, 0, k, 0, 0))],
        out_specs=pl.BlockSpec((1, l2, nk, c), lambda b, k: (b, 0, k, 0)),
        compiler_params=_cparams("parallel", "parallel"),
        name="fft_stage2",
    )(m2, fc, t5)
    return y.reshape(nb * length, c)


def _shift_rows(x, k):
    n = x.shape[0]
    return x if k % n == 0 else pltpu.roll(x, (-k) % n, 0)


def _split_bf16(v, pieces):
    out = []
    for _ in range(pieces):
        p = v.astype(BF16)
        out.append(p)
        v = v - p.astype(F32)
    return out


def _ssd_chunk(reverse, rows, x_ref, b_ref, c_ref, dt_ref, dtb_ref, alog_ref, expand_ref, state_ref):
    q = SSM_CHUNK
    n = SSM_STATE
    assert q == LANES and n == LANES
    ch0 = SSM_HEADS * (1 if reverse else 0)
    x_b = x_ref[0, rows]
    raw = dt_ref[0, rows] + dtb_ref[...]
    e = jnp.exp(-jnp.abs(raw))
    u = 1.0 + e
    um1 = u - 1.0
    dt = jnp.maximum(raw, 0.0) + jnp.where(um1 == 0.0, e, jnp.log(u) * (e / jnp.where(um1 == 0.0, 1.0, um1)))
    da = dt * (-LOG2E * jnp.exp(alog_ref[...]))
    row = lax.broadcasted_iota(jnp.int32, (q, q), 0)
    col = lax.broadcasted_iota(jnp.int32, (q, q), 1)
    mask = (col >= row) if reverse else (col <= row)
    tri = jnp.where(mask, 1.0, 0.0).astype(BF16)
    acum = sum(_dot(tri, p) for p in _split_bf16(da, 3))
    src_t = (acum - jnp.log2(dt)).T
    total = acum[0:1] if reverse else acum[q - 1:q]
    expand = expand_ref[...]
    step_w = _dot((dt * jnp.exp2(total - acum)).astype(BF16), expand)
    xw = (x_b.astype(F32) * step_w).astype(BF16)
    e_total = jnp.exp2(jnp.broadcast_to(total, (F32_SUBLANE_TILE, LANES)))
    e_total = sum(_dot(p, expand) for p in _split_bf16(e_total, 3))[0:1]

    first_head = lax.broadcasted_iota(jnp.int32, (q, 2 * SSM_HEAD_DIM), 1) < SSM_HEAD_DIM
    ys = []
    for g in range(SSM_GROUPS):
        bg = b_ref[0, rows, g * n:(g + 1) * n]
        cg = c_ref[0, rows, g * n:(g + 1) * n]
        scores = _dot_nt(cg, bg)
        cg_f = cg.astype(F32)
        state = state_ref[g]
        state_b = state.astype(BF16)
        pairs = []
        for j in range(SSM_HEADS_PER_GROUP // 2):
            lhs = []
            for h in (2 * j, 2 * j + 1):
                ch = ch0 + g * SSM_HEADS_PER_GROUP + h
                a_l = jnp.broadcast_to(acum[:, ch:ch + 1], (q, q))
                decay_dt = jnp.exp2(jnp.where(mask, a_l - src_t[ch:ch + 1, :], -jnp.inf))
                s_h = (scores * decay_dt).astype(BF16)
                c_h = (cg_f * jnp.exp2(a_l)).astype(BF16)
                lhs.append(jnp.concatenate([s_h, c_h], axis=1))
            pw = 2 * SSM_HEAD_DIM
            lanes = slice(g * SSM_GROUP_WIDTH + pw * j, g * SSM_GROUP_WIDTH + pw * (j + 1))
            rhs = jnp.concatenate([x_b[:, lanes], state_b[:, pw * j:pw * (j + 1)]], axis=0)
            out = _dot(jnp.concatenate(lhs, axis=0), rhs)
            pairs.append(jnp.where(first_head, out[:q], out[q:]))
        ys.append(jnp.concatenate(pairs, axis=1))
        gcols = slice(g * SSM_GROUP_WIDTH, (g + 1) * SSM_GROUP_WIDTH)
        state_ref[g] = state * e_total[:, gcols] + _dot_tn(bg, xw[:, gcols])
    return jnp.concatenate(ys, axis=1)


SSD_CHUNKS_PER_STEP = 4


def _ssd_body(xf_ref, bf_ref, cf_ref, dtf_ref, xr_ref, br_ref, cr_ref, dtr_ref, dtb_ref, alog_ref,
              ef_ref, er_ref, dskip_ref, yf_ref, yr_ref, sf_ref, sr_ref):
    @pl.when(pl.program_id(1) == 0)
    def _():
        sf_ref[...] = jnp.zeros_like(sf_ref)
        sr_ref[...] = jnp.zeros_like(sr_ref)

    q = SSM_CHUNK
    per_step = xf_ref.shape[1] // q
    for j in range(per_step):
        rows = slice(j * q, (j + 1) * q)
        yf = _ssd_chunk(False, rows, xf_ref, bf_ref, cf_ref, dtf_ref, dtb_ref, alog_ref, ef_ref, sf_ref)
        yf_ref[0, rows] = yf.astype(yf_ref.dtype)
        rows = slice((per_step - 1 - j) * q, (per_step - j) * q)
        yr = _ssd_chunk(True, rows, xr_ref, br_ref, cr_ref, dtr_ref, dtb_ref, alog_ref, er_ref, sr_ref)
        yr_ref[0, rows] = (yr + dskip_ref[...] * xr_ref[0, rows].astype(F32)).astype(yr_ref.dtype)


def _ssd(xbc, dt, dt_bias, a_log, d_skip):
    nb, length, _ = xbc.shape
    q = SSM_CHUNK * math.gcd(length // SSM_CHUNK, SSD_CHUNKS_PER_STEP)
    nc = length // q
    fixed = lambda b, c: (0, 0)
    state = pltpu.VMEM((SSM_GROUPS, SSM_STATE, SSM_GROUP_WIDTH), F32)
    pad = LANES - 2 * SSM_HEADS
    dtb = jnp.pad(dt_bias.reshape(1, 2 * SSM_HEADS), ((0, 0), (0, pad)))
    alog = jnp.pad(a_log.reshape(1, 2 * SSM_HEADS), ((0, 0), (0, pad)))

    def head_expand(direction):
        e = np.zeros((LANES, SSM_INNER), np.float32)
        for h in range(SSM_HEADS):
            e[direction * SSM_HEADS + h, h * SSM_HEAD_DIM:(h + 1) * SSM_HEAD_DIM] = 1.0
        return jnp.asarray(e, BF16)

    def specs(chunk):
        return [
            pl.BlockSpec((1, q, SSM_INNER), lambda b, c: (b, chunk(c), 0)),
            pl.BlockSpec((1, q, SSM_BC_WIDTH), lambda b, c: (b, chunk(c), SSM_INNER // SSM_BC_WIDTH)),
            pl.BlockSpec((1, q, SSM_BC_WIDTH), lambda b, c: (b, chunk(c), SSM_INNER // SSM_BC_WIDTH + 1)),
            pl.BlockSpec((1, q, LANES), lambda b, c: (b, chunk(c), 0)),
        ]

    fw = lambda c: c
    rv = lambda c: nc - 1 - c
    dskip = jnp.repeat(d_skip, SSM_HEAD_DIM).reshape(1, SSM_INNER)
    out = jax.ShapeDtypeStruct((nb, length, SSM_INNER), BF16)
    return pl.pallas_call(
        _ssd_body,
        out_shape=(out, out),
        grid=(nb, nc),
        in_specs=specs(fw) + specs(rv) + [
            pl.BlockSpec((1, LANES), fixed),
            pl.BlockSpec((1, LANES), fixed),
            pl.BlockSpec((LANES, SSM_INNER), fixed),
            pl.BlockSpec((LANES, SSM_INNER), fixed),
            pl.BlockSpec((1, SSM_INNER), fixed),
        ],
        out_specs=(pl.BlockSpec((1, q, SSM_INNER), lambda b, c: (b, fw(c), 0)),
                   pl.BlockSpec((1, q, SSM_INNER), lambda b, c: (b, rv(c), 0))),
        scratch_shapes=[state, state],
        compiler_params=_cparams("parallel", "arbitrary"),
        name="ssd_scan",
    )(xbc, xbc, xbc, dt, xbc, xbc, xbc, dt, dtb, alog, head_expand(0), head_expand(1), dskip)


def _ab_out_body(x_ref, yf_ref, sf_ref, sr_ref, z_ref, gn_ref, w_ref, o_ref):
    gw = SSM_GROUP_WIDTH
    acc = x_ref[...] + _dot(yf_ref[...], w_ref[0:D_MODEL, :])
    for g in range(SSM_GROUPS):
        cols = slice(g * gw, (g + 1) * gw)
        y = (sf_ref[:, cols].astype(F32) + sr_ref[:, cols].astype(F32)) * _silu(z_ref[:, cols].astype(F32))
        yn = _rms(y, gn_ref[:, cols]).astype(BF16)
        acc = acc + _dot(yn, w_ref[D_MODEL + g * gw:D_MODEL + (g + 1) * gw, :])
    o_ref[...] = acc


def _ab_out(x, y_four, y_fw, y_bw, z, gate_norm, w_out):
    t, d = x.shape
    tm = _row_tile(t, 512)
    row = lambda i: (i, 0)
    fixed = lambda i: (0, 0)
    return pl.pallas_call(
        _ab_out_body,
        out_shape=jax.ShapeDtypeStruct((t, d), F32),
        grid=(t // tm,),
        in_specs=[
            pl.BlockSpec((tm, d), row),
            pl.BlockSpec((tm, D_MODEL), row),
            pl.BlockSpec((tm, SSM_INNER), row),
            pl.BlockSpec((tm, SSM_INNER), row),
            pl.BlockSpec((tm, SSM_INNER), row),
            pl.BlockSpec((1, SSM_INNER), fixed),
            pl.BlockSpec(w_out.shape, fixed),
        ],
        out_specs=pl.BlockSpec((tm, d), row),
        compiler_params=_cparams("parallel"),
        name="ab_out",
    )(x, y_four, y_fw, y_bw, z, gate_norm.reshape(1, SSM_INNER), w_out)


def _rope_tables(length):
    inv = ROPE_THETA ** (-jnp.arange(0, ROT_DIM, 2, dtype=F32) / ROT_DIM)
    ang = jnp.arange(length, dtype=F32)[:, None] * inv[None, :]
    cos, sin = jnp.cos(ang), jnp.sin(ang)
    half = ROT_DIM // 2
    pad = DIFF_HEAD_DIM - ROT_DIM
    ones = jnp.ones((length, pad), F32)
    zeros = jnp.zeros((length, pad), F32)
    zh = jnp.zeros((length, half), F32)
    c_self = jnp.concatenate([cos, cos, ones], axis=1)
    c_up = jnp.concatenate([-sin, zh, zeros], axis=1)
    c_down = jnp.concatenate([zh, sin, zeros], axis=1)
    rep = LANES // DIFF_HEAD_DIM
    return tuple(jnp.tile(tb, (1, rep)) for tb in (c_self, c_up, c_down))


def _pooled(length, ext, w_ref, s_ref):
    hr = F32_SUBLANE_TILE
    tm = ext.shape[0] - 2 * hr
    gd = POOL_GROUP_DIM
    pos = pl.program_id(1) * tm + lax.broadcasted_iota(jnp.int32, (tm, 1), 0)
    win = ext + _shift_rows(ext, -1)
    outs = []
    for g, w in enumerate(POOL_WINDOWS):
        if g > 0:
            win = win[:, gd:]
            win = _shift_rows(win, -(w // 4)) + _shift_rows(win, w // 4)
        lo = jnp.maximum(pos - w // 2, 0)
        hi = jnp.minimum(pos + w // 2 - 1, length - 1)
        mean = win[hr:hr + tm, :gd] / (hi - lo + 1).astype(F32)
        centred = (mean - ext[hr:hr + tm, g * gd:(g + 1) * gd]).astype(BF16)
        outs.append(_dot(centred, w_ref[g]))
    return jnp.concatenate(outs, axis=1) * s_ref[...]


def _cd_in_body(length, xp_ref, x_ref, xq_ref, g_ref, w_ref, ones_ref, qg_ref, kg_ref, cs_ref, cu_ref, cd_ref,
                pw_ref, ps_ref, yp_ref, q_ref, k_ref, v_ref):
    hr = F32_SUBLANE_TILE
    i = pl.program_id(1)
    keep_prev = (i > 0).astype(F32)
    keep_next = (i < pl.num_programs(1) - 1).astype(F32)
    xn = _rms(x_ref[0], g_ref[...]).astype(BF16)
    halo = _rms(jnp.concatenate([xp_ref[0], xq_ref[0]], axis=0), g_ref[...]).astype(BF16)
    d = D_MODEL
    half = ROT_DIM // 2
    rep = d // LANES
    c_self = jnp.tile(cs_ref[...], (1, rep))
    c_up = jnp.tile(cu_ref[...], (1, rep))
    c_down = jnp.tile(cd_ref[...], (1, rep))

    def qk_norm_rope(t, gain):
        sq = (t * t).astype(BF16)
        ms = jnp.concatenate([_dot(sq[:, c:c + MXU_DIM], ones_ref[...]) for c in range(0, d, MXU_DIM)], axis=1)
        t = t * lax.rsqrt(ms * (1.0 / DIFF_HEAD_DIM) + EPS) * gain
        return t * c_self + pltpu.roll(t, d - half, 1) * c_up + pltpu.roll(t, half, 1) * c_down

    edge = _dot(halo, w_ref[:, 0:d])
    ext = jnp.concatenate([edge[0:hr] * keep_prev, _dot(xn, w_ref[:, 0:d]), edge[hr:2 * hr] * keep_next], axis=0)
    yp_ref[0] = _pooled(length, ext, pw_ref, ps_ref).astype(BF16)
    q = qk_norm_rope(_dot(xn, w_ref[:, d:2 * d]), qg_ref[...])
    q_ref[0] = (q * (LOG2E * DIFF_HEAD_DIM ** -0.5)).astype(BF16)
    k_ref[0] = qk_norm_rope(_dot(xn, w_ref[:, 2 * d:3 * d]), kg_ref[...]).astype(BF16)
    v_ref[0] = _dot(xn, w_ref[:, 3 * d:4 * d]).astype(BF16)


def _cd_in(x, g, w_in, q_norm, k_norm, pool_w, pool_scale):
    nb, length, d = x.shape
    tm = _row_tile(length, 512)
    hr = F32_SUBLANE_TILE
    per = tm // hr
    last = length // hr - 1
    fixed = lambda b, i: (0, 0)
    tile = lambda b, i: (b, i, 0)
    pos = lambda b, i: (i, 0)
    ones_blk = jnp.asarray(np.kron(np.eye(MXU_DIM // DIFF_HEAD_DIM), np.ones((DIFF_HEAD_DIM, DIFF_HEAD_DIM))), BF16)
    qg = jnp.tile(q_norm, d // DIFF_HEAD_DIM).reshape(1, d)
    kg = jnp.tile(k_norm, d // DIFF_HEAD_DIM).reshape(1, d)
    out = jax.ShapeDtypeStruct((nb, length, d), BF16)
    return pl.pallas_call(
        functools.partial(_cd_in_body, length),
        out_shape=(out, out, out, out),
        grid=(nb, length // tm),
        in_specs=[
            pl.BlockSpec((1, hr, d), lambda b, i: (b, jnp.maximum(i * per - 1, 0), 0)),
            pl.BlockSpec((1, tm, d), tile),
            pl.BlockSpec((1, hr, d), lambda b, i: (b, jnp.minimum((i + 1) * per, last), 0)),
            pl.BlockSpec((1, d), fixed),
            pl.BlockSpec(w_in.shape, fixed),
            pl.BlockSpec((MXU_DIM, MXU_DIM), fixed),
            pl.BlockSpec((1, d), fixed),
            pl.BlockSpec((1, d), fixed),
            pl.BlockSpec((tm, LANES),pos),
            pl.BlockSpec((tm, LANES),pos),
            pl.BlockSpec((tm, LANES),pos),
            pl.BlockSpec(pool_w.shape, lambda b, i: (0, 0, 0)),
            pl.BlockSpec((1, d), fixed),
        ],
        out_specs=tuple(pl.BlockSpec((1, tm, d), tile) for _ in range(4)),
        compiler_params=_cparams("parallel", "parallel"),
        name="cd_in",
    )(x, x, x, g.reshape(1, d), w_in, ones_blk, qg, kg, *_rope_tables(length), pool_w, pool_scale.reshape(1, d))


ATTN_Q_TILE = 2048
ATTN_KV_TILE = 4096
ATTN_COL_BLOCK = 1024
ATTN_KEY_BLOCK = 2048
ATTN_SUM_ROWS = 16
LOG2E = math.log2(math.e)


def _diff_attn_body(lambda_init, q_ref, k_ref, v_ref, lam_ref, sub_ref, o_ref, qs_ref, m_ref, acc_ref, s_ref):
    kv = pl.program_id(3)
    tq = q_ref.shape[1]
    tk = k_ref.shape[1]
    vd = DIFF_V_DIM

    @pl.when(kv == 0)
    def _():
        q = q_ref[0]
        lane = lax.broadcasted_iota(jnp.int32, q.shape, 1)
        zero = jnp.zeros_like(q)
        qs_ref[0:tq] = jnp.where(lane < DIFF_HEAD_DIM, q, zero)
        qs_ref[tq:2 * tq] = jnp.where(lane >= DIFF_HEAD_DIM, q, zero)
        m_ref[...] = jnp.full_like(m_ref, -jnp.inf)
        acc_ref[...] = jnp.zeros_like(acc_ref)

    vt = jnp.concatenate([v_ref[0].T, jnp.ones((ATTN_SUM_ROWS, tk), BF16)], axis=0)
    cb = s_ref.shape[2]
    kb = math.gcd(tk, ATTN_KEY_BLOCK)
    blocks = [slice(c, c + cb) for c in range(0, 2 * tq, cb)]

    def scores(n):
        top = None
        for r in range(0, tk, kb):
            part = _dot_nt(k_ref[0, r:r + kb], qs_ref[blocks[n]])
            s_ref[n % 2, r:r + kb] = part
            part = jnp.max(part, axis=0, keepdims=True)
            top = part if top is None else jnp.maximum(top, part)
        return top

    top_next = scores(0)
    for n, cols in enumerate(blocks):
        top = top_next
        if n + 1 < len(blocks):
            top_next = scores(n + 1)
        m_prev = m_ref[:, cols]
        m_next = jnp.maximum(m_prev, top)
        alpha = jnp.exp2(m_prev - m_next)
        acc = acc_ref[:, cols] * alpha[0:1]
        for r in range(0, tk, kb):
            p = jnp.exp2(s_ref[n % 2, r:r + kb] - m_next[0:1]).astype(BF16)
            acc = acc + _dot(vt[:, r:r + kb], p)
        acc_ref[:, cols] = acc
        m_ref[:, cols] = m_next

    @pl.when(kv == pl.num_programs(3) - 1)
    def _():
        acc = acc_ref[...]
        o = acc[0:vd] / acc[vd:vd + 1]
        lv = lam_ref[...]
        lam = (jnp.exp(jnp.sum(lv[0:1] * lv[1:2], axis=-1, keepdims=True))
               - jnp.exp(jnp.sum(lv[2:3] * lv[3:4], axis=-1, keepdims=True)) + lambda_init)
        diff = (o[:, 0:tq] - lam * o[:, tq:2 * tq]).T
        o_ref[0] = (_rms(diff, sub_ref[...]) * (1.0 - lambda_init)).astype(o_ref.dtype)


def _diff_attn(q, k, v, lam_vecs, sub_norm, lambda_init):
    nb, length, d = q.shape
    tq = _row_tile(length, ATTN_Q_TILE)
    tk = _row_tile(length, ATTN_KV_TILE)
    hw = 2 * DIFF_HEAD_DIM
    assert hw == LANES and DIFF_V_DIM == LANES
    return pl.pallas_call(
        functools.partial(_diff_attn_body, lambda_init),
        out_shape=jax.ShapeDtypeStruct((nb, length, d), BF16),
        grid=(nb, DIFF_HEADS, length // tq, length // tk),
        in_specs=[
            pl.BlockSpec((1, tq, hw), lambda b, h, i, j: (b, i, h)),
            pl.BlockSpec((1, tk, hw), lambda b, h, i, j: (b, j, h)),
            pl.BlockSpec((1, tk, DIFF_V_DIM), lambda b, h, i, j: (b, j, h)),
            pl.BlockSpec((4, DIFF_HEAD_DIM), lambda b, h, i, j: (0, 0)),
            pl.BlockSpec((1, DIFF_V_DIM), lambda b, h, i, j: (0, 0)),
        ],
        out_specs=pl.BlockSpec((1, tq, DIFF_V_DIM), lambda b, h, i, j: (b, i, h)),
        scratch_shapes=[
            pltpu.VMEM((2 * tq, hw), BF16),
            pltpu.VMEM((F32_SUBLANE_TILE, 2 * tq), F32),
            pltpu.VMEM((DIFF_V_DIM + ATTN_SUM_ROWS, 2 * tq), F32),
            pltpu.VMEM((2, tk, min(ATTN_COL_BLOCK, 2 * tq)), F32),
        ],
        compiler_params=_cparams("parallel", "parallel", "parallel", "arbitrary"),
        name="diff_attn",
    )(q, k, v, lam_vecs, sub_norm.reshape(1, DIFF_V_DIM))


def _cd_out_body(x_ref, yp_ref, o_ref_in, w_ref, o_ref):
    d = D_MODEL
    o_ref[...] = x_ref[...] + _dot(yp_ref[...], w_ref[0:d, :]) + _dot(o_ref_in[...], w_ref[d:2 * d, :])


def _cd_out(x, y_pool, o, w_out):
    t, d = x.shape
    tm = _row_tile(t, 512)
    row = lambda i: (i, 0)
    return pl.pallas_call(
        _cd_out_body,
        out_shape=jax.ShapeDtypeStruct((t, d), F32),
        grid=(t // tm,),
        in_specs=[pl.BlockSpec((tm, d), row), pl.BlockSpec((tm, d), row), pl.BlockSpec((tm, d), row),
                  pl.BlockSpec(w_out.shape, lambda i: (0, 0))],
        out_specs=pl.BlockSpec((tm, d), row),
        compiler_params=_cparams("parallel"),
        name="cd_out",
    )(x, y_pool, o, w_out)


def _mem_kv_body(m_ref, g_ref, w_ref, kg_ref, k_ref, v_ref):
    d = D_MODEL
    mn = _rms(m_ref[0], g_ref[...]).astype(BF16)
    k = _dot(mn, w_ref[:, 0:d])
    hd = CROSS_HEAD_DIM
    k_ref[0] = jnp.concatenate(
        [_rms(k[:, h * hd:(h + 1) * hd], kg_ref[...]) for h in range(CROSS_HEADS)], axis=1).astype(BF16)
    v_ref[0] = _dot(mn, w_ref[:, d:2 * d]).astype(BF16)


def _mem_kv(mem, g, w_kv, k_norm):
    nb, n_mem, d = mem.shape
    fixed = lambda b: (0, 0)
    out = jax.ShapeDtypeStruct((nb, n_mem, d), BF16)
    blk = pl.BlockSpec((1, n_mem, d), lambda b: (b, 0, 0))
    return pl.pallas_call(
        _mem_kv_body,
        out_shape=(out, out),
        grid=(nb,),
        in_specs=[blk, pl.BlockSpec((1, d), fixed), pl.BlockSpec(w_kv.shape, fixed),
                  pl.BlockSpec((1, CROSS_HEAD_DIM), fixed)],
        out_specs=(blk, blk),
        compiler_params=_cparams("parallel"),
        name="cross_mem_kv",
    )(mem, g.reshape(1, d), w_kv, k_norm.reshape(1, CROSS_HEAD_DIM))


def _cross_body(x_ref, g_ref, wq_ref, qg_ref, k_ref, v_ref, wo_ref, *rest):
    hd = CROSS_HEAD_DIM
    x = x_ref[0]
    o_ref = rest[-1]
    if len(rest) == 4:
        yp_ref, at_ref, wm_ref = rest[:3]
        x = x + _dot(yp_ref[0], wm_ref[0:D_MODEL, :]) + _dot(at_ref[0], wm_ref[D_MODEL:2 * D_MODEL, :])
    q = _dot(_rms(x, g_ref[...]).astype(BF16), wq_ref[...])
    heads = []
    for h in range(CROSS_HEADS):
        cols = slice(h * hd, (h + 1) * hd)
        qh = (_rms(q[:, cols], qg_ref[...]) * (hd ** -0.5)).astype(BF16)
        s = _dot_nt(qh, k_ref[0, :, cols])
        p = jnp.exp(s - jnp.max(s, axis=-1, keepdims=True))
        p = p / jnp.sum(p, axis=-1, keepdims=True)
        heads.append(_dot(p.astype(BF16), v_ref[0, :, cols]).astype(BF16))
    o_ref[0] = x + _dot(jnp.concatenate(heads, axis=1), wo_ref[...])


def _cross(x, g, w_q, q_norm, k, v, w_o, mixer_out=None):
    nb, length, d = x.shape
    n_mem = k.shape[1]
    tm = _row_tile(length, 512)
    fixed = lambda b, i: (0, 0)
    tile = lambda b, i: (b, i, 0)
    per_batch = lambda b, i: (b, 0, 0)
    extra_specs, extra_args = [], ()
    if mixer_out is not None:
        extra_specs = [pl.BlockSpec((1, tm, d), tile), pl.BlockSpec((1, tm, d), tile),
                       pl.BlockSpec(mixer_out[2].shape, fixed)]
        extra_args = tuple(mixer_out)
    return pl.pallas_call(
        _cross_body,
        out_shape=jax.ShapeDtypeStruct((nb, length, d), F32),
        grid=(nb, length // tm),
        in_specs=[
            pl.BlockSpec((1, tm, d), tile),
            pl.BlockSpec((1, d), fixed),
            pl.BlockSpec((d, d), fixed),
            pl.BlockSpec((1, CROSS_HEAD_DIM), fixed),
            pl.BlockSpec((1, n_mem, d), per_batch),
            pl.BlockSpec((1, n_mem, d), per_batch),
            pl.BlockSpec((d, d), fixed),
        ] + extra_specs,
        out_specs=pl.BlockSpec((1, tm, d), tile),
        compiler_params=_cparams("parallel", "parallel"),
        name="cross_attn",
    )(x, g.reshape(1, d), w_q, q_norm.reshape(1, CROSS_HEAD_DIM), k, v, w_o, *extra_args)


def _lambda_init(layer_idx):
    return 0.8 - 0.6 * math.exp(-0.3 * layer_idx)


def _mixer_ab(x, p, i):
    nb, length, d = x.shape
    t = nb * length
    w_in = p['ab_w_in'][i]
    n_main = D_MODEL + SSM_INNER + SSM_CONV_CH
    w_dt = jnp.pad(w_in[:, n_main:], ((0, 0), (0, LANES - 2 * SSM_HEADS)))
    uf, z, xbc, dt = _ab_in(x.reshape(t, d), p['mix_norm_l'], w_in[:, :n_main].astype(BF16), w_dt.astype(BF16),
                            p['ab_conv_w'][i], p['ab_conv_b'][i], length)
    y_four = _fourier(uf, nb, length)
    y_fw, y_bw = _ssd(xbc.reshape(nb, length, SSM_CONV_CH), dt.reshape(nb, length, LANES), p['ab_dt_bias'][i],
                      p['ab_a_log'][i], p['ab_d_skip'][i])
    out = _ab_out(x.reshape(t, d), y_four, y_fw.reshape(t, SSM_INNER), y_bw.reshape(t, SSM_INNER), z,
                  p['ab_gate_norm'][i], p['ab_w_out'][i].astype(BF16))
    return out.reshape(nb, length, d)


def _mixer_cd(x, p, i, layer_idx):
    nb, length, d = x.shape
    t = nb * length
    y_pool, q, k, v = _cd_in(x, p['mix_norm_l'], p['cd_w_in'][i].astype(BF16), p['cd_q_norm'][i], p['cd_k_norm'][i],
                             p['cd_pool_w'][i].astype(BF16), p['cd_pool_scale'][i])
    lam_vecs = jnp.stack([p['cd_lambda_q1'][i], p['cd_lambda_k1'][i], p['cd_lambda_q2'][i], p['cd_lambda_k2'][i]])
    o = _diff_attn(q, k, v, lam_vecs, p['cd_sub_norm'][i], _lambda_init(layer_idx))
    return x, (y_pool, o, p['cd_w_out'][i].astype(BF16))


def _trunk(x, mem, p, depth):
    nb, length, d = x.shape
    t = nb * length
    for l in range(depth):
        x = _ffn(x.reshape(t, d), p['ffn1_norm'][l], p['ffn1_w_gate'][l].astype(BF16),
                 p['ffn1_w_up'][l].astype(BF16), p['ffn1_w_down'][l].astype(BF16)).reshape(nb, length, d)
        pl_ = dict(p, mix_norm_l=p['mix_norm'][l])
        mixer_out = None
        if l % 2 == 0:
            x = _mixer_ab(x, pl_, l // 2)
        else:
            x, mixer_out = _mixer_cd(x, pl_, l // 2, l)
        mk, mv = _mem_kv(mem, p['cross_mem_norm'][l], p['cross_w_kv'][l].astype(BF16), p['cross_k_norm'][l])
        x = _cross(x, p['cross_norm'][l], p['cross_w_q'][l].astype(BF16), p['cross_q_norm'][l], mk, mv,
                   p['cross_w_o'][l].astype(BF16), mixer_out)
        x = _ffn(x.reshape(t, d), p['ffn2_norm'][l], p['ffn2_w_gate'][l].astype(BF16),
                 p['ffn2_w_up'][l].astype(BF16), p['ffn2_w_down'][l].astype(BF16)).reshape(nb, length, d)
    return x


def kernel(x_prompt, x_sample, mem_prompt, mem_sample, ffn1_norm, ffn1_w_gate, ffn1_w_up, ffn1_w_down, mix_norm, ab_w_in, ab_conv_w, ab_conv_b, ab_dt_bias, ab_a_log, ab_d_skip, ab_gate_norm, ab_w_out, cd_w_in, cd_pool_w, cd_pool_scale, cd_q_norm, cd_k_norm, cd_lambda_q1, cd_lambda_k1, cd_lambda_q2, cd_lambda_k2, cd_sub_norm, cd_w_out, cross_norm, cross_mem_norm, cross_w_q, cross_w_kv, cross_q_norm, cross_k_norm, cross_w_o, ffn2_norm, ffn2_w_gate, ffn2_w_up, ffn2_w_down):
    p = {
        'ffn1_norm': ffn1_norm, 'ffn1_w_gate': ffn1_w_gate, 'ffn1_w_up': ffn1_w_up, 'ffn1_w_down': ffn1_w_down,
        'mix_norm': mix_norm,
        'ab_w_in': ab_w_in, 'ab_conv_w': ab_conv_w, 'ab_conv_b': ab_conv_b, 'ab_dt_bias': ab_dt_bias,
        'ab_a_log': ab_a_log, 'ab_d_skip': ab_d_skip, 'ab_gate_norm': ab_gate_norm, 'ab_w_out': ab_w_out,
        'cd_w_in': cd_w_in, 'cd_pool_w': cd_pool_w, 'cd_pool_scale': cd_pool_scale, 'cd_q_norm': cd_q_norm,
        'cd_k_norm': cd_k_norm, 'cd_lambda_q1': cd_lambda_q1, 'cd_lambda_k1': cd_lambda_k1,
        'cd_lambda_q2': cd_lambda_q2, 'cd_lambda_k2': cd_lambda_k2, 'cd_sub_norm': cd_sub_norm,
        'cd_w_out': cd_w_out,
        'cross_norm': cross_norm, 'cross_mem_norm': cross_mem_norm, 'cross_w_q': cross_w_q,
        'cross_w_kv': cross_w_kv, 'cross_q_norm': cross_q_norm, 'cross_k_norm': cross_k_norm,
        'cross_w_o': cross_w_o,
        'ffn2_norm': ffn2_norm, 'ffn2_w_gate': ffn2_w_gate, 'ffn2_w_up': ffn2_w_up, 'ffn2_w_down': ffn2_w_down,
    }
    depth = ffn1_norm.shape[0]
    return (_trunk(x_prompt, mem_prompt, p, depth), _trunk(x_sample, mem_sample, p, depth))
```
